```python
import jax, jax.numpy as jnp
from jax import lax
import numpy as np

D_MODEL = 1024
BATCH = 4
SEQ = 4096
DEPTH = 2

GRID_W = 64
CTX_LEN = 256
EPS = 1e-6
F32 = jnp.float32
NEG_INF = -1e30
F_FLOOR = 1e-30
NA_HEADS = 8
NA_HEAD_DIM = 64
NA_WIDTH = NA_HEADS * NA_HEAD_DIM
NA_WIN_ROWS = 8
NA_WIN_COLS = 16
NA_QCOL_BLOCK = 16
NA_KCOL_BAND = NA_QCOL_BLOCK + NA_WIN_COLS
HG_HEADS = 4
HG_HEAD_DIM = 128
HG_WIDTH = HG_HEADS * HG_HEAD_DIM
ML_HEADS = 4
ML_HEAD_DIM = 128
ML_WIDTH = ML_HEADS * ML_HEAD_DIM
ML_GATE_COLS = 2 * 2 * ML_HEADS
ROPE_BASE = 10000.0
CHUNK = 64
N_BRANCH = 3
BRANCH_WIDTH = 512
FFN_DIM = 2816
N_EXPERTS = 8
TOP_K = 2
EXPERT_DIM = 3584
MOE_BLOCK = 512

IN_NAMES = ('na_q', 'na_k', 'na_v', 'hg_q', 'hg_f_fwd', 'hg_f_bwd', 'hg_i', 'hg_g',
            'ml_q', 'ml_k', 'ml_v', 'ml_o', 'ml_gates', 'branch_gates')
IN_SIZES = (NA_WIDTH, NA_WIDTH, NA_WIDTH, HG_WIDTH, HG_WIDTH, HG_WIDTH, HG_WIDTH, HG_WIDTH,
            ML_WIDTH, ML_WIDTH, ML_WIDTH, ML_WIDTH, ML_GATE_COLS, N_BRANCH * D_MODEL)
IN_WIDTH = 3 * NA_WIDTH + 5 * HG_WIDTH + 4 * ML_WIDTH + ML_GATE_COLS + N_BRANCH * D_MODEL

kernel_name = 'hybrid_na_hgrn2_mlstm_moe_dit'


def rms_norm(x, w):
    xf = x.astype(F32)
    y = xf * lax.rsqrt(jnp.mean(xf * xf, axis=-1, keepdims=True) + EPS)
    return (y * w.astype(F32)).astype(x.dtype)


def modulate(h, shift, scale):
    return h * (1 + scale) + shift


def split_heads(t, n):
    return t.reshape(*t.shape[:-1], n, t.shape[-1] // n)


def split_in(proj):
    offs = np.cumsum(IN_SIZES)[:-1].tolist()
    return dict(zip(IN_NAMES, jnp.split(proj, offs, axis=-1)))


def head_rms(o, w):
    o = o * lax.rsqrt(jnp.mean(o * o, axis=-1, keepdims=True) + EPS)
    return o.reshape(*o.shape[:2], -1) * w.astype(F32)


def axial_rope(t, rows_pos, cols_pos):
    half = t.shape[-1] // 2
    n_freq = half // 2
    inv_freq = ROPE_BASE ** (-jnp.arange(n_freq, dtype=F32) / n_freq)

    def rotate(part, pos):
        ang = pos.astype(F32)[:, None] * inv_freq
        cos = jnp.cos(ang)[None, :, None, :]
        sin = jnp.sin(ang)[None, :, None, :]
        p1, p2 = part[..., :n_freq], part[..., n_freq:]
        return jnp.concatenate([p1 * cos - p2 * sin, p1 * sin + p2 * cos], axis=-1)

    return jnp.concatenate([rotate(t[..., :half], rows_pos), rotate(t[..., half:], cols_pos)], axis=-1)


def to_chunks(a):
    B, T, H, d = a.shape
    return a.reshape(B, T // CHUNK, CHUNK, H, d).transpose(1, 0, 3, 2, 4)


def gate_chunks(a):
    B, T, H = a.shape
    return a.reshape(B, T // CHUNK, CHUNK, H).transpose(1, 0, 3, 2)


def from_chunks(a):
    n, B, H, L, d = a.shape
    return a.transpose(1, 0, 3, 2, 4).reshape(B, n * L, H, d)


def na_latent(q, k, v, k_ctx, v_ctx, rpb):
    B, S, H, dh = q.shape
    rows = S // GRID_W
    wr = min(NA_WIN_ROWS, rows)
    n_cb = GRID_W // NA_QCOL_BLOCK
    scale = dh ** -0.5
    qcol = np.arange(GRID_W).reshape(n_cb, NA_QCOL_BLOCK)
    band_start = np.clip(np.arange(n_cb) * NA_QCOL_BLOCK - NA_WIN_COLS // 2, 0, GRID_W - NA_KCOL_BAND)
    kcol = band_start[:, None] + np.arange(NA_KCOL_BAND)
    win_start = np.clip(qcol - NA_WIN_COLS // 2, 0, GRID_W - NA_WIN_COLS)
    col_mask = (kcol[:, None, :] >= win_start[..., None]) & (kcol[:, None, :] < win_start[..., None] + NA_WIN_COLS)
    dc = np.clip(kcol[:, None, :] - qcol[..., None] + NA_WIN_COLS - 1, 0, 2 * NA_WIN_COLS - 2)
    rpb_c = rpb.astype(F32)[:, :, dc]
    qg = q.reshape(B, rows, GRID_W, H, dh)
    kband = k.reshape(B, rows, GRID_W, H, dh)[:, :, kcol]
    vband = v.reshape(B, rows, GRID_W, H, dh)[:, :, kcol]
    mask = jnp.asarray(col_mask)[:, :, None, :]

    def row_block(r):
        r0 = jnp.clip(r - NA_WIN_ROWS // 2, 0, rows - wr)
        q_r = lax.dynamic_index_in_dim(qg, r, axis=1, keepdims=False).reshape(B, n_cb, NA_QCOL_BLOCK, H, dh)
        k_r = lax.dynamic_slice_in_dim(kband, r0, wr, axis=1)
        v_r = lax.dynamic_slice_in_dim(vband, r0, wr, axis=1)
        s_loc = jnp.einsum('bcqhd,bwckhd->bhcqwk', q_r, k_r, preferred_element_type=F32) * scale
        dr = r0 + jnp.arange(wr) - r + NA_WIN_ROWS - 1
        bias = jnp.take(rpb_c, dr, axis=1).transpose(0, 2, 3, 1, 4)
        s_loc = jnp.where(mask, s_loc + bias, NEG_INF).reshape(B, H, n_cb, NA_QCOL_BLOCK, wr * NA_KCOL_BAND)
        s_ctx = jnp.einsum('bcqhd,bjhd->bhcqj', q_r, k_ctx, preferred_element_type=F32) * scale
        p = jax.nn.softmax(jnp.concatenate([s_loc, s_ctx], axis=-1), axis=-1).astype(v.dtype)
        p_loc = p[..., :wr * NA_KCOL_BAND].reshape(B, H, n_cb, NA_QCOL_BLOCK, wr, NA_KCOL_BAND)
        p_ctx = p[..., wr * NA_KCOL_BAND:]
        o = jnp.einsum('bhcqwk,bwckhd->bcqhd', p_loc, v_r) + jnp.einsum('bhcqj,bjhd->bcqhd', p_ctx, v_ctx)
        return o.reshape(B, GRID_W, H, dh)

    out = lax.map(row_block, jnp.arange(rows))
    return out.transpose(1, 0, 2, 3, 4).reshape(B, S, H * dh)


def context_attention(q, k, v):
    B, L, H, dh = q.shape
    s = jnp.einsum('bqhd,bkhd->bhqk', q, k, preferred_element_type=F32) * dh ** -0.5
    p = jax.nn.softmax(s, axis=-1).astype(v.dtype)
    return jnp.einsum('bhqk,bkhd->bqhd', p, v).reshape(B, L, H * dh)


def bidirectional(scan_fn, ctx_dirs, lat_dirs, init):
    outs_c, outs_l = [], []
    for d in range(2):
        rev = (lambda a: a) if d == 0 else (lambda a: jnp.flip(a, axis=1))
        o_c, state = scan_fn(*[rev(a) for a in ctx_dirs[d]], init)
        o_l, _ = scan_fn(*[rev(a) for a in lat_dirs[d]], state)
        outs_c.append(rev(o_c))
        outs_l.append(rev(o_l))
    return outs_c[0] + outs_c[1], outs_l[0] + outs_l[1]


def hgrn2_chunk_scan(q, k, v, log_f, state):
    tri = jnp.tril(jnp.ones((CHUNK, CHUNK), dtype=bool))[:, :, None]

    def step(S, inp):
        qc, kc, vc, gc = inp
        G = jnp.cumsum(gc, axis=2)
        decay = jnp.exp(jnp.where(tri, G[:, :, :, None, :] - G[:, :, None, :, :], NEG_INF))
        A = jnp.einsum('bhtd,bhsd,bhtsd->bhts', qc, kc, decay)
        o = jnp.einsum('bhts,bhsv->bhtv', A, vc) + jnp.einsum('bhtd,bhdv->bhtv', qc * jnp.exp(G), S)
        G_end = G[:, :, -1, :]
        S_new = jnp.exp(G_end)[..., None] * S + jnp.einsum('bhsd,bhsv->bhdv', kc * jnp.exp(G_end[:, :, None, :] - G), vc)
        return S_new, o

    S_fin, o = lax.scan(step, state, (to_chunks(q), to_chunks(k), to_chunks(v), to_chunks(log_f)))
    return from_chunks(o), S_fin


def hgrn2_branch(p_ctx, p_lat, lower, norm_w):
    def prep(p):
        q = jax.nn.silu(split_heads(p['hg_q'], HG_HEADS).astype(F32))
        v = split_heads(p['hg_i'], HG_HEADS).astype(F32)
        per_dir = []
        for d, name in enumerate(('hg_f_fwd', 'hg_f_bwd')):
            z = split_heads(p[name], HG_HEADS).astype(F32)
            lb = lower[d].reshape(HG_HEADS, HG_HEAD_DIM)
            k = (1.0 - lb) * jax.nn.sigmoid(-z)
            f = lb + (1.0 - lb) * jax.nn.sigmoid(z)
            log_f = jnp.log(jnp.maximum(f, F_FLOOR))
            per_dir.append((q, k, v, log_f))
        return per_dir

    B = p_lat['hg_q'].shape[0]
    init = jnp.zeros((B, HG_HEADS, HG_HEAD_DIM, HG_HEAD_DIM), F32)
    o_c, o_l = bidirectional(hgrn2_chunk_scan, prep(p_ctx), prep(p_lat), init)

    def readout(o, p):
        return head_rms(o, norm_w) * jax.nn.silu(p['hg_g'].astype(F32))

    return readout(o_c, p_ctx), readout(o_l, p_lat)


def mlstm_chunk_scan(q, k, v, log_i, log_f, state):
    tri = jnp.tril(jnp.ones((CHUNK, CHUNK), dtype=bool))

    def step(carry, inp):
        C, nv, m = carry
        qc, kc, vc, ic, fc = inp
        b = jnp.cumsum(fc, axis=-1)
        dmat = jnp.where(tri, b[..., :, None] - b[..., None, :] + ic[..., None, :], NEG_INF)
        inter = b + m[..., None]
        m_t = jnp.maximum(inter, jnp.max(dmat, axis=-1))
        w_inter = jnp.exp(inter - m_t)
        p = jnp.exp(dmat - m_t[..., None]) * jnp.einsum('bhtd,bhsd->bhts', qc, kc)
        num = jnp.einsum('bhts,bhsv->bhtv', p, vc) + w_inter[..., None] * jnp.einsum('bhtd,bhdv->bhtv', qc, C)
        den = jnp.sum(p, axis=-1) + w_inter * jnp.einsum('bhtd,bhd->bht', qc, nv)
        h = num / jnp.maximum(jnp.abs(den), jnp.exp(-m_t))[..., None]
        b_end = b[..., -1]
        e = b_end[..., None] - b + ic
        m_new = jnp.maximum(b_end + m, jnp.max(e, axis=-1))
        w_old = jnp.exp(b_end + m - m_new)
        w_s = jnp.exp(e - m_new[..., None])
        C_new = w_old[..., None, None] * C + jnp.einsum('bhs,bhsd,bhsv->bhdv', w_s, kc, vc)
        n_new = w_old[..., None] * nv + jnp.einsum('bhs,bhsd->bhd', w_s, kc)
        return (C_new, n_new, m_new), h

    xs = (to_chunks(q), to_chunks(k), to_chunks(v), gate_chunks(log_i), gate_chunks(log_f))
    state_out, h = lax.scan(step, state, xs)
    return from_chunks(h), state_out


def mlstm_branch(p_ctx, p_lat, gate_b, norm_w, rows_pos, cols_pos):
    def prep(p, use_rope):
        q = split_heads(p['ml_q'], ML_HEADS).astype(F32)
        k = split_heads(p['ml_k'], ML_HEADS).astype(F32) * ML_HEAD_DIM ** -0.5
        v = split_heads(p['ml_v'], ML_HEADS).astype(F32)
        if use_rope:
            q = axial_rope(q, rows_pos, cols_pos)
            k = axial_rope(k, rows_pos, cols_pos)
        B, T = q.shape[:2]
        g = p['ml_gates'].astype(F32).reshape(B, T, 2, 2, ML_HEADS) + gate_b.astype(F32)
        return [(q, k, v, g[:, :, d, 0], jax.nn.log_sigmoid(g[:, :, d, 1])) for d in range(2)]

    B = p_lat['ml_q'].shape[0]
    init = (jnp.zeros((B, ML_HEADS, ML_HEAD_DIM, ML_HEAD_DIM), F32),
            jnp.zeros((B, ML_HEADS, ML_HEAD_DIM), F32),
            jnp.zeros((B, ML_HEADS), F32))
    h_c, h_l = bidirectional(mlstm_chunk_scan, prep(p_ctx, False), prep(p_lat, True), init)

    def readout(h, p):
        return jax.nn.sigmoid(p['ml_o'].astype(F32)) * head_rms(h, norm_w)

    return readout(h_c, p_ctx), readout(h_l, p_lat)


def gated_merge(branches, gate_raw, w_branch, w_out):
    dt = gate_raw.dtype
    gates = jnp.split(jax.nn.sigmoid(gate_raw.astype(F32)).astype(dt), N_BRANCH, axis=-1)
    y = gates[0] * (branches[0].astype(dt) @ w_branch[0])
    for i in range(1, N_BRANCH):
        y = y + gates[i] * (branches[i].astype(dt) @ w_branch[i])
    return y @ w_out


def hybrid_mixer(a_lat, a_ctx, w_in, na_rpb, hg_lower, hg_norm_w, ml_gate_b, ml_norm_w, w_branch, w_out,
                 rows_pos, cols_pos, need_ctx_out):
    pl = split_in(a_lat @ w_in)
    pc = split_in(a_ctx @ w_in)
    nh = lambda t: split_heads(t, NA_HEADS)
    na_l = na_latent(nh(pl['na_q']), nh(pl['na_k']), nh(pl['na_v']), nh(pc['na_k']), nh(pc['na_v']), na_rpb)
    hg_c, hg_l = hgrn2_branch(pc, pl, hg_lower, hg_norm_w)
    ml_c, ml_l = mlstm_branch(pc, pl, ml_gate_b, ml_norm_w, rows_pos, cols_pos)
    y_lat = gated_merge((na_l, hg_l, ml_l), pl['branch_gates'], w_branch, w_out)
    if not need_ctx_out:
        return y_lat, None
    na_c = context_attention(nh(pc['na_q']), nh(pc['na_k']), nh(pc['na_v']))
    y_ctx = gated_merge((na_c, hg_c, ml_c), pc['branch_gates'], w_branch, w_out)
    return y_lat, y_ctx


def swiglu(h, w_up, w_down):
    a, u = jnp.split(h @ w_up, 2, axis=-1)
    return (jax.nn.silu(a) * u) @ w_down


def moe_swiglu(h, w_router, w_up, w_down):
    B, T, D = h.shape
    x = h.reshape(B * T, D)
    n_assign = B * T * TOP_K
    logits = jnp.einsum('nd,de->ne', x, w_router, preferred_element_type=F32)
    top_v, top_e = lax.top_k(logits, TOP_K)
    top_w = jax.nn.softmax(top_v, axis=-1)
    flat_e = top_e.reshape(-1)
    flat_tok = jnp.arange(n_assign, dtype=jnp.int32) // TOP_K
    flat_w = top_w.reshape(-1)
    order = jnp.argsort(flat_e)
    e_sorted = flat_e[order]
    counts = jnp.bincount(flat_e, length=N_EXPERTS)
    padded = (counts + MOE_BLOCK - 1) // MOE_BLOCK * MOE_BLOCK
    pad_end = jnp.cumsum(padded)
    pad_start = pad_end - padded
    grp_start = jnp.cumsum(counts) - counts
    dest = pad_start[e_sorted] + jnp.arange(n_assign) - grp_start[e_sorted]
    n_blocks = -(-(n_assign + N_EXPERTS * (MOE_BLOCK - 1)) // MOE_BLOCK)
    n_slots = n_blocks * MOE_BLOCK
    slot_tok = jnp.zeros((n_slots,), jnp.int32).at[dest].set(flat_tok[order])
    slot_w = jnp.zeros((n_slots,), F32).at[dest].set(flat_w[order])
    block_e = jnp.minimum(jnp.searchsorted(pad_end, jnp.arange(n_blocks) * MOE_BLOCK, side='right'), N_EXPERTS - 1)

    def block_fn(args):
        tok, w, e = args
        a, u = jnp.split(x[tok] @ w_up[e], 2, axis=-1)
        return ((jax.nn.silu(a) * u) @ w_down[e]) * w[:, None].astype(x.dtype)

    y = lax.map(block_fn, (slot_tok.reshape(n_blocks, MOE_BLOCK), slot_w.reshape(n_blocks, MOE_BLOCK), block_e))
    out = jnp.zeros_like(x).at[slot_tok].add(y.reshape(n_slots, D))
    return out.reshape(B, T, D)


def setup_inputs(seed: int = 0) -> dict:
    key = jax.random.key(seed)
    ks = jax.random.split(key, 24)
    D = D_MODEL
    n_dense = (DEPTH + 1) // 2
    n_moe = DEPTH // 2
    nrm = lambda k, shape, s: jax.random.normal(k, shape, F32) * s
    return {
        'x': nrm(ks[0], (BATCH, SEQ, D), 1.0),
        'c': nrm(ks[1], (BATCH, D), 1.0),
        'ctx': nrm(ks[2], (BATCH, CTX_LEN, D), 1.0),
        'c_ctx': nrm(ks[3], (D,), 1.0),
        'mod_w': nrm(ks[4], (DEPTH, D, 6 * D), 0.5 * D ** -0.5),
        'mod_b': nrm(ks[5], (DEPTH, 6 * D), 0.02),
        'norm1_w': 1.0 + nrm(ks[6], (DEPTH, D), 0.05),
        'w_in': nrm(ks[7], (DEPTH, D, IN_WIDTH), D ** -0.5),
        'na_rpb': nrm(ks[8], (DEPTH, NA_HEADS, 2 * NA_WIN_ROWS - 1, 2 * NA_WIN_COLS - 1), 0.1),
        'hg_lb': 1.0 + nrm(ks[9], (DEPTH, 2, HG_WIDTH), 0.5),
        'hg_norm_w': 1.0 + nrm(ks[10], (DEPTH, HG_WIDTH), 0.05),
        'ml_gate_b': nrm(ks[11], (DEPTH, 2, 2, ML_HEADS), 0.1) + jnp.array([0.0, 3.0], F32)[:, None],
        'ml_norm_w': 1.0 + nrm(ks[12], (DEPTH, ML_WIDTH), 0.05),
        'w_branch': nrm(ks[13], (DEPTH, N_BRANCH, BRANCH_WIDTH, D), BRANCH_WIDTH ** -0.5),
        'w_out': nrm(ks[14], (DEPTH, D, D), D ** -0.5),
        'norm2_w': 1.0 + nrm(ks[15], (DEPTH, D), 0.05),
        'ffn_w_up': nrm(ks[16], (n_dense, D, 2 * FFN_DIM), D ** -0.5),
        'ffn_w_down': nrm(ks[17], (n_dense, FFN_DIM, D), FFN_DIM ** -0.5),
        'moe_router': nrm(ks[18], (n_moe, D, N_EXPERTS), D ** -0.5),
        'moe_w_up': nrm(ks[19], (n_moe, N_EXPERTS, D, 2 * EXPERT_DIM), D ** -0.5),
        'moe_w_down': nrm(ks[20], (n_moe, N_EXPERTS, EXPERT_DIM, D), EXPERT_DIM ** -0.5),
        'final_norm_w': 1.0 + nrm(ks[21], (D,), 0.05),
    }


def reference(x, c, ctx, c_ctx, mod_w, mod_b, norm1_w, w_in, na_rpb, hg_lb, hg_norm_w, ml_gate_b, ml_norm_w,
              w_branch, w_out, norm2_w, ffn_w_up, ffn_w_down, moe_router, moe_w_up, moe_w_down, final_norm_w):
    B, S, D = x.shape
    t = jnp.arange(S, dtype=jnp.int32)
    rows_pos, cols_pos = t // GRID_W, t % GRID_W
    lb_p = jax.nn.softmax(hg_lb.astype(F32), axis=0)
    hg_lower = jnp.cumsum(lb_p, axis=0) - lb_p[0]
    h_lat, h_ctx = x, ctx
    for layer in range(DEPTH):
        last = layer == DEPTH - 1
        mod_lat = (jax.nn.silu(c) @ mod_w[layer] + mod_b[layer]).reshape(B, 1, 6, D)
        mod_ctx = (jax.nn.silu(c_ctx) @ mod_w[layer] + mod_b[layer]).reshape(1, 1, 6, D)
        a_lat = modulate(rms_norm(h_lat, norm1_w[layer]), mod_lat[:, :, 0], mod_lat[:, :, 1])
        a_ctx = modulate(rms_norm(h_ctx, norm1_w[layer]), mod_ctx[:, :, 0], mod_ctx[:, :, 1])
        y_lat, y_ctx = hybrid_mixer(a_lat, a_ctx, w_in[layer], na_rpb[layer], hg_lower[layer], hg_norm_w[layer],
                                    ml_gate_b[layer], ml_norm_w[layer], w_branch[layer], w_out[layer],
                                    rows_pos, cols_pos, not last)
        h_lat = h_lat + mod_lat[:, :, 2] * y_lat
        if not last:
            h_ctx = h_ctx + mod_ctx[:, :, 2] * y_ctx
        f_lat = modulate(rms_norm(h_lat, norm2_w[layer]), mod_lat[:, :, 3], mod_lat[:, :, 4])
        i = layer // 2
        if layer % 2 == 0:
            h_lat = h_lat + mod_lat[:, :, 5] * swiglu(f_lat, ffn_w_up[i], ffn_w_down[i])
        else:
            h_lat = h_lat + mod_lat[:, :, 5] * moe_swiglu(f_lat, moe_router[i], moe_w_up[i], moe_w_down[i])
        if not last:
            f_ctx = modulate(rms_norm(h_ctx, norm2_w[layer]), mod_ctx[:, :, 3], mod_ctx[:, :, 4])
            if layer % 2 == 0:
                h_ctx = h_ctx + mod_ctx[:, :, 5] * swiglu(f_ctx, ffn_w_up[i], ffn_w_down[i])
            else:
                h_ctx = h_ctx + mod_ctx[:, :, 5] * moe_swiglu(f_ctx, moe_router[i], moe_w_up[i], moe_w_down[i])
    return rms_norm(h_lat, final_norm_w)
```

```python
import functools

import numpy as np
import jax
import jax.numpy as jnp
from jax import lax
from jax.experimental import pallas as pl
from jax.experimental.pallas import tpu as pltpu

F32 = jnp.float32
BF16 = jnp.bfloat16

D_MODEL = 1024
BATCH = 4
SEQ = 4096
DEPTH = 2
GRID_W = 64
GRID_H = SEQ // GRID_W
CTX_LEN = 256
EPS = 1e-6
NEG_INF = -1e30
F_FLOOR = 1e-30
NA_HEADS = 8
NA_HEAD_DIM = 64
NA_WIN_ROWS = 8
NA_WIN_COLS = 16
HG_HEADS = 4
ML_HEADS = 4
HEAD_DIM = 128
ML_GATE_COLS = 16
ROPE_BASE = 10000.0
BRANCH_WIDTH = 512
FFN_DIM = 2816
N_EXPERTS = 8
EXPERT_DIM = 3584

N_LAT = BATCH * SEQ
N_CTX = BATCH * CTX_LEN
N_ALL = N_LAT + N_CTX

LANES = 128
VMEM_LIMIT = 56 * 1024 * 1024

MAIN_W = 12 * 512
BG_OFF = MAIN_W
GATE_OFF = MAIN_W + 3 * D_MODEL
PROJ_W = GATE_OFF + 512

TM_PROJ = 1024
TN_PROJ = 512
TM_MERGE = 256
TM_FFN = 1024
TH_FFN = 256
NA_QROWS = 8
NA_KROWS = NA_QROWS + NA_WIN_ROWS - 1
CHUNK = 256
TM_MOE = 512
TH_MOE = 512
N_MOE_BLOCKS = -(-(2 * N_LAT + N_EXPERTS * (TM_MOE - 1)) // TM_MOE)
N_SLOTS = N_MOE_BLOCKS * TM_MOE


def _params(*sem):
    return pltpu.CompilerParams(dimension_semantics=sem, vmem_limit_bytes=VMEM_LIMIT)


def _sigmoid(x):
    return 1.0 / (1.0 + jnp.exp(-x))


def _silu(x):
    return x * _sigmoid(x)


def _dot(a, b):
    return jnp.dot(a, b, preferred_element_type=F32)


def _dot_nt(a, b):
    return lax.dot_general(a, b, (((1,), (1,)), ((), ())), preferred_element_type=F32)


def _split3(x):
    hi = x.astype(BF16)
    r = x - hi.astype(F32)
    mid = r.astype(BF16)
    lo = (r - mid.astype(F32)).astype(BF16)
    return hi, mid, lo


def _sel_dot(sel, x):
    hi, mid, lo = _split3(x)
    return _dot(sel, lo) + _dot(sel, mid) + _dot(sel, hi)


def _sel_dot_nt(sel, x):
    hi, mid, lo = _split3(x)
    return _dot_nt(sel, lo) + _dot_nt(sel, mid) + _dot_nt(sel, hi)


def _dot_sel(x, sel):
    hi, mid, lo = _split3(x)
    return _dot(lo, sel) + _dot(mid, sel) + _dot(hi, sel)


def _norm_mod(x, nw, shift, scale):
    y = x * lax.rsqrt(jnp.mean(x * x, axis=-1, keepdims=True) + EPS) * nw
    return y * (1.0 + scale) + shift


def _mod_kernel(c_ref, w_ref, b_ref, o_ref):
    s = _silu(c_ref[...])
    o_ref[0] = jnp.dot(s, w_ref[0], preferred_element_type=F32, precision=lax.Precision.HIGHEST) + b_ref[0]


def _modulation(c8, mod_w, mod_b):
    tn = 1536
    return pl.pallas_call(
        _mod_kernel,
        grid=(DEPTH, 6 * D_MODEL // tn),
        in_specs=[
            pl.BlockSpec((8, D_MODEL), lambda l, j: (0, 0)),
            pl.BlockSpec((1, D_MODEL, tn), lambda l, j: (l, 0, j)),
            pl.BlockSpec((1, 1, tn), lambda l, j: (l, 0, j)),
        ],
        out_specs=pl.BlockSpec((1, 8, tn), lambda l, j: (l, 0, j)),
        out_shape=jax.ShapeDtypeStruct((DEPTH, 8, 6 * D_MODEL), F32),
        compiler_params=_params("parallel", "parallel"),
        name="modulation",
    )(c8, mod_w, mod_b.reshape(DEPTH, 1, 6 * D_MODEL))


def _inproj_kernel(h_ref, nw_ref, mod_ref, w_ref, o_ref, a_scr):
    @pl.when(pl.program_id(1) == 0)
    def _():
        a = _norm_mod(h_ref[...], nw_ref[...], mod_ref[0, 0:1, :], mod_ref[0, 1:2, :])
        a_scr[...] = a.astype(BF16)

    o_ref[...] = _dot(a_scr[...], w_ref[...])


def _inproj(h_all, nw, mod5, w_perm):
    tiles_per_batch = SEQ // TM_PROJ
    return pl.pallas_call(
        _inproj_kernel,
        grid=(N_ALL // TM_PROJ, PROJ_W // TN_PROJ),
        in_specs=[
            pl.BlockSpec((TM_PROJ, D_MODEL), lambda i, j: (i, 0)),
            pl.BlockSpec((1, D_MODEL), lambda i, j: (0, 0)),
            pl.BlockSpec((1, 6, D_MODEL), lambda i, j: (i // tiles_per_batch, 0, 0)),
            pl.BlockSpec((D_MODEL, TN_PROJ), lambda i, j: (0, j)),
        ],
        out_specs=pl.BlockSpec((TM_PROJ, TN_PROJ), lambda i, j: (i, j)),
        out_shape=jax.ShapeDtypeStruct((N_ALL, PROJ_W), F32),
        scratch_shapes=[pltpu.VMEM((TM_PROJ, D_MODEL), BF16)],
        compiler_params=_params("parallel", "arbitrary"),
        name="inproj",
    )(h_all, nw, mod5, w_perm)


def _na_bias_tables(rpb):
    qr = np.arange(NA_QROWS)[:, None, None, None]
    qc = np.arange(GRID_W)[None, :, None, None]
    kr = np.arange(NA_KROWS)[None, None, :, None]
    kc = np.arange(GRID_W)[None, None, None, :]
    tables = []
    for q0, k0 in ((0, 0), (NA_QROWS, NA_QROWS - NA_WIN_ROWS // 2), (GRID_H - NA_QROWS, GRID_H - NA_KROWS)):
        r = q0 + qr
        krow = k0 + kr
        r0 = np.clip(r - NA_WIN_ROWS // 2, 0, GRID_H - NA_WIN_ROWS)
        ws = np.clip(qc - NA_WIN_COLS // 2, 0, GRID_W - NA_WIN_COLS)
        valid = (krow >= r0) & (krow < r0 + NA_WIN_ROWS) & (kc >= ws) & (kc < ws + NA_WIN_COLS)
        dr = np.clip(krow - r + NA_WIN_ROWS - 1, 0, 2 * NA_WIN_ROWS - 2)
        dc = np.clip(kc - qc + NA_WIN_COLS - 1, 0, 2 * NA_WIN_COLS - 2)
        shape = (NA_QROWS, GRID_W, NA_KROWS, GRID_W)
        flat = (np.broadcast_to(dr, shape) * (2 * NA_WIN_COLS - 1) + np.broadcast_to(dc, shape)).reshape(-1)
        vals = jnp.take(rpb.reshape(NA_HEADS, -1), jnp.asarray(flat, jnp.int32), axis=1)
        vals = jnp.where(jnp.asarray(np.broadcast_to(valid, shape).reshape(-1)), vals, NEG_INF)
        tables.append(vals.reshape(NA_HEADS, NA_QROWS * GRID_W, NA_KROWS * GRID_W))
    return jnp.stack(tables)


def _na_kernel(q_ref, k_ref, v_ref, kc_ref, vc_ref, bias_ref, o_ref):
    blk = pl.program_id(2)
    k0 = jnp.clip(blk * NA_QROWS - NA_WIN_ROWS // 2, 0, GRID_H - NA_KROWS)
    start = pl.multiple_of(k0 * GRID_W, GRID_W)
    nk = NA_KROWS * GRID_W
    outs = []
    for hh in range(2):
        sl = slice(hh * NA_HEAD_DIM, (hh + 1) * NA_HEAD_DIM)
        q = (q_ref[:, sl] * (NA_HEAD_DIM ** -0.5)).astype(BF16)
        k = k_ref[pl.ds(start, nk), sl].astype(BF16)
        v = v_ref[pl.ds(start, nk), sl].astype(BF16)
        kc = kc_ref[:, sl].astype(BF16)
        vc = vc_ref[:, sl].astype(BF16)
        s_loc = _dot_nt(q, k) + bias_ref[0, hh]
        s_ctx = _dot_nt(q, kc)
        m = jnp.maximum(jnp.max(s_loc, axis=-1, keepdims=True), jnp.max(s_ctx, axis=-1, keepdims=True))
        p_loc = jnp.exp(s_loc - m)
        p_ctx = jnp.exp(s_ctx - m)
        den = jnp.sum(p_loc, axis=-1, keepdims=True) + jnp.sum(p_ctx, axis=-1, keepdims=True)
        o = _dot(p_loc.astype(BF16), v) + _dot(p_ctx.astype(BF16), vc)
        outs.append(o / den)
    o_ref[...] = jnp.concatenate(outs, axis=-1)


def _na_latent(proj, bias):
    nq = NA_QROWS * GRID_W
    nblk = GRID_H // NA_QROWS
    ctx_blk0 = N_LAT // CTX_LEN

    def bias_idx(b, hp, blk):
        return (jnp.minimum(blk, 1) + (blk == nblk - 1).astype(jnp.int32), hp, 0, 0)

    return pl.pallas_call(
        _na_kernel,
        grid=(BATCH, NA_HEADS // 2, nblk),
        in_specs=[
            pl.BlockSpec((nq, LANES), lambda b, hp, blk: (b * nblk + blk, hp)),
            pl.BlockSpec((SEQ, LANES), lambda b, hp, blk: (b, 4 + hp)),
            pl.BlockSpec((SEQ, LANES), lambda b, hp, blk: (b, 8 + hp)),
            pl.BlockSpec((CTX_LEN, LANES), lambda b, hp, blk: (ctx_blk0 + b, 4 + hp)),
            pl.BlockSpec((CTX_LEN, LANES), lambda b, hp, blk: (ctx_blk0 + b, 8 + hp)),
            pl.BlockSpec((1, 2, nq, NA_KROWS * GRID_W), bias_idx),
        ],
        out_specs=pl.BlockSpec((nq, LANES), lambda b, hp, blk: (b * nblk + blk, hp)),
        out_shape=jax.ShapeDtypeStruct((N_LAT, NA_HEADS * NA_HEAD_DIM), F32),
        compiler_params=_params("parallel", "parallel", "arbitrary"),
        name="na_latent",
    )(proj, proj, proj, proj, proj, bias)


def _ctx_attn_kernel(q_ref, k_ref, v_ref, o_ref):
    outs = []
    for hh in range(2):
        sl = slice(hh * NA_HEAD_DIM, (hh + 1) * NA_HEAD_DIM)
        q = (q_ref[:, sl] * (NA_HEAD_DIM ** -0.5)).astype(BF16)
        s = _dot_nt(q, k_ref[:, sl].astype(BF16))
        p = jnp.exp(s - jnp.max(s, axis=-1, keepdims=True))
        o = _dot(p.astype(BF16), v_ref[:, sl].astype(BF16))
        outs.append(o / jnp.sum(p, axis=-1, keepdims=True))
    o_ref[...] = jnp.concatenate(outs, axis=-1)


def _ctx_attention(proj):
    ctx_blk0 = N_LAT // CTX_LEN
    return pl.pallas_call(
        _ctx_attn_kernel,
        grid=(BATCH, NA_HEADS // 2),
        in_specs=[
            pl.BlockSpec((CTX_LEN, LANES), lambda b, hp: (ctx_blk0 + b, hp)),
            pl.BlockSpec((CTX_LEN, LANES), lambda b, hp: (ctx_blk0 + b, 4 + hp)),
            pl.BlockSpec((CTX_LEN, LANES), lambda b, hp: (ctx_blk0 + b, 8 + hp)),
        ],
        out_specs=pl.BlockSpec((CTX_LEN, LANES), lambda b, hp: (b, hp)),
        out_shape=jax.ShapeDtypeStruct((N_CTX, NA_HEADS * NA_HEAD_DIM), F32),
        compiler_params=_params("parallel", "parallel"),
        name="ctx_attention",
    )(proj, proj, proj)


N_CTX_CHUNKS = CTX_LEN // CHUNK
N_LAT_CHUNKS = SEQ // CHUNK
N_STEPS = N_CTX_CHUNKS + N_LAT_CHUNKS
N_LEVELS = CHUNK.bit_length() - 1


def _chunk_block(b, s, rev):
    c_ctx = (N_CTX_CHUNKS - 1 - s) if rev else s
    c_lat = (N_LAT_CHUNKS - 1 - (s - N_CTX_CHUNKS)) if rev else (s - N_CTX_CHUNKS)
    ctx_blk = N_LAT // CHUNK + b * N_CTX_CHUNKS + c_ctx
    lat_blk = b * N_LAT_CHUNKS + c_lat
    return jnp.where(s < N_CTX_CHUNKS, ctx_blk, lat_blk)


def _tri_consts():
    i = np.arange(CHUNK)
    low = (i[None, :] <= i[:, None]).astype(np.float32)
    eye = np.eye(CHUNK, dtype=np.float32)
    return jnp.asarray(np.stack([low, low.T, eye]), BF16)


def _level_masks():
    t = np.arange(CHUNK)[:, None]
    s = np.arange(CHUNK)[None, :]
    out = np.zeros((2, N_LEVELS + 1, CHUNK, CHUNK), np.float32)
    for l in range(N_LEVELS):
        pair = ((t ^ s) >> l) == 1
        out[0, l] = pair & (t > s)
        out[1, l] = pair & (t < s)
    out[:, N_LEVELS] = (t == s)
    return jnp.asarray(out)


def _hgrn_direction(q_raw, z, v, lb, st_ref, tri, masks_ref, rev):
    L = CHUNK
    e = jnp.exp(-jnp.abs(z))
    r = 1.0 / (1.0 + e)
    pos = z >= 0
    sig = jnp.where(pos, r, e * r)
    nsig = jnp.where(pos, e * r, r)
    k = (1.0 - lb) * nsig
    logf = jnp.log(jnp.maximum(lb + (1.0 - lb) * sig, F_FLOOR))
    q = _silu(q_raw)
    g = _sel_dot(tri, logf)
    row = lax.broadcasted_iota(jnp.int32, (L, HEAD_DIM), 0)

    st = st_ref[...]
    o = _dot_nt((q * jnp.exp(g)).astype(BF16), st.astype(BF16))

    a = jnp.zeros((L, L), F32)
    bnd = g
    for l in range(N_LEVELS):
        blk = 1 << l
        q_side = ((row & blk) == 0) if rev else ((row & blk) != 0)
        prev_end = pltpu.roll(bnd, (L - blk) if rev else blk, 0)
        dq = jnp.where(q_side, g - prev_end, 0.0)
        dk = jnp.where(q_side, 0.0, bnd - g)
        qb = (q * jnp.exp(dq)).astype(BF16)
        kb = (k * jnp.exp(dk)).astype(BF16)
        a = a + masks_ref[l] * _dot_nt(qb, kb)
        nxt = pltpu.roll(bnd, blk if rev else (L - blk), 0)
        bnd = jnp.where(q_side, bnd, nxt)
    a = a + masks_ref[N_LEVELS] * _dot_nt(q.astype(BF16), k.astype(BF16))
    o = o + _dot(a.astype(BF16), v.astype(BF16))

    kd = k * jnp.exp(bnd - g)
    st_ref[...] = jnp.exp(bnd[0:1, :]) * st + lax.dot_general(
        v.astype(BF16), kd.astype(BF16), (((0,), (0,)), ((), ())), preferred_element_type=F32)
    return o


def _hgrn_kernel(qf_ref, zf_ref, vf_ref, qb_ref, zb_ref, vb_ref, lb_ref, tri_ref, masks_ref,
                 of_ref, ob_ref, stf_ref, stb_ref):
    @pl.when(pl.program_id(2) == 0)
    def _():
        stf_ref[...] = jnp.zeros_like(stf_ref)
        stb_ref[...] = jnp.zeros_like(stb_ref)

    of_ref[...] = _hgrn_direction(qf_ref[...], zf_ref[...], vf_ref[...], lb_ref[0:1, :], stf_ref,
                                  tri_ref[0], masks_ref.at[0], False)
    ob_ref[...] = _hgrn_direction(qb_ref[...], zb_ref[...], vb_ref[...], lb_ref[1:2, :], stb_ref,
                                  tri_ref[1], masks_ref.at[1], True)


def _hgrn2(proj, lower, tri, masks):
    def spec(col0, rev):
        return pl.BlockSpec((CHUNK, LANES), lambda b, h, s: (_chunk_block(b, s, rev), col0 + h))

    out_spec = lambda rev: pl.BlockSpec((CHUNK, LANES), lambda b, h, s: (_chunk_block(b, s, rev), h))
    shape = jax.ShapeDtypeStruct((N_ALL, HG_HEADS * HEAD_DIM), F32)
    return pl.pallas_call(
        _hgrn_kernel,
        grid=(BATCH, HG_HEADS, N_STEPS),
        in_specs=[
            spec(12, False), spec(16, False), spec(24, False),
            spec(12, True), spec(20, True), spec(24, True),
            pl.BlockSpec((2, LANES), lambda b, h, s: (0, h)),
            pl.BlockSpec((3, CHUNK, CHUNK), lambda b, h, s: (0, 0, 0)),
            pl.BlockSpec((2, N_LEVELS + 1, CHUNK, CHUNK), lambda b, h, s: (0, 0, 0, 0)),
        ],
        out_specs=[out_spec(False), out_spec(True)],
        out_shape=[shape, shape],
        scratch_shapes=[pltpu.VMEM((HEAD_DIM, HEAD_DIM), F32), pltpu.VMEM((HEAD_DIM, HEAD_DIM), F32)],
        compiler_params=_params("parallel", "parallel", "arbitrary"),
        name="hgrn2",
    )(proj, proj, proj, proj, proj, proj, lower, tri, masks)


def _rope_tables():
    n_freq = HEAD_DIM // 4
    inv_freq = ROPE_BASE ** (-jnp.arange(n_freq, dtype=F32) / n_freq)
    t = jnp.arange(SEQ, dtype=jnp.int32)
    ang_r = (t // GRID_W).astype(F32)[:, None] * inv_freq
    ang_c = (t % GRID_W).astype(F32)[:, None] * inv_freq
    cos = jnp.concatenate([jnp.cos(ang_r), jnp.cos(ang_r), jnp.cos(ang_c), jnp.cos(ang_c)], axis=-1)
    sin = jnp.concatenate([-jnp.sin(ang_r), jnp.sin(ang_r), -jnp.sin(ang_c), jnp.sin(ang_c)], axis=-1)
    cos = jnp.concatenate([cos, jnp.ones((CHUNK, HEAD_DIM), F32)], axis=0)
    sin = jnp.concatenate([sin, jnp.zeros((CHUNK, HEAD_DIM), F32)], axis=0)
    return cos, sin


def _rope(x, cos, sin, lane):
    partner = jnp.where((lane & 32) == 0, pltpu.roll(x, HEAD_DIM - 32, 1), pltpu.roll(x, 32, 1))
    return x * cos + partner * sin


def _mlstm_direction(q_raw, k_raw, v, gates, gbias, cos, sin, cn_ref, m_ref, tri_ref, rev):
    L = CHUNK
    lane = lax.broadcasted_iota(jnp.int32, (L, HEAD_DIM), 1)
    qc = _rope(q_raw, cos, sin, lane)
    kc = _rope(k_raw * (HEAD_DIM ** -0.5), cos, sin, lane)

    graw = gates + gbias
    log_i = graw[0:1, :]
    xf = graw[1:2, :]
    log_f = jnp.minimum(xf, 0.0) - jnp.log(1.0 + jnp.exp(-jnp.abs(xf)))
    r8 = jnp.concatenate([log_f, log_i, jnp.zeros((6, L), F32)], axis=0)
    low, up, eye = tri_ref[0], tri_ref[1], tri_ref[2]
    b_row = _dot_sel(r8, low if rev else up)[0:1, :]
    b_col = _sel_dot_nt(up if rev else low, r8)[:, 0:1]
    i_col = _sel_dot_nt(eye, r8)[:, 1:2]
    b_end = b_row[:, 0:1] if rev else b_row[:, L - 1:L]

    ti = lax.broadcasted_iota(jnp.int32, (L, L), 0)
    si = lax.broadcasted_iota(jnp.int32, (L, L), 1)
    tri = (si >= ti) if rev else (si <= ti)
    m_prev = m_ref[0:1, 0:1]
    dmat = jnp.where(tri, b_col + (log_i - b_row), NEG_INF)
    inter = b_col + m_prev
    m_t = jnp.maximum(inter, jnp.max(dmat, axis=-1, keepdims=True))
    w_inter = jnp.exp(inter - m_t)
    p = jnp.exp(dmat - m_t) * _dot_nt(qc.astype(BF16), kc.astype(BF16))

    ones_col = jnp.where(lane == 0, 1.0, 0.0)
    v_ext = jnp.concatenate([v, ones_col], axis=-1).astype(BF16)
    cn = cn_ref[...]
    acc = _dot(p.astype(BF16), v_ext) + w_inter * _dot(qc.astype(BF16), cn.astype(BF16))
    num = acc[:, :HEAD_DIM]
    den = acc[:, HEAD_DIM:HEAD_DIM + 1]
    h = num / jnp.maximum(jnp.abs(den), jnp.exp(-m_t))

    e_row = b_end + (log_i - b_row)
    m_new = jnp.maximum(b_end + m_prev, jnp.max(e_row, axis=-1, keepdims=True))
    w_old = jnp.exp(b_end + m_prev - m_new)
    w_s = jnp.exp(b_end - b_col + i_col - m_new)
    cn_ref[...] = w_old * cn + lax.dot_general(
        (w_s * kc).astype(BF16), v_ext, (((0,), (0,)), ((), ())), preferred_element_type=F32)
    m_ref[...] = jnp.broadcast_to(m_new, m_ref.shape)
    return h


def _mlstm_kernel(qf_ref, kf_ref, vf_ref, gf_ref, cf_ref, sf_ref,
                  qb_ref, kb_ref, vb_ref, gb_ref, cb_ref, sb_ref,
                  gbias_ref, tri_ref, of_ref, ob_ref, cnf_ref, cnb_ref, mf_ref, mb_ref):
    @pl.when(pl.program_id(2) == 0)
    def _():
        cnf_ref[...] = jnp.zeros_like(cnf_ref)
        cnb_ref[...] = jnp.zeros_like(cnb_ref)
        mf_ref[...] = jnp.zeros_like(mf_ref)
        mb_ref[...] = jnp.zeros_like(mb_ref)

    of_ref[...] = _mlstm_direction(qf_ref[...], kf_ref[...], vf_ref[...], gf_ref[0, 0], gbias_ref[0, 0][:, 0:1],
                                   cf_ref[...], sf_ref[...], cnf_ref, mf_ref, tri_ref, False)
    ob_ref[...] = _mlstm_direction(qb_ref[...], kb_ref[...], vb_ref[...], gb_ref[0, 0], gbias_ref[1, 0][:, 0:1],
                                   cb_ref[...], sb_ref[...], cnb_ref, mb_ref, tri_ref, True)


def _mlstm(proj, gates_t, gbias, cos, sin, tri):
    def spec(col0, rev):
        return pl.BlockSpec((CHUNK, LANES), lambda b, h, s: (_chunk_block(b, s, rev), col0 + h))

    def gate_spec(rev):
        d = 1 if rev else 0
        return pl.BlockSpec((1, 1, 2, CHUNK), lambda b, h, s: (d, h, 0, _chunk_block(b, s, rev)))

    def rope_spec(rev):
        def idx(b, h, s):
            blk = _chunk_block(b, s, rev)
            lat = blk - b * N_LAT_CHUNKS
            return (jnp.where(s < N_CTX_CHUNKS, N_LAT_CHUNKS, lat), 0)
        return pl.BlockSpec((CHUNK, LANES), idx)

    out_spec = lambda rev: pl.BlockSpec((CHUNK, LANES), lambda b, h, s: (_chunk_block(b, s, rev), h))
    shape = jax.ShapeDtypeStruct((N_ALL, ML_HEADS * HEAD_DIM), F32)
    per_dir = lambda rev: [spec(32, rev), spec(36, rev), spec(40, rev), gate_spec(rev), rope_spec(rev), rope_spec(rev)]
    args_dir = [proj, proj, proj, gates_t, cos, sin]
    return pl.pallas_call(
        _mlstm_kernel,
        grid=(BATCH, ML_HEADS, N_STEPS),
        in_specs=per_dir(False) + per_dir(True) + [
            pl.BlockSpec((2, 1, 2, LANES), lambda b, h, s: (0, h, 0, 0)),
            pl.BlockSpec((3, CHUNK, CHUNK), lambda b, h, s: (0, 0, 0)),
        ],
        out_specs=[out_spec(False), out_spec(True)],
        out_shape=[shape, shape],
        scratch_shapes=[pltpu.VMEM((HEAD_DIM, 2 * HEAD_DIM), F32), pltpu.VMEM((HEAD_DIM, 2 * HEAD_DIM), F32),
                        pltpu.VMEM((8, LANES), F32), pltpu.VMEM((8, LANES), F32)],
        compiler_params=_params("parallel", "parallel", "arbitrary"),
        name="mlstm",
    )(*args_dir, *args_dir, gbias, tri)


def _head_rms(x, w):
    parts = []
    for hh in range(x.shape[-1] // HEAD_DIM):
        xs = x[:, hh * HEAD_DIM:(hh + 1) * HEAD_DIM]
        parts.append(xs * lax.rsqrt(jnp.mean(xs * xs, axis=-1, keepdims=True) + EPS))
    return jnp.concatenate(parts, axis=-1) * w


def _merge_kernel(na_ref, nac_ref, hgf_ref, hgb_ref, mlf_ref, mlb_ref, hgg_ref, mlo_ref, bg0_ref, bg1_ref, bg2_ref,
                  h_ref, mod_ref, hgw_ref, mlw_ref, wb_ref, wo_ref, o_ref):
    hg = _head_rms(hgf_ref[...] + hgb_ref[...], hgw_ref[...]) * _silu(hgg_ref[...])
    ml = _sigmoid(mlo_ref[...]) * _head_rms(mlf_ref[...] + mlb_ref[...], mlw_ref[...])
    is_ctx = pl.program_id(0) >= N_LAT // TM_MERGE
    na = jnp.where(is_ctx, nac_ref[...], na_ref[...])
    y = _sigmoid(bg0_ref[...]) * _dot(na.astype(BF16), wb_ref[0])
    y = y + _sigmoid(bg1_ref[...]) * _dot(hg.astype(BF16), wb_ref[1])
    y = y + _sigmoid(bg2_ref[...]) * _dot(ml.astype(BF16), wb_ref[2])
    o_ref[...] = h_ref[...] + mod_ref[0, 2:3, :] * _dot(y.astype(BF16), wo_ref[...])


def _merge(n_rows, na, na_ctx, hgf, hgb, mlf, mlb, proj, h_all, mod5, hg_w, ml_w, w_branch, w_out):
    tm = TM_MERGE
    tiles_per_batch = SEQ // tm
    n_lat_tiles = N_LAT // tm
    row = lambda w, c: pl.BlockSpec((tm, w), lambda i: (i, c))
    const = lambda shape: pl.BlockSpec(shape, lambda i: (0,) * len(shape))
    bg0 = BG_OFF // D_MODEL
    return pl.pallas_call(
        _merge_kernel,
        grid=(n_rows // tm,),
        in_specs=[
            pl.BlockSpec((tm, 512), lambda i: (jnp.minimum(i, n_lat_tiles - 1), 0)),
            pl.BlockSpec((tm, 512), lambda i: (jnp.maximum(i - n_lat_tiles, 0), 0)),
            row(512, 0), row(512, 0), row(512, 0), row(512, 0),
            row(512, 7), row(512, 11),
            row(D_MODEL, bg0), row(D_MODEL, bg0 + 1), row(D_MODEL, bg0 + 2),
            row(D_MODEL, 0),
            pl.BlockSpec((1, 6, D_MODEL), lambda i: (jnp.minimum(i // tiles_per_batch, BATCH), 0, 0)),
            const((1, 512)), const((1, 512)),
            const((3, BRANCH_WIDTH, D_MODEL)), const((D_MODEL, D_MODEL)),
        ],
        out_specs=row(D_MODEL, 0),
        out_shape=jax.ShapeDtypeStruct((n_rows, D_MODEL), F32),
        compiler_params=_params("parallel"),
        name="merge",
    )(na, na_ctx, hgf, hgb, mlf, mlb, proj, proj, proj, proj, proj, h_all, mod5, hg_w, ml_w, w_branch, w_out)


def _ffn_kernel(h_ref, nw_ref, mod_ref, wa_ref, wu_ref, wd_ref, o_ref, f_scr, acc_scr):
    j = pl.program_id(1)

    @pl.when(j == 0)
    def _():
        f = _norm_mod(h_ref[...], nw_ref[...], mod_ref[0, 3:4, :], mod_ref[0, 4:5, :])
        f_scr[...] = f.astype(BF16)
        acc_scr[...] = jnp.zeros_like(acc_scr)

    f = f_scr[...]
    g = _silu(_dot(f, wa_ref[...])) * _dot(f, wu_ref[...])
    acc_scr[...] += _dot(g.astype(BF16), wd_ref[...])

    @pl.when(j == pl.num_programs(1) - 1)
    def _():
        o_ref[...] = h_ref[...] + mod_ref[0, 5:6, :] * acc_scr[...]


def _ffn(h_all, nw, mod5, w_up, w_down):
    n_rows = h_all.shape[0]
    nj = FFN_DIM // TH_FFN
    tiles_per_batch = SEQ // TM_FFN
    return pl.pallas_call(
        _ffn_kernel,
        grid=(n_rows // TM_FFN, nj),
        in_specs=[
            pl.BlockSpec((TM_FFN, D_MODEL), lambda i, j: (i, 0)),
            pl.BlockSpec((1, D_MODEL), lambda i, j: (0, 0)),
            pl.BlockSpec((1, 6, D_MODEL), lambda i, j: (i // tiles_per_batch, 0, 0)),
            pl.BlockSpec((D_MODEL, TH_FFN), lambda i, j: (0, j)),
            pl.BlockSpec((D_MODEL, TH_FFN), lambda i, j: (0, nj + j)),
            pl.BlockSpec((TH_FFN, D_MODEL), lambda i, j: (j, 0)),
        ],
        out_specs=pl.BlockSpec((TM_FFN, D_MODEL), lambda i, j: (i, 0)),
        out_shape=jax.ShapeDtypeStruct((n_rows, D_MODEL), F32),
        scratch_shapes=[pltpu.VMEM((TM_FFN, D_MODEL), BF16), pltpu.VMEM((TM_FFN, D_MODEL), F32)],
        compiler_params=_params("parallel", "arbitrary"),
        name="ffn",
    )(h_all, nw, mod5, w_up, w_up, w_down)


def _router_kernel(h_ref, nw_ref, mod_ref, wr_ref, f_ref, r_ref):
    f = _norm_mod(h_ref[...], nw_ref[...], mod_ref[0, 3:4, :], mod_ref[0, 4:5, :])
    f_ref[...] = f
    logits = jnp.dot(f, wr_ref[...], preferred_element_type=F32, precision=lax.Precision.HIGHEST)
    lane = lax.broadcasted_iota(jnp.int32, logits.shape, 1)
    logits = jnp.where(lane < N_EXPERTS, logits, -jnp.inf)
    m1 = jnp.max(logits, axis=-1, keepdims=True)
    i1 = jnp.min(jnp.where(logits == m1, lane, LANES), axis=-1, keepdims=True)
    rest = jnp.where(lane == i1, -jnp.inf, logits)
    m2 = jnp.max(rest, axis=-1, keepdims=True)
    i2 = jnp.min(jnp.where(rest == m2, lane, LANES), axis=-1, keepdims=True)
    e2 = jnp.exp(m2 - m1)
    w1 = 1.0 / (1.0 + e2)
    w2 = e2 / (1.0 + e2)
    r_ref[...] = jnp.where(lane == 0, i1.astype(F32),
                           jnp.where(lane == 1, i2.astype(F32),
                                     jnp.where(lane == 2, w1, jnp.where(lane == 3, w2, 0.0))))


def _router(h_lat, nw, mod5, w_router_pad):
    tm = 512
    tiles_per_batch = SEQ // tm
    return pl.pallas_call(
        _router_kernel,
        grid=(N_LAT // tm,),
        in_specs=[
            pl.BlockSpec((tm, D_MODEL), lambda i: (i, 0)),
            pl.BlockSpec((1, D_MODEL), lambda i: (0, 0)),
            pl.BlockSpec((1, 6, D_MODEL), lambda i: (i // tiles_per_batch, 0, 0)),
            pl.BlockSpec((D_MODEL, LANES), lambda i: (0, 0)),
        ],
        out_specs=[pl.BlockSpec((tm, D_MODEL), lambda i: (i, 0)), pl.BlockSpec((tm, LANES), lambda i: (i, 0))],
        out_shape=[jax.ShapeDtypeStruct((N_LAT, D_MODEL), F32), jax.ShapeDtypeStruct((N_LAT, LANES), F32)],
        compiler_params=_params("parallel"),
        name="router",
    )(h_lat, nw, mod5, w_router_pad)


def _moe_kernel(be_ref, nused_ref, code_ref, f_hbm, wa_ref, wu_ref, wd_ref, y_hbm,
                xbuf, x16, acc, sem_in, sem_out):
    i = pl.program_id(0)
    j = pl.program_id(1)
    active = i < nused_ref[0]
    base = i * TM_MOE

    def row_copy_in(r):
        tok = jnp.maximum(code_ref[base + r], 0) >> 1
        return pltpu.make_async_copy(f_hbm.at[pl.ds(tok, 1)], xbuf.at[pl.ds(r, 1)], sem_in)

    def row_copy_out(r):
        dst = jnp.maximum(code_ref[base + r], 0)
        return pltpu.make_async_copy(acc.at[pl.ds(r, 1)], y_hbm.at[pl.ds(dst, 1)], sem_out)

    @pl.when(jnp.logical_and(active, j == 0))
    def _():
        def start(r, c):
            row_copy_in(r).start()
            return c
        lax.fori_loop(0, TM_MOE, start, 0)

        def wait(r, c):
            row_copy_in(r).wait()
            return c
        lax.fori_loop(0, TM_MOE, wait, 0)
        x16[...] = xbuf[...].astype(BF16)
        acc[...] = jnp.zeros_like(acc)

    @pl.when(active)
    def _():
        x = x16[...]
        g = _silu(_dot(x, wa_ref[0])) * _dot(x, wu_ref[0])
        acc[...] += _dot(g.astype(BF16), wd_ref[0])

    @pl.when(jnp.logical_and(active, j == pl.num_programs(1) - 1))
    def _():
        def start(r, c):
            @pl.when(code_ref[base + r] >= 0)
            def _():
                row_copy_out(r).start()
            return c
        lax.fori_loop(0, TM_MOE, start, 0)

        def wait(r, c):
            @pl.when(code_ref[base + r] >= 0)
            def _():
                row_copy_out(r).wait()
            return c
        lax.fori_loop(0, TM_MOE, wait, 0)


def _moe_experts(block_e, n_used, codes, f_lat, w_up, w_down):
    nj = EXPERT_DIM // TH_MOE

    def jj(i, j, nu):
        return jnp.where(i < nu[0], j, nj - 1)

    grid_spec = pltpu.PrefetchScalarGridSpec(
        num_scalar_prefetch=3,
        grid=(N_MOE_BLOCKS, nj),
        in_specs=[
            pl.BlockSpec(memory_space=pl.ANY),
            pl.BlockSpec((1, D_MODEL, TH_MOE), lambda i, j, be, nu, cd: (be[i], 0, jj(i, j, nu))),
            pl.BlockSpec((1, D_MODEL, TH_MOE), lambda i, j, be, nu, cd: (be[i], 0, nj + jj(i, j, nu))),
            pl.BlockSpec((1, TH_MOE, D_MODEL), lambda i, j, be, nu, cd: (be[i], jj(i, j, nu), 0)),
        ],
        out_specs=pl.BlockSpec(memory_space=pl.ANY),
        scratch_shapes=[
            pltpu.VMEM((TM_MOE, D_MODEL), F32), pltpu.VMEM((TM_MOE, D_MODEL), BF16),
            pltpu.VMEM((TM_MOE, D_MODEL), F32),
            pltpu.SemaphoreType.DMA(()), pltpu.SemaphoreType.DMA(()),
        ],
    )
    return pl.pallas_call(
        _moe_kernel,
        grid_spec=grid_spec,
        out_shape=jax.ShapeDtypeStruct((2 * N_LAT, D_MODEL), F32),
        compiler_params=_params("arbitrary", "arbitrary"),
        name="moe_experts",
    )(block_e, n_used, codes, f_lat, w_up, w_up, w_down)


def _combine_kernel(h_ref, y_ref, r_ref, mod_ref, fw_ref, o_ref):
    r = r_ref[...]
    y = r[:, 2:3] * y_ref[:, :D_MODEL] + r[:, 3:4] * y_ref[:, D_MODEL:]
    h = h_ref[...] + mod_ref[0, 5:6, :] * y
    o_ref[...] = h * lax.rsqrt(jnp.mean(h * h, axis=-1, keepdims=True) + EPS) * fw_ref[...]


def _combine_final(h_lat, y_pairs, route, mod5, final_w):
    tm = 512
    tiles_per_batch = SEQ // tm
    return pl.pallas_call(
        _combine_kernel,
        grid=(N_LAT // tm,),
        in_specs=[
            pl.BlockSpec((tm, D_MODEL), lambda i: (i, 0)),
            pl.BlockSpec((tm, 2 * D_MODEL), lambda i: (i, 0)),
            pl.BlockSpec((tm, LANES), lambda i: (i, 0)),
            pl.BlockSpec((1, 6, D_MODEL), lambda i: (i // tiles_per_batch, 0, 0)),
            pl.BlockSpec((1, D_MODEL), lambda i: (0, 0)),
        ],
        out_specs=pl.BlockSpec((tm, D_MODEL), lambda i: (i, 0)),
        out_shape=jax.ShapeDtypeStruct((N_LAT, D_MODEL), F32),
        compiler_params=_params("parallel"),
        name="combine_final",
    )(h_lat, y_pairs, route, mod5, final_w)


def _moe_plan(route):
    e12 = route[:, 0:2].astype(jnp.int32)
    onehot = (e12[:, :, None] == jnp.arange(N_EXPERTS, dtype=jnp.int32)).astype(jnp.int32).sum(axis=1)
    before = jnp.cumsum(onehot, axis=0) - onehot
    counts = jnp.sum(onehot, axis=0)
    nblk = (counts + TM_MOE - 1) // TM_MOE
    blk_end = jnp.cumsum(nblk)
    slot0 = (blk_end - nblk) * TM_MOE
    rank = jnp.take_along_axis(before, e12, axis=1)
    dest = slot0[e12] + rank
    codes = jnp.full((N_SLOTS,), -1, jnp.int32).at[dest.reshape(-1)].set(jnp.arange(2 * N_LAT, dtype=jnp.int32))
    n_used = blk_end[-1]
    blocks = jnp.minimum(jnp.arange(N_MOE_BLOCKS, dtype=jnp.int32), n_used - 1)
    block_e = jnp.minimum(jnp.sum((blocks[:, None] >= blk_end[None, :]).astype(jnp.int32), axis=1), N_EXPERTS - 1)
    return block_e, n_used.reshape(1).astype(jnp.int32), codes


def kernel(x, c, ctx, c_ctx, mod_w, mod_b, norm1_w, w_in, na_rpb, hg_lb, hg_norm_w, ml_gate_b, ml_norm_w,
           w_branch, w_out, norm2_w, ffn_w_up, ffn_w_down, moe_router, moe_w_up, moe_w_down, final_norm_w):
    h_all = jnp.concatenate([x.reshape(N_LAT, D_MODEL), ctx.reshape(N_CTX, D_MODEL)], axis=0)
    c8 = jnp.concatenate([c, c_ctx[None, :], jnp.zeros((3, D_MODEL), F32)], axis=0)
    mods = _modulation(c8, mod_w, mod_b).reshape(DEPTH, 8, 6, D_MODEL)

    lb_p = jax.nn.softmax(hg_lb.astype(F32), axis=0)
    hg_lower = jnp.cumsum(lb_p, axis=0) - lb_p[0]
    tri = _tri_consts()
    masks = _level_masks()
    cos, sin = _rope_tables()

    out = None
    for layer in range(DEPTH):
        last = layer == DEPTH - 1
        mod5 = mods[layer, :5]
        wl = w_in[layer]
        w_perm = jnp.concatenate(
            [wl[:, :MAIN_W], wl[:, MAIN_W + ML_GATE_COLS:], wl[:, MAIN_W:MAIN_W + ML_GATE_COLS],
             jnp.zeros((D_MODEL, 512 - ML_GATE_COLS), F32)], axis=1).astype(BF16)
        proj = _inproj(h_all, norm1_w[layer][None, :], mod5, w_perm)

        na = _na_latent(proj, _na_bias_tables(na_rpb[layer]))
        na_ctx = na if last else _ctx_attention(proj)
        hgf, hgb = _hgrn2(proj, hg_lower[layer], tri, masks)
        gates_t = proj[:, GATE_OFF:GATE_OFF + ML_GATE_COLS].T.reshape(2, 2, ML_HEADS, N_ALL).transpose(0, 2, 1, 3)
        gbias = jnp.broadcast_to(ml_gate_b[layer].transpose(0, 2, 1)[..., None], (2, ML_HEADS, 2, LANES))
        mlf, mlb = _mlstm(proj, gates_t, gbias, cos, sin, tri)

        n_rows = N_LAT if last else N_ALL
        h_all = _merge(n_rows, na, na_ctx, hgf, hgb, mlf, mlb, proj, h_all, mod5,
                       hg_norm_w[layer][None, :], ml_norm_w[layer][None, :],
                       w_branch[layer].astype(BF16), w_out[layer].astype(BF16))
        i = layer // 2
        if layer % 2 == 0:
            h_all = _ffn(h_all, norm2_w[layer][None, :], mod5, ffn_w_up[i].astype(BF16), ffn_w_down[i].astype(BF16))
            if last:
                raise NotImplementedError("final norm after a dense last layer")
        else:
            if not last:
                raise NotImplementedError("MoE on the context stream")
            w_router_pad = jnp.pad(moe_router[i], ((0, 0), (0, LANES - N_EXPERTS)))
            f_lat, route = _router(h_all, norm2_w[layer][None, :], mod5, w_router_pad)
            block_e, n_used, codes = _moe_plan(route)
            y = _moe_experts(block_e, n_used, codes, f_lat, moe_w_up[i].astype(BF16), moe_w_down[i].astype(BF16))
            out = _combine_final(h_all, y.reshape(N_LAT, 2 * D_MODEL), route, mod5, final_norm_w[None, :])
    return out.reshape(BATCH, SEQ, D_MODEL)
```

```python
import functools

import numpy as np
import jax
import jax.numpy as jnp
from jax import lax
from jax.experimental import pallas as pl
from jax.experimental.pallas import tpu as pltpu

F32 = jnp.float32
BF16 = jnp.bfloat16

D_MODEL = 1024
BATCH = 4
SEQ = 4096
DEPTH = 2
GRID_W = 64
GRID_H = SEQ // GRID_W
CTX_LEN = 256
EPS = 1e-6
NEG_INF = -1e30
F_FLOOR = 1e-30
NA_HEADS = 8
NA_HEAD_DIM = 64
NA_WIN_ROWS = 8
NA_WIN_COLS = 16
HG_HEADS = 4
ML_HEADS = 4
HEAD_DIM = 128
ML_GATE_COLS = 16
ROPE_BASE = 10000.0
BRANCH_WIDTH = 512
FFN_DIM = 2816
N_EXPERTS = 8
EXPERT_DIM = 3584

N_LAT = BATCH * SEQ
N_CTX = BATCH * CTX_LEN
N_ALL = N_LAT + N_CTX

LANES = 128
VMEM_LIMIT = 56 * 1024 * 1024

MAIN_W = 12 * 512
BG_OFF = MAIN_W
PROJ_W = MAIN_W + 3 * D_MODEL

TM_PROJ = 1024
TN_PROJ = PROJ_W // 4
TM_MERGE = 256
TM_FFN = 512
TH_FFN = FFN_DIM // 2
NA_QROWS = 8
NA_KROWS = NA_QROWS + NA_WIN_ROWS
CHUNK = 256
TM_MOE = 512
TH_MOE = EXPERT_DIM // 2
N_MOE_BLOCKS = -(-(2 * N_LAT + N_EXPERTS * (TM_MOE - 1)) // TM_MOE)
N_SLOTS = N_MOE_BLOCKS * TM_MOE


def _params(*sem):
    return pltpu.CompilerParams(dimension_semantics=sem, vmem_limit_bytes=VMEM_LIMIT)


def _sigmoid(x):
    return 1.0 / (1.0 + jnp.exp(-x))


def _silu(x):
    return x * _sigmoid(x)


def _dot(a, b):
    return jnp.dot(a, b, preferred_element_type=F32)


def _dot_nt(a, b):
    return lax.dot_general(a, b, (((1,), (1,)), ((), ())), preferred_element_type=F32)


def _split3(x):
    hi = x.astype(BF16)
    r = x - hi.astype(F32)
    mid = r.astype(BF16)
    lo = (r - mid.astype(F32)).astype(BF16)
    return hi, mid, lo


def _sel_dot(sel, x):
    hi, mid, lo = _split3(x)
    return _dot(sel, lo) + _dot(sel, mid) + _dot(sel, hi)


def _sel_dot_nt(sel, x):
    hi, mid, lo = _split3(x)
    return _dot_nt(sel, lo) + _dot_nt(sel, mid) + _dot_nt(sel, hi)


def _dot_sel(x, sel):
    hi, mid, lo = _split3(x)
    return _dot(lo, sel) + _dot(mid, sel) + _dot(hi, sel)


def _norm_mod(x, nw, shift, scale):
    y = x * lax.rsqrt(jnp.mean(x * x, axis=-1, keepdims=True) + EPS) * nw
    return y * (1.0 + scale) + shift


def _mod_kernel(c_ref, w_ref, b_ref, o_ref):
    s = _silu(c_ref[...])
    o_ref[0] = jnp.dot(s, w_ref[0], preferred_element_type=F32, precision=lax.Precision.HIGHEST) + b_ref[0]


def _modulation(c8, mod_w, mod_b):
    tn = 1536
    return pl.pallas_call(
        _mod_kernel,
        grid=(DEPTH, 6 * D_MODEL // tn),
        in_specs=[
            pl.BlockSpec((8, D_MODEL), lambda l, j: (0, 0)),
            pl.BlockSpec((1, D_MODEL, tn), lambda l, j: (l, 0, j)),
            pl.BlockSpec((1, 1, tn), lambda l, j: (l, 0, j)),
        ],
        out_specs=pl.BlockSpec((1, 8, tn), lambda l, j: (l, 0, j)),
        out_shape=jax.ShapeDtypeStruct((DEPTH, 8, 6 * D_MODEL), F32),
        compiler_params=_params("parallel", "parallel"),
        name="modulation",
    )(c8, mod_w, mod_b.reshape(DEPTH, 1, 6 * D_MODEL))


def _inproj_kernel(h_ref, nw_ref, mod_ref, w_ref, wg_ref, o_ref, g_ref, a_scr):
    @pl.when(pl.program_id(1) == 0)
    def _():
        a = _norm_mod(h_ref[...], nw_ref[...], mod_ref[0, 0:1, :], mod_ref[0, 1:2, :])
        a_scr[...] = a.astype(BF16)
        g_ref[...] = _dot(a_scr[...], wg_ref[...])

    o_ref[...] = _dot(a_scr[...], w_ref[...])


def _inproj(h_all, nw, mod5, w_main, w_gates):
    tiles_per_batch = SEQ // TM_PROJ
    return pl.pallas_call(
        _inproj_kernel,
        grid=(N_ALL // TM_PROJ, PROJ_W // TN_PROJ),
        in_specs=[
            pl.BlockSpec((TM_PROJ, D_MODEL), lambda i, j: (i, 0)),
            pl.BlockSpec((1, D_MODEL), lambda i, j: (0, 0)),
            pl.BlockSpec((1, 6, D_MODEL), lambda i, j: (i // tiles_per_batch, 0, 0)),
            pl.BlockSpec((D_MODEL, TN_PROJ), lambda i, j: (0, j)),
            pl.BlockSpec((D_MODEL, LANES), lambda i, j: (0, 0)),
        ],
        out_specs=[pl.BlockSpec((TM_PROJ, TN_PROJ), lambda i, j: (i, j)),
                   pl.BlockSpec((TM_PROJ, LANES), lambda i, j: (i, 0))],
        out_shape=[jax.ShapeDtypeStruct((N_ALL, PROJ_W), F32), jax.ShapeDtypeStruct((N_ALL, LANES), F32)],
        scratch_shapes=[pltpu.VMEM((TM_PROJ, D_MODEL), BF16)],
        compiler_params=_params("parallel", "arbitrary"),
        name="inproj",
    )(h_all, nw, mod5, w_main, w_gates)


N_DR = 2 * NA_WIN_ROWS - 1


def _na_bias_tables(rpb):
    qc = np.arange(GRID_W)[:, None]
    kc = np.arange(GRID_W)[None, :]
    dc = np.clip(kc - qc + NA_WIN_COLS - 1, 0, 2 * NA_WIN_COLS - 2)
    ws = np.clip(qc - NA_WIN_COLS // 2, 0, GRID_W - NA_WIN_COLS)
    col_ok = (kc >= ws) & (kc < ws + NA_WIN_COLS)
    onehot = ((dc[None] == np.arange(2 * NA_WIN_COLS - 1)[:, None, None]) & col_ok[None]).astype(np.float32)
    t = jnp.einsum('hrd,dqk->hrqk', rpb.astype(F32), jnp.asarray(onehot), precision=lax.Precision.HIGHEST)
    t = t + jnp.asarray(np.where(col_ok, 0.0, NEG_INF).astype(np.float32))
    tp = jnp.pad(t, ((0, 0), (1, 2), (0, 0), (0, 0)))
    return jnp.concatenate([tp[:, :N_DR + 2], tp[:, 1:]], axis=-1)


def _na_kernel(q_ref, k_ref, v_ref, kc_ref, vc_ref, tab_ref, o_ref):
    blk = pl.program_id(2)
    q0 = blk * NA_QROWS
    k0 = jnp.clip(q0 - NA_WIN_ROWS // 2, 0, GRID_H - NA_KROWS)
    start = pl.multiple_of(k0 * GRID_W, GRID_W)
    nk = NA_KROWS * GRID_W
    lane = lax.broadcasted_iota(jnp.int32, (GRID_W, LANES), 1)

    tab_idx, penalty = [], []
    for qr in range(NA_QROWS):
        r = q0 + qr
        r0 = jnp.clip(r - NA_WIN_ROWS // 2, 0, GRID_H - NA_WIN_ROWS)
        idx_row, pen_row = [], []
        for j in range(NA_KROWS // 2):
            kra = k0 + 2 * j
            pa = jnp.where(jnp.logical_and(kra >= r0, kra < r0 + NA_WIN_ROWS), 0.0, NEG_INF)
            pb = jnp.where(jnp.logical_and(kra + 1 >= r0, kra + 1 < r0 + NA_WIN_ROWS), 0.0, NEG_INF)
            idx_row.append(jnp.clip(kra - r + NA_WIN_ROWS, 0, N_DR + 1))
            pen_row.append(jnp.where(lane < GRID_W, pa, pb))
        tab_idx.append(idx_row)
        penalty.append(pen_row)

    outs = []
    for hh in range(2):
        sl = slice(hh * NA_HEAD_DIM, (hh + 1) * NA_HEAD_DIM)
        q = (q_ref[:, sl] * (NA_HEAD_DIM ** -0.5)).astype(BF16)
        k = k_ref[pl.ds(start, nk), sl].astype(BF16)
        v = v_ref[pl.ds(start, nk), sl].astype(BF16)
        kc = kc_ref[:, sl].astype(BF16)
        vc = vc_ref[:, sl].astype(BF16)
        s_raw = _dot_nt(q, k)
        rows = []
        for qr in range(NA_QROWS):
            cols = []
            for j in range(NA_KROWS // 2):
                s_blk = s_raw[qr * GRID_W:(qr + 1) * GRID_W, j * LANES:(j + 1) * LANES]
                cols.append(s_blk + (tab_ref[hh, tab_idx[qr][j]] + penalty[qr][j]))
            rows.append(jnp.concatenate(cols, axis=1))
        s_loc = jnp.concatenate(rows, axis=0)
        s_ctx = _dot_nt(q, kc)
        m = jnp.maximum(jnp.max(s_loc, axis=-1, keepdims=True), jnp.max(s_ctx, axis=-1, keepdims=True))
        p_loc = jnp.exp(s_loc - m)
        p_ctx = jnp.exp(s_ctx - m)
        den = jnp.sum(p_loc, axis=-1, keepdims=True) + jnp.sum(p_ctx, axis=-1, keepdims=True)
        o = _dot(p_loc.astype(BF16), v) + _dot(p_ctx.astype(BF16), vc)
        outs.append(o / den)
    o_ref[...] = jnp.concatenate(outs, axis=-1)


def _na_latent(proj, bias):
    nq = NA_QROWS * GRID_W
    nblk = GRID_H // NA_QROWS
    ctx_blk0 = N_LAT // CTX_LEN
    return pl.pallas_call(
        _na_kernel,
        grid=(BATCH, NA_HEADS // 2, nblk),
        in_specs=[
            pl.BlockSpec((nq, LANES), lambda b, hp, blk: (b * nblk + blk, hp)),
            pl.BlockSpec((SEQ, LANES), lambda b, hp, blk: (b, 4 + hp)),
            pl.BlockSpec((SEQ, LANES), lambda b, hp, blk: (b, 8 + hp)),
            pl.BlockSpec((CTX_LEN, LANES), lambda b, hp, blk: (ctx_blk0 + b, 4 + hp)),
            pl.BlockSpec((CTX_LEN, LANES), lambda b, hp, blk: (ctx_blk0 + b, 8 + hp)),
            pl.BlockSpec((2, N_DR + 2, GRID_W, LANES), lambda b, hp, blk: (hp, 0, 0, 0)),
        ],
        out_specs=pl.BlockSpec((nq, LANES), lambda b, hp, blk: (b * nblk + blk, hp)),
        out_shape=jax.ShapeDtypeStruct((N_LAT, NA_HEADS * NA_HEAD_DIM), F32),
        compiler_params=_params("parallel", "parallel", "arbitrary"),
        name="na_latent",
    )(proj, proj, proj, proj, proj, bias)


def _ctx_attn_kernel(q_ref, k_ref, v_ref, o_ref):
    outs = []
    for hh in range(2):
        sl = slice(hh * NA_HEAD_DIM, (hh + 1) * NA_HEAD_DIM)
        q = (q_ref[:, sl] * (NA_HEAD_DIM ** -0.5)).astype(BF16)
        s = _dot_nt(q, k_ref[:, sl].astype(BF16))
        p = jnp.exp(s - jnp.max(s, axis=-1, keepdims=True))
        o = _dot(p.astype(BF16), v_ref[:, sl].astype(BF16))
        outs.append(o / jnp.sum(p, axis=-1, keepdims=True))
    o_ref[...] = jnp.concatenate(outs, axis=-1)


def _ctx_attention(proj):
    ctx_blk0 = N_LAT // CTX_LEN
    return pl.pallas_call(
        _ctx_attn_kernel,
        grid=(BATCH, NA_HEADS // 2),
        in_specs=[
            pl.BlockSpec((CTX_LEN, LANES), lambda b, hp: (ctx_blk0 + b, hp)),
            pl.BlockSpec((CTX_LEN, LANES), lambda b, hp: (ctx_blk0 + b, 4 + hp)),
            pl.BlockSpec((CTX_LEN, LANES), lambda b, hp: (ctx_blk0 + b, 8 + hp)),
        ],
        out_specs=pl.BlockSpec((CTX_LEN, LANES), lambda b, hp: (b, hp)),
        out_shape=jax.ShapeDtypeStruct((N_CTX, NA_HEADS * NA_HEAD_DIM), F32),
        compiler_params=_params("parallel", "parallel"),
        name="ctx_attention",
    )(proj, proj, proj)


N_CTX_CHUNKS = CTX_LEN // CHUNK
N_LAT_CHUNKS = SEQ // CHUNK
N_STEPS = N_CTX_CHUNKS + N_LAT_CHUNKS
N_LEVELS = CHUNK.bit_length() - 1


def _chunk_block(b, s, rev):
    c_ctx = (N_CTX_CHUNKS - 1 - s) if rev else s
    c_lat = (N_LAT_CHUNKS - 1 - (s - N_CTX_CHUNKS)) if rev else (s - N_CTX_CHUNKS)
    ctx_blk = N_LAT // CHUNK + b * N_CTX_CHUNKS + c_ctx
    lat_blk = b * N_LAT_CHUNKS + c_lat
    return jnp.where(s < N_CTX_CHUNKS, ctx_blk, lat_blk)


def _tri_consts():
    i = np.arange(CHUNK)
    low = (i[None, :] <= i[:, None]).astype(np.float32)
    eye = np.eye(CHUNK, dtype=np.float32)
    return jnp.asarray(np.stack([low, low.T, eye]), BF16)


def _level_masks():
    t = np.arange(CHUNK)[:, None]
    s = np.arange(CHUNK)[None, :]
    out = np.zeros((2, N_LEVELS + 1, CHUNK, CHUNK), np.float32)
    for l in range(N_LEVELS):
        pair = ((t ^ s) >> l) == 1
        out[0, l] = pair & (t > s)
        out[1, l] = pair & (t < s)
    out[:, N_LEVELS] = (t == s)
    return jnp.asarray(out)


def _hgrn_direction(q_raw, z, v, lb, st_ref, tri, masks_ref, rev):
    L = CHUNK
    e = jnp.exp(-jnp.abs(z))
    r = 1.0 / (1.0 + e)
    pos = z >= 0
    sig = jnp.where(pos, r, e * r)
    nsig = jnp.where(pos, e * r, r)
    k = (1.0 - lb) * nsig
    logf = jnp.log(jnp.maximum(lb + (1.0 - lb) * sig, F_FLOOR))
    q = _silu(q_raw)
    g = _sel_dot(tri, logf)
    row = lax.broadcasted_iota(jnp.int32, (L, HEAD_DIM), 0)

    st = st_ref[...]
    o = _dot_nt((q * jnp.exp(g)).astype(BF16), st.astype(BF16))

    a = jnp.zeros((L, L), F32)
    bnd = g
    for l in range(N_LEVELS):
        blk = 1 << l
        q_side = ((row & blk) == 0) if rev else ((row & blk) != 0)
        prev_end = pltpu.roll(bnd, (L - blk) if rev else blk, 0)
        dq = jnp.where(q_side, g - prev_end, 0.0)
        dk = jnp.where(q_side, 0.0, bnd - g)
        qb = (q * jnp.exp(dq)).astype(BF16)
        kb = (k * jnp.exp(dk)).astype(BF16)
        a = a + masks_ref[l] * _dot_nt(qb, kb)
        nxt = pltpu.roll(bnd, blk if rev else (L - blk), 0)
        bnd = jnp.where(q_side, bnd, nxt)
    a = a + masks_ref[N_LEVELS] * _dot_nt(q.astype(BF16), k.astype(BF16))
    o = o + _dot(a.astype(BF16), v.astype(BF16))

    kd = k * jnp.exp(bnd - g)
    st_ref[...] = jnp.exp(bnd[0:1, :]) * st + lax.dot_general(
        v.astype(BF16), kd.astype(BF16), (((0,), (0,)), ((), ())), preferred_element_type=F32)
    return o


def _hgrn_kernel(qf_ref, zf_ref, vf_ref, qb_ref, zb_ref, vb_ref, lb_ref, tri_ref, masks_ref,
                 of_ref, ob_ref, stf_ref, stb_ref):
    @pl.when(pl.program_id(2) == 0)
    def _():
        stf_ref[...] = jnp.zeros_like(stf_ref)
        stb_ref[...] = jnp.zeros_like(stb_ref)

    of_ref[...] = _hgrn_direction(qf_ref[...], zf_ref[...], vf_ref[...], lb_ref[0:1, :], stf_ref,
                                  tri_ref[0], masks_ref.at[0], False)
    ob_ref[...] = _hgrn_direction(qb_ref[...], zb_ref[...], vb_ref[...], lb_ref[1:2, :], stb_ref,
                                  tri_ref[1], masks_ref.at[1], True)


def _hgrn2(proj, lower, tri, masks):
    def spec(col0, rev):
        return pl.BlockSpec((CHUNK, LANES), lambda b, h, s: (_chunk_block(b, s, rev), col0 + h))

    out_spec = lambda rev: pl.BlockSpec((CHUNK, LANES), lambda b, h, s: (_chunk_block(b, s, rev), h))
    shape = jax.ShapeDtypeStruct((N_ALL, HG_HEADS * HEAD_DIM), F32)
    return pl.pallas_call(
        _hgrn_kernel,
        grid=(BATCH, HG_HEADS, N_STEPS),
        in_specs=[
            spec(12, False), spec(16, False), spec(24, False),
            spec(12, True), spec(20, True), spec(24, True),
            pl.BlockSpec((2, LANES), lambda b, h, s: (0, h)),
            pl.BlockSpec((3, CHUNK, CHUNK), lambda b, h, s: (0, 0, 0)),
            pl.BlockSpec((2, N_LEVELS + 1, CHUNK, CHUNK), lambda b, h, s: (0, 0, 0, 0)),
        ],
        out_specs=[out_spec(False), out_spec(True)],
        out_shape=[shape, shape],
        scratch_shapes=[pltpu.VMEM((HEAD_DIM, HEAD_DIM), F32), pltpu.VMEM((HEAD_DIM, HEAD_DIM), F32)],
        compiler_params=_params("parallel", "parallel", "arbitrary"),
        name="hgrn2",
    )(proj, proj, proj, proj, proj, proj, lower, tri, masks)


def _rope_tables():
    n_freq = HEAD_DIM // 4
    inv_freq = ROPE_BASE ** (-jnp.arange(n_freq, dtype=F32) / n_freq)
    t = jnp.arange(SEQ, dtype=jnp.int32)
    ang_r = (t // GRID_W).astype(F32)[:, None] * inv_freq
    ang_c = (t % GRID_W).astype(F32)[:, None] * inv_freq
    cos = jnp.concatenate([jnp.cos(ang_r), jnp.cos(ang_r), jnp.cos(ang_c), jnp.cos(ang_c)], axis=-1)
    sin = jnp.concatenate([-jnp.sin(ang_r), jnp.sin(ang_r), -jnp.sin(ang_c), jnp.sin(ang_c)], axis=-1)
    cos = jnp.concatenate([cos, jnp.ones((CHUNK, HEAD_DIM), F32)], axis=0)
    sin = jnp.concatenate([sin, jnp.zeros((CHUNK, HEAD_DIM), F32)], axis=0)
    return cos, sin


def _rope(x, cos, sin, lane):
    partner = jnp.where((lane & 32) == 0, pltpu.roll(x, HEAD_DIM - 32, 1), pltpu.roll(x, 32, 1))
    return x * cos + partner * sin


def _mlstm_direction(q_raw, k_raw, v, gates, gbias, cos, sin, cn_ref, m_ref, tri_ref, rev):
    L = CHUNK
    lane = lax.broadcasted_iota(jnp.int32, (L, HEAD_DIM), 1)
    qc = _rope(q_raw, cos, sin, lane)
    kc = _rope(k_raw * (HEAD_DIM ** -0.5), cos, sin, lane)

    graw = gates + gbias
    log_i = graw[0:1, :]
    xf = graw[1:2, :]
    log_f = jnp.minimum(xf, 0.0) - jnp.log(1.0 + jnp.exp(-jnp.abs(xf)))
    r8 = jnp.concatenate([log_f, log_i, jnp.zeros((6, L), F32)], axis=0)
    low, up, eye = tri_ref[0], tri_ref[1], tri_ref[2]
    b_row = _dot_sel(r8, low if rev else up)[0:1, :]
    b_col = _sel_dot_nt(up if rev else low, r8)[:, 0:1]
    i_col = _sel_dot_nt(eye, r8)[:, 1:2]
    b_end = b_row[:, 0:1] if rev else b_row[:, L - 1:L]

    ti = lax.broadcasted_iota(jnp.int32, (L, L), 0)
    si = lax.broadcasted_iota(jnp.int32, (L, L), 1)
    tri = (si >= ti) if rev else (si <= ti)
    m_prev = m_ref[0:1, 0:1]
    dmat = jnp.where(tri, b_col + (log_i - b_row), NEG_INF)
    inter = b_col + m_prev
    m_t = jnp.maximum(inter, jnp.max(dmat, axis=-1, keepdims=True))
    w_inter = jnp.exp(inter - m_t)
    p = jnp.exp(dmat - m_t) * _dot_nt(qc.astype(BF16), kc.astype(BF16))

    ones_col = jnp.where(lane == 0, 1.0, 0.0)
    v_ext = jnp.concatenate([v, ones_col], axis=-1).astype(BF16)
    cn = cn_ref[...]
    acc = _dot(p.astype(BF16), v_ext) + w_inter * _dot(qc.astype(BF16), cn.astype(BF16))
    num = acc[:, :HEAD_DIM]
    den = acc[:, HEAD_DIM:HEAD_DIM + 1]
    h = num / jnp.maximum(jnp.abs(den), jnp.exp(-m_t))

    e_row = b_end + (log_i - b_row)
    m_new = jnp.maximum(b_end + m_prev, jnp.max(e_row, axis=-1, keepdims=True))
    w_old = jnp.exp(b_end + m_prev - m_new)
    w_s = jnp.exp(b_end - b_col + i_col - m_new)
    cn_ref[...] = w_old * cn + lax.dot_general(
        (w_s * kc).astype(BF16), v_ext, (((0,), (0,)), ((), ())), preferred_element_type=F32)
    m_ref[...] = jnp.broadcast_to(m_new, m_ref.shape)
    return h


def _mlstm_kernel(qf_ref, kf_ref, vf_ref, gf_ref, cf_ref, sf_ref,
                  qb_ref, kb_ref, vb_ref, gb_ref, cb_ref, sb_ref,
                  gbias_ref, tri_ref, of_ref, ob_ref, cnf_ref, cnb_ref, mf_ref, mb_ref):
    @pl.when(pl.program_id(2) == 0)
    def _():
        cnf_ref[...] = jnp.zeros_like(cnf_ref)
        cnb_ref[...] = jnp.zeros_like(cnb_ref)
        mf_ref[...] = jnp.zeros_like(mf_ref)
        mb_ref[...] = jnp.zeros_like(mb_ref)

    of_ref[...] = _mlstm_direction(qf_ref[...], kf_ref[...], vf_ref[...], gf_ref[0, 0], gbias_ref[0, 0][:, 0:1],
                                   cf_ref[...], sf_ref[...], cnf_ref, mf_ref, tri_ref, False)
    ob_ref[...] = _mlstm_direction(qb_ref[...], kb_ref[...], vb_ref[...], gb_ref[0, 0], gbias_ref[1, 0][:, 0:1],
                                   cb_ref[...], sb_ref[...], cnb_ref, mb_ref, tri_ref, True)


def _mlstm(proj, gates_t, gbias, cos, sin, tri):
    def spec(col0, rev):
        return pl.BlockSpec((CHUNK, LANES), lambda b, h, s: (_chunk_block(b, s, rev), col0 + h))

    def gate_spec(rev):
        d = 1 if rev else 0
        return pl.BlockSpec((1, 1, 2, CHUNK), lambda b, h, s: (d, h, 0, _chunk_block(b, s, rev)))

    def rope_spec(rev):
        def idx(b, h, s):
            blk = _chunk_block(b, s, rev)
            lat = blk - b * N_LAT_CHUNKS
            return (jnp.where(s < N_CTX_CHUNKS, N_LAT_CHUNKS, lat), 0)
        return pl.BlockSpec((CHUNK, LANES), idx)

    out_spec = lambda rev: pl.BlockSpec((CHUNK, LANES), lambda b, h, s: (_chunk_block(b, s, rev), h))
    shape = jax.ShapeDtypeStruct((N_ALL, ML_HEADS * HEAD_DIM), F32)
    per_dir = lambda rev: [spec(32, rev), spec(36, rev), spec(40, rev), gate_spec(rev), rope_spec(rev), rope_spec(rev)]
    args_dir = [proj, proj, proj, gates_t, cos, sin]
    return pl.pallas_call(
        _mlstm_kernel,
        grid=(BATCH, ML_HEADS, N_STEPS),
        in_specs=per_dir(False) + per_dir(True) + [
            pl.BlockSpec((2, 1, 2, LANES), lambda b, h, s: (0, h, 0, 0)),
            pl.BlockSpec((3, CHUNK, CHUNK), lambda b, h, s: (0, 0, 0)),
        ],
        out_specs=[out_spec(False), out_spec(True)],
        out_shape=[shape, shape],
        scratch_shapes=[pltpu.VMEM((HEAD_DIM, 2 * HEAD_DIM), F32), pltpu.VMEM((HEAD_DIM, 2 * HEAD_DIM), F32),
                        pltpu.VMEM((8, LANES), F32), pltpu.VMEM((8, LANES), F32)],
        compiler_params=_params("parallel", "parallel", "arbitrary"),
        name="mlstm",
    )(*args_dir, *args_dir, gbias, tri)


def _head_rms(x, w):
    parts = []
    for hh in range(x.shape[-1] // HEAD_DIM):
        xs = x[:, hh * HEAD_DIM:(hh + 1) * HEAD_DIM]
        parts.append(xs * lax.rsqrt(jnp.mean(xs * xs, axis=-1, keepdims=True) + EPS))
    return jnp.concatenate(parts, axis=-1) * w


def _merge_kernel(na_ref, nac_ref, hgf_ref, hgb_ref, mlf_ref, mlb_ref, hgg_ref, mlo_ref, bg0_ref, bg1_ref, bg2_ref,
                  h_ref, mod_ref, hgw_ref, mlw_ref, wb_ref, wo_ref, o_ref):
    hg = _head_rms(hgf_ref[...] + hgb_ref[...], hgw_ref[...]) * _silu(hgg_ref[...])
    ml = _sigmoid(mlo_ref[...]) * _head_rms(mlf_ref[...] + mlb_ref[...], mlw_ref[...])
    is_ctx = pl.program_id(0) >= N_LAT // TM_MERGE
    na = jnp.where(is_ctx, nac_ref[...], na_ref[...])
    y = _sigmoid(bg0_ref[...]) * _dot(na.astype(BF16), wb_ref[0])
    y = y + _sigmoid(bg1_ref[...]) * _dot(hg.astype(BF16), wb_ref[1])
    y = y + _sigmoid(bg2_ref[...]) * _dot(ml.astype(BF16), wb_ref[2])
    o_ref[...] = h_ref[...] + mod_ref[0, 2:3, :] * _dot(y.astype(BF16), wo_ref[...])


def _merge(n_rows, na, na_ctx, hgf, hgb, mlf, mlb, proj, h_all, mod5, hg_w, ml_w, w_branch, w_out):
    tm = TM_MERGE
    tiles_per_batch = SEQ // tm
    n_lat_tiles = N_LAT // tm
    row = lambda w, c: pl.BlockSpec((tm, w), lambda i: (i, c))
    const = lambda shape: pl.BlockSpec(shape, lambda i: (0,) * len(shape))
    bg0 = BG_OFF // D_MODEL
    return pl.pallas_call(
        _merge_kernel,
        grid=(n_rows // tm,),
        in_specs=[
            pl.BlockSpec((tm, 512), lambda i: (jnp.minimum(i, n_lat_tiles - 1), 0)),
            pl.BlockSpec((tm, 512), lambda i: (jnp.maximum(i - n_lat_tiles, 0), 0)),
            row(512, 0), row(512, 0), row(512, 0), row(512, 0),
            row(512, 7), row(512, 11),
            row(D_MODEL, bg0), row(D_MODEL, bg0 + 1), row(D_MODEL, bg0 + 2),
            row(D_MODEL, 0),
            pl.BlockSpec((1, 6, D_MODEL), lambda i: (jnp.minimum(i // tiles_per_batch, BATCH), 0, 0)),
            const((1, 512)), const((1, 512)),
            const((3, BRANCH_WIDTH, D_MODEL)), const((D_MODEL, D_MODEL)),
        ],
        out_specs=row(D_MODEL, 0),
        out_shape=jax.ShapeDtypeStruct((n_rows, D_MODEL), F32),
        compiler_params=_params("parallel"),
        name="merge",
    )(na, na_ctx, hgf, hgb, mlf, mlb, proj, proj, proj, proj, proj, h_all, mod5, hg_w, ml_w, w_branch, w_out)


def _ffn_kernel(h_ref, nw_ref, mod_ref, wa_ref, wu_ref, wd_ref, o_ref, f_scr, acc_scr):
    j = pl.program_id(1)

    @pl.when(j == 0)
    def _():
        f = _norm_mod(h_ref[...], nw_ref[...], mod_ref[0, 3:4, :], mod_ref[0, 4:5, :])
        f_scr[...] = f.astype(BF16)
        acc_scr[...] = jnp.zeros_like(acc_scr)

    f = f_scr[...]
    g = _silu(_dot(f, wa_ref[...])) * _dot(f, wu_ref[...])
    acc_scr[...] += _dot(g.astype(BF16), wd_ref[...])

    @pl.when(j == pl.num_programs(1) - 1)
    def _():
        o_ref[...] = h_ref[...] + mod_ref[0, 5:6, :] * acc_scr[...]


def _ffn(h_all, nw, mod5, w_up, w_down):
    n_rows = h_all.shape[0]
    nj = FFN_DIM // TH_FFN
    tiles_per_batch = SEQ // TM_FFN
    return pl.pallas_call(
        _ffn_kernel,
        grid=(n_rows // TM_FFN, nj),
        in_specs=[
            pl.BlockSpec((TM_FFN, D_MODEL), lambda i, j: (i, 0)),
            pl.BlockSpec((1, D_MODEL), lambda i, j: (0, 0)),
            pl.BlockSpec((1, 6, D_MODEL), lambda i, j: (i // tiles_per_batch, 0, 0)),
            pl.BlockSpec((D_MODEL, TH_FFN), lambda i, j: (0, j)),
            pl.BlockSpec((D_MODEL, TH_FFN), lambda i, j: (0, nj + j)),
            pl.BlockSpec((TH_FFN, D_MODEL), lambda i, j: (j, 0)),
        ],
        out_specs=pl.BlockSpec((TM_FFN, D_MODEL), lambda i, j: (i, 0)),
        out_shape=jax.ShapeDtypeStruct((n_rows, D_MODEL), F32),
        scratch_shapes=[pltpu.VMEM((TM_FFN, D_MODEL), BF16), pltpu.VMEM((TM_FFN, D_MODEL), F32)],
        compiler_params=_params("parallel", "arbitrary"),
        name="ffn",
    )(h_all, nw, mod5, w_up, w_up, w_down)


def _router_kernel(h_ref, nw_ref, mod_ref, wr_ref, f_ref, r_ref):
    f = _norm_mod(h_ref[...], nw_ref[...], mod_ref[0, 3:4, :], mod_ref[0, 4:5, :])
    f_ref[...] = f
    logits = jnp.dot(f, wr_ref[...], preferred_element_type=F32, precision=lax.Precision.HIGHEST)
    lane = lax.broadcasted_iota(jnp.int32, logits.shape, 1)
    logits = jnp.where(lane < N_EXPERTS, logits, -jnp.inf)
    m1 = jnp.max(logits, axis=-1, keepdims=True)
    i1 = jnp.min(jnp.where(logits == m1, lane, LANES), axis=-1, keepdims=True)
    rest = jnp.where(lane == i1, -jnp.inf, logits)
    m2 = jnp.max(rest, axis=-1, keepdims=True)
    i2 = jnp.min(jnp.where(rest == m2, lane, LANES), axis=-1, keepdims=True)
    e2 = jnp.exp(m2 - m1)
    w1 = 1.0 / (1.0 + e2)
    w2 = e2 / (1.0 + e2)
    r_ref[...] = jnp.where(lane == 0, i1.astype(F32),
                           jnp.where(lane == 1, i2.astype(F32),
                                     jnp.where(lane == 2, w1, jnp.where(lane == 3, w2, 0.0))))


def _router(h_lat, nw, mod5, w_router_pad):
    tm = 512
    tiles_per_batch = SEQ // tm
    return pl.pallas_call(
        _router_kernel,
        grid=(N_LAT // tm,),
        in_specs=[
            pl.BlockSpec((tm, D_MODEL), lambda i: (i, 0)),
            pl.BlockSpec((1, D_MODEL), lambda i: (0, 0)),
            pl.BlockSpec((1, 6, D_MODEL), lambda i: (i // tiles_per_batch, 0, 0)),
            pl.BlockSpec((D_MODEL, LANES), lambda i: (0, 0)),
        ],
        out_specs=[pl.BlockSpec((tm, D_MODEL), lambda i: (i, 0)), pl.BlockSpec((tm, LANES), lambda i: (i, 0))],
        out_shape=[jax.ShapeDtypeStruct((N_LAT, D_MODEL), F32), jax.ShapeDtypeStruct((N_LAT, LANES), F32)],
        compiler_params=_params("parallel"),
        name="router",
    )(h_lat, nw, mod5, w_router_pad)


def _moe_kernel(be_ref, nused_ref, nvalid_ref, code_ref, f_hbm, wa_ref, wu_ref, wd_ref, y_hbm,
                xbuf, x16, acc, ybuf, sem_in, sem_out):
    i = pl.program_id(0)
    j = pl.program_id(1)
    last_j = pl.num_programs(1) - 1
    n_used = nused_ref[0]
    active = i < n_used
    slot = i % 2

    def start_gather(blk, buf):
        def body(r, c):
            tok = jnp.maximum(code_ref[blk * TM_MOE + r], 0) >> 1
            pltpu.make_async_copy(f_hbm.at[pl.ds(tok, 1)], xbuf.at[buf, pl.ds(r, 1)], sem_in.at[buf]).start()
            return c
        lax.fori_loop(0, TM_MOE, body, 0, unroll=8)

    def wait_gather(buf):
        pltpu.make_async_copy(f_hbm.at[pl.ds(0, TM_MOE)], xbuf.at[buf], sem_in.at[buf]).wait()

    def start_scatter(blk):
        def body(r, c):
            dst = code_ref[blk * TM_MOE + r]
            pltpu.make_async_copy(ybuf.at[pl.ds(r, 1)], y_hbm.at[pl.ds(dst, 1)], sem_out).start()
            return c
        lax.fori_loop(0, nvalid_ref[blk], body, 0)

    def wait_scatter(blk):
        n = nvalid_ref[blk]
        p = TM_MOE
        while p >= 8:
            @pl.when((n & p) != 0)
            def _(p=p):
                pltpu.make_async_copy(ybuf.at[pl.ds(0, p)], y_hbm.at[pl.ds(0, p)], sem_out).wait()
            p //= 2

        def one(r, c):
            pltpu.make_async_copy(ybuf.at[pl.ds(0, 1)], y_hbm.at[pl.ds(0, 1)], sem_out).wait()
            return c
        lax.fori_loop(0, n & 7, one, 0)

    @pl.when(jnp.logical_and(active, j == 0))
    def _():
        @pl.when(i == 0)
        def _():
            start_gather(0, 0)

        wait_gather(slot)
        x16[...] = xbuf[slot].astype(BF16)
        acc[...] = jnp.zeros_like(acc)

        @pl.when(i + 1 < n_used)
        def _():
            start_gather(i + 1, 1 - slot)

    @pl.when(active)
    def _():
        x = x16[...]
        g = _silu(_dot(x, wa_ref[0])) * _dot(x, wu_ref[0])
        acc[...] += _dot(g.astype(BF16), wd_ref[0])

    @pl.when(jnp.logical_and(active, j == last_j))
    def _():
        @pl.when(i > 0)
        def _():
            wait_scatter(i - 1)

        ybuf[...] = acc[...]
        start_scatter(i)

        @pl.when(i == n_used - 1)
        def _():
            wait_scatter(i)


def _moe_experts(block_e, n_used, n_valid, codes, f_lat, w_up, w_down):
    nj = EXPERT_DIM // TH_MOE

    def jj(i, j, nu):
        return jnp.where(i < nu[0], j, nj - 1)

    grid_spec = pltpu.PrefetchScalarGridSpec(
        num_scalar_prefetch=4,
        grid=(N_MOE_BLOCKS, nj),
        in_specs=[
            pl.BlockSpec(memory_space=pl.ANY),
            pl.BlockSpec((1, D_MODEL, TH_MOE), lambda i, j, be, nu, nv, cd: (be[i], 0, jj(i, j, nu))),
            pl.BlockSpec((1, D_MODEL, TH_MOE), lambda i, j, be, nu, nv, cd: (be[i], 0, nj + jj(i, j, nu))),
            pl.BlockSpec((1, TH_MOE, D_MODEL), lambda i, j, be, nu, nv, cd: (be[i], jj(i, j, nu), 0)),
        ],
        out_specs=pl.BlockSpec(memory_space=pl.ANY),
        scratch_shapes=[
            pltpu.VMEM((2, TM_MOE, D_MODEL), F32), pltpu.VMEM((TM_MOE, D_MODEL), BF16),
            pltpu.VMEM((TM_MOE, D_MODEL), F32), pltpu.VMEM((TM_MOE, D_MODEL), F32),
            pltpu.SemaphoreType.DMA((2,)), pltpu.SemaphoreType.DMA(()),
        ],
    )
    return pl.pallas_call(
        _moe_kernel,
        grid_spec=grid_spec,
        out_shape=jax.ShapeDtypeStruct((2 * N_LAT, D_MODEL), F32),
        compiler_params=_params("arbitrary", "arbitrary"),
        name="moe_experts",
    )(block_e, n_used, n_valid, codes, f_lat, w_up, w_up, w_down)


def _combine_kernel(h_ref, y_ref, r_ref, mod_ref, fw_ref, o_ref):
    r = r_ref[...]
    y = r[:, 2:3] * y_ref[:, :D_MODEL] + r[:, 3:4] * y_ref[:, D_MODEL:]
    h = h_ref[...] + mod_ref[0, 5:6, :] * y
    o_ref[...] = h * lax.rsqrt(jnp.mean(h * h, axis=-1, keepdims=True) + EPS) * fw_ref[...]


def _combine_final(h_lat, y_pairs, route, mod5, final_w):
    tm = 512
    tiles_per_batch = SEQ // tm
    return pl.pallas_call(
        _combine_kernel,
        grid=(N_LAT // tm,),
        in_specs=[
            pl.BlockSpec((tm, D_MODEL), lambda i: (i, 0)),
            pl.BlockSpec((tm, 2 * D_MODEL), lambda i: (i, 0)),
            pl.BlockSpec((tm, LANES), lambda i: (i, 0)),
            pl.BlockSpec((1, 6, D_MODEL), lambda i: (i // tiles_per_batch, 0, 0)),
            pl.BlockSpec((1, D_MODEL), lambda i: (0, 0)),
        ],
        out_specs=pl.BlockSpec((tm, D_MODEL), lambda i: (i, 0)),
        out_shape=jax.ShapeDtypeStruct((N_LAT, D_MODEL), F32),
        compiler_params=_params("parallel"),
        name="combine_final",
    )(h_lat, y_pairs, route, mod5, final_w)


def _moe_plan(route):
    e12 = route[:, 0:2].astype(jnp.int32)
    onehot = (e12[:, :, None] == jnp.arange(N_EXPERTS, dtype=jnp.int32)).astype(jnp.int32).sum(axis=1)
    before = jnp.cumsum(onehot, axis=0) - onehot
    counts = jnp.sum(onehot, axis=0)
    nblk = (counts + TM_MOE - 1) // TM_MOE
    blk_end = jnp.cumsum(nblk)
    slot0 = (blk_end - nblk) * TM_MOE
    rank = jnp.take_along_axis(before, e12, axis=1)
    dest = slot0[e12] + rank
    codes = jnp.full((N_SLOTS,), -1, jnp.int32).at[dest.reshape(-1)].set(jnp.arange(2 * N_LAT, dtype=jnp.int32))
    n_used = blk_end[-1]
    blocks = jnp.minimum(jnp.arange(N_MOE_BLOCKS, dtype=jnp.int32), n_used - 1)
    block_e = jnp.minimum(jnp.sum((blocks[:, None] >= blk_end[None, :]).astype(jnp.int32), axis=1), N_EXPERTS - 1)
    n_valid = jnp.sum((codes >= 0).astype(jnp.int32).reshape(N_MOE_BLOCKS, TM_MOE), axis=1)
    return block_e, n_used.reshape(1).astype(jnp.int32), n_valid, codes


def kernel(x, c, ctx, c_ctx, mod_w, mod_b, norm1_w, w_in, na_rpb, hg_lb, hg_norm_w, ml_gate_b, ml_norm_w,
           w_branch, w_out, norm2_w, ffn_w_up, ffn_w_down, moe_router, moe_w_up, moe_w_down, final_norm_w):
    h_all = jnp.concatenate([x.reshape(N_LAT, D_MODEL), ctx.reshape(N_CTX, D_MODEL)], axis=0)
    c8 = jnp.concatenate([c, c_ctx[None, :], jnp.zeros((3, D_MODEL), F32)], axis=0)
    mods = _modulation(c8, mod_w, mod_b).reshape(DEPTH, 8, 6, D_MODEL)

    lb_p = jax.nn.softmax(hg_lb.astype(F32), axis=0)
    hg_lower = jnp.cumsum(lb_p, axis=0) - lb_p[0]
    tri = _tri_consts()
    masks = _level_masks()
    cos, sin = _rope_tables()

    out = None
    for layer in range(DEPTH):
        last = layer == DEPTH - 1
        mod5 = mods[layer, :5]
        wl = w_in[layer]
        w_main = jnp.concatenate([wl[:, :MAIN_W], wl[:, MAIN_W + ML_GATE_COLS:]], axis=1).astype(BF16)
        w_gates = jnp.pad(wl[:, MAIN_W:MAIN_W + ML_GATE_COLS], ((0, 0), (0, LANES - ML_GATE_COLS))).astype(BF16)
        proj, gates = _inproj(h_all, norm1_w[layer][None, :], mod5, w_main, w_gates)

        na = _na_latent(proj, _na_bias_tables(na_rpb[layer]))
        na_ctx = na if last else _ctx_attention(proj)
        hgf, hgb = _hgrn2(proj, hg_lower[layer], tri, masks)
        gates_t = gates[:, :ML_GATE_COLS].T.reshape(2, 2, ML_HEADS, N_ALL).transpose(0, 2, 1, 3)
        gbias = jnp.broadcast_to(ml_gate_b[layer].transpose(0, 2, 1)[..., None], (2, ML_HEADS, 2, LANES))
        mlf, mlb = _mlstm(proj, gates_t, gbias, cos, sin, tri)

        n_rows = N_LAT if last else N_ALL
        h_all = _merge(n_rows, na, na_ctx, hgf, hgb, mlf, mlb, proj, h_all, mod5,
                       hg_norm_w[layer][None, :], ml_norm_w[layer][None, :],
                       w_branch[layer].astype(BF16), w_out[layer].astype(BF16))
        i = layer // 2
        if layer % 2 == 0:
            h_all = _ffn(h_all, norm2_w[layer][None, :], mod5, ffn_w_up[i].astype(BF16), ffn_w_down[i].astype(BF16))
            if last:
                raise NotImplementedError("final norm after a dense last layer")
        else:
            if not last:
                raise NotImplementedError("MoE on the context stream")
            w_router_pad = jnp.pad(moe_router[i], ((0, 0), (0, LANES - N_EXPERTS)))
            f_lat, route = _router(h_all, norm2_w[layer][None, :], mod5, w_router_pad)
            block_e, n_used, n_valid, codes = _moe_plan(route)
            y = _moe_experts(block_e, n_used, n_valid, codes, f_lat,
                             moe_w_up[i].astype(BF16), moe_w_down[i].astype(BF16))
            out = _combine_final(h_all, y.reshape(N_LAT, 2 * D_MODEL), route, mod5, final_norm_w[None, :])
    return out.reshape(BATCH, SEQ, D_MODEL)
```

```python
import functools

import numpy as np
import jax
import jax.numpy as jnp
from jax import lax
from jax.experimental import pallas as pl
from jax.experimental.pallas import tpu as pltpu

F32 = jnp.float32
BF16 = jnp.bfloat16

D_MODEL = 1024
BATCH = 4
SEQ = 4096
DEPTH = 2
GRID_W = 64
GRID_H = SEQ // GRID_W
CTX_LEN = 256
EPS = 1e-6
NEG_INF = -1e30
F_FLOOR = 1e-30
NA_HEADS = 8
NA_HEAD_DIM = 64
NA_WIN_ROWS = 8
NA_WIN_COLS = 16
HG_HEADS = 4
ML_HEADS = 4
HEAD_DIM = 128
ML_GATE_COLS = 16
ROPE_BASE = 10000.0
BRANCH_WIDTH = 512
FFN_DIM = 2816
N_EXPERTS = 8
EXPERT_DIM = 3584

N_LAT = BATCH * SEQ
N_CTX = BATCH * CTX_LEN
N_ALL = N_LAT + N_CTX

LANES = 128
VMEM_LIMIT = 56 * 1024 * 1024

MAIN_W = 12 * 512
BG_OFF = MAIN_W
PROJ_W = MAIN_W + 3 * D_MODEL

TM_PROJ = 1024
TN_PROJ = PROJ_W // 4
TM_MERGE = 256
TM_FFN = 512
TH_FFN = FFN_DIM // 2
NA_QROWS = 8
NA_KROWS = NA_QROWS + NA_WIN_ROWS
CHUNK = 256
TM_MOE = 512
TH_MOE = EXPERT_DIM // 2
N_MOE_BLOCKS = -(-(2 * N_LAT + N_EXPERTS * (TM_MOE - 1)) // TM_MOE)
N_SLOTS = N_MOE_BLOCKS * TM_MOE


def _params(*sem):
    return pltpu.CompilerParams(dimension_semantics=sem, vmem_limit_bytes=VMEM_LIMIT)


def _sigmoid(x):
    return 1.0 / (1.0 + jnp.exp(-x))


def _silu(x):
    return x * _sigmoid(x)


def _dot(a, b):
    return jnp.dot(a, b, preferred_element_type=F32)


def _dot_nt(a, b):
    return lax.dot_general(a, b, (((1,), (1,)), ((), ())), preferred_element_type=F32)


def _split3(x):
    hi = x.astype(BF16)
    r = x - hi.astype(F32)
    mid = r.astype(BF16)
    lo = (r - mid.astype(F32)).astype(BF16)
    return hi, mid, lo


def _sel_dot(sel, x):
    hi, mid, lo = _split3(x)
    return _dot(sel, lo) + _dot(sel, mid) + _dot(sel, hi)


def _sel_dot_nt(sel, x):
    hi, mid, lo = _split3(x)
    return _dot_nt(sel, lo) + _dot_nt(sel, mid) + _dot_nt(sel, hi)


def _dot_sel(x, sel):
    hi, mid, lo = _split3(x)
    return _dot(lo, sel) + _dot(mid, sel) + _dot(hi, sel)


def _norm_mod(x, nw, shift, scale):
    y = x * lax.rsqrt(jnp.mean(x * x, axis=-1, keepdims=True) + EPS) * nw
    return y * (1.0 + scale) + shift


def _mod_kernel(c_ref, w_ref, b_ref, o_ref):
    s = _silu(c_ref[...])
    o_ref[0] = jnp.dot(s, w_ref[0], preferred_element_type=F32, precision=lax.Precision.HIGHEST) + b_ref[0]


def _modulation(c8, mod_w, mod_b):
    tn = 1536
    return pl.pallas_call(
        _mod_kernel,
        grid=(DEPTH, 6 * D_MODEL // tn),
        in_specs=[
            pl.BlockSpec((8, D_MODEL), lambda l, j: (0, 0)),
            pl.BlockSpec((1, D_MODEL, tn), lambda l, j: (l, 0, j)),
            pl.BlockSpec((1, 1, tn), lambda l, j: (l, 0, j)),
        ],
        out_specs=pl.BlockSpec((1, 8, tn), lambda l, j: (l, 0, j)),
        out_shape=jax.ShapeDtypeStruct((DEPTH, 8, 6 * D_MODEL), F32),
        compiler_params=_params("parallel", "parallel"),
        name="modulation",
    )(c8, mod_w, mod_b.reshape(DEPTH, 1, 6 * D_MODEL))


def _inproj_kernel(h_ref, nw_ref, mod_ref, w_ref, wg_ref, o_ref, g_ref, a_scr):
    @pl.when(pl.program_id(1) == 0)
    def _():
        a = _norm_mod(h_ref[...], nw_ref[...], mod_ref[0, 0:1, :], mod_ref[0, 1:2, :])
        a_scr[...] = a.astype(BF16)
        g_ref[...] = _dot(a_scr[...], wg_ref[...])

    o_ref[...] = _dot(a_scr[...], w_ref[...])


def _inproj(h_all, nw, mod5, w_main, w_gates):
    tiles_per_batch = SEQ // TM_PROJ
    return pl.pallas_call(
        _inproj_kernel,
        grid=(N_ALL // TM_PROJ, PROJ_W // TN_PROJ),
        in_specs=[
            pl.BlockSpec((TM_PROJ, D_MODEL), lambda i, j: (i, 0)),
            pl.BlockSpec((1, D_MODEL), lambda i, j: (0, 0)),
            pl.BlockSpec((1, 6, D_MODEL), lambda i, j: (i // tiles_per_batch, 0, 0)),
            pl.BlockSpec((D_MODEL, TN_PROJ), lambda i, j: (0, j)),
            pl.BlockSpec((D_MODEL, LANES), lambda i, j: (0, 0)),
        ],
        out_specs=[pl.BlockSpec((TM_PROJ, TN_PROJ), lambda i, j: (i, j)),
                   pl.BlockSpec((TM_PROJ, LANES), lambda i, j: (i, 0))],
        out_shape=[jax.ShapeDtypeStruct((N_ALL, PROJ_W), F32), jax.ShapeDtypeStruct((N_ALL, LANES), F32)],
        scratch_shapes=[pltpu.VMEM((TM_PROJ, D_MODEL), BF16)],
        compiler_params=_params("parallel", "arbitrary"),
        name="inproj",
    )(h_all, nw, mod5, w_main, w_gates)


N_DR = 2 * NA_WIN_ROWS - 1


def _na_bias_tables(rpb):
    qc = np.arange(GRID_W)[:, None]
    kc = np.arange(GRID_W)[None, :]
    dc = np.clip(kc - qc + NA_WIN_COLS - 1, 0, 2 * NA_WIN_COLS - 2)
    ws = np.clip(qc - NA_WIN_COLS // 2, 0, GRID_W - NA_WIN_COLS)
    col_ok = (kc >= ws) & (kc < ws + NA_WIN_COLS)
    onehot = ((dc[None] == np.arange(2 * NA_WIN_COLS - 1)[:, None, None]) & col_ok[None]).astype(np.float32)
    t = jnp.einsum('hrd,dqk->hrqk', rpb.astype(F32), jnp.asarray(onehot), precision=lax.Precision.HIGHEST)
    t = t + jnp.asarray(np.where(col_ok, 0.0, NEG_INF).astype(np.float32))
    tp = jnp.pad(t, ((0, 0), (1, 2), (0, 0), (0, 0)))
    return jnp.concatenate([tp[:, :N_DR + 2], tp[:, 1:]], axis=-1)


def _na_kernel(q_ref, k_ref, v_ref, kc_ref, vc_ref, tab_ref, o_ref):
    blk = pl.program_id(2)
    q0 = blk * NA_QROWS
    k0 = jnp.clip(q0 - NA_WIN_ROWS // 2, 0, GRID_H - NA_KROWS)
    start = pl.multiple_of(k0 * GRID_W, GRID_W)
    nk = NA_KROWS * GRID_W
    lane = lax.broadcasted_iota(jnp.int32, (GRID_W, LANES), 1)

    tab_idx, penalty = [], []
    for qr in range(NA_QROWS):
        r = q0 + qr
        r0 = jnp.clip(r - NA_WIN_ROWS // 2, 0, GRID_H - NA_WIN_ROWS)
        idx_row, pen_row = [], []
        for j in range(NA_KROWS // 2):
            kra = k0 + 2 * j
            pa = jnp.where(jnp.logical_and(kra >= r0, kra < r0 + NA_WIN_ROWS), 0.0, NEG_INF)
            pb = jnp.where(jnp.logical_and(kra + 1 >= r0, kra + 1 < r0 + NA_WIN_ROWS), 0.0, NEG_INF)
            idx_row.append(jnp.clip(kra - r + NA_WIN_ROWS, 0, N_DR + 1))
            pen_row.append(jnp.where(lane < GRID_W, pa, pb))
        tab_idx.append(idx_row)
        penalty.append(pen_row)

    outs = []
    for hh in range(2):
        sl = slice(hh * NA_HEAD_DIM, (hh + 1) * NA_HEAD_DIM)
        q = (q_ref[:, sl] * (NA_HEAD_DIM ** -0.5)).astype(BF16)
        k = k_ref[pl.ds(start, nk), sl].astype(BF16)
        v = v_ref[pl.ds(start, nk), sl].astype(BF16)
        kc = kc_ref[:, sl].astype(BF16)
        vc = vc_ref[:, sl].astype(BF16)
        s_raw = _dot_nt(q, k)
        rows = []
        for qr in range(NA_QROWS):
            cols = []
            for j in range(NA_KROWS // 2):
                s_blk = s_raw[qr * GRID_W:(qr + 1) * GRID_W, j * LANES:(j + 1) * LANES]
                cols.append(s_blk + (tab_ref[hh, tab_idx[qr][j]] + penalty[qr][j]))
            rows.append(jnp.concatenate(cols, axis=1))
        s_loc = jnp.concatenate(rows, axis=0)
        s_ctx = _dot_nt(q, kc)
        m = jnp.maximum(jnp.max(s_loc, axis=-1, keepdims=True), jnp.max(s_ctx, axis=-1, keepdims=True))
        p_loc = jnp.exp(s_loc - m)
        p_ctx = jnp.exp(s_ctx - m)
        den = jnp.sum(p_loc, axis=-1, keepdims=True) + jnp.sum(p_ctx, axis=-1, keepdims=True)
        o = _dot(p_loc.astype(BF16), v) + _dot(p_ctx.astype(BF16), vc)
        outs.append(o / den)
    o_ref[...] = jnp.concatenate(outs, axis=-1)


def _na_latent(proj, bias):
    nq = NA_QROWS * GRID_W
    nblk = GRID_H // NA_QROWS
    ctx_blk0 = N_LAT // CTX_LEN
    return pl.pallas_call(
        _na_kernel,
        grid=(BATCH, NA_HEADS // 2, nblk),
        in_specs=[
            pl.BlockSpec((nq, LANES), lambda b, hp, blk: (b * nblk + blk, hp)),
            pl.BlockSpec((SEQ, LANES), lambda b, hp, blk: (b, 4 + hp)),
            pl.BlockSpec((SEQ, LANES), lambda b, hp, blk: (b, 8 + hp)),
            pl.BlockSpec((CTX_LEN, LANES), lambda b, hp, blk: (ctx_blk0 + b, 4 + hp)),
            pl.BlockSpec((CTX_LEN, LANES), lambda b, hp, blk: (ctx_blk0 + b, 8 + hp)),
            pl.BlockSpec((2, N_DR + 2, GRID_W, LANES), lambda b, hp, blk: (hp, 0, 0, 0)),
        ],
        out_specs=pl.BlockSpec((nq, LANES), lambda b, hp, blk: (b * nblk + blk, hp)),
        out_shape=jax.ShapeDtypeStruct((N_LAT, NA_HEADS * NA_HEAD_DIM), F32),
        compiler_params=_params("parallel", "parallel", "arbitrary"),
        name="na_latent",
    )(proj, proj, proj, proj, proj, bias)


def _ctx_attn_kernel(q_ref, k_ref, v_ref, o_ref):
    outs = []
    for hh in range(2):
        sl = slice(hh * NA_HEAD_DIM, (hh + 1) * NA_HEAD_DIM)
        q = (q_ref[:, sl] * (NA_HEAD_DIM ** -0.5)).astype(BF16)
        s = _dot_nt(q, k_ref[:, sl].astype(BF16))
        p = jnp.exp(s - jnp.max(s, axis=-1, keepdims=True))
        o = _dot(p.astype(BF16), v_ref[:, sl].astype(BF16))
        outs.append(o / jnp.sum(p, axis=-1, keepdims=True))
    o_ref[...] = jnp.concatenate(outs, axis=-1)


def _ctx_attention(proj):
    ctx_blk0 = N_LAT // CTX_LEN
    return pl.pallas_call(
        _ctx_attn_kernel,
        grid=(BATCH, NA_HEADS // 2),
        in_specs=[
            pl.BlockSpec((CTX_LEN, LANES), lambda b, hp: (ctx_blk0 + b, hp)),
            pl.BlockSpec((CTX_LEN, LANES), lambda b, hp: (ctx_blk0 + b, 4 + hp)),
            pl.BlockSpec((CTX_LEN, LANES), lambda b, hp: (ctx_blk0 + b, 8 + hp)),
        ],
        out_specs=pl.BlockSpec((CTX_LEN, LANES), lambda b, hp: (b, hp)),
        out_shape=jax.ShapeDtypeStruct((N_CTX, NA_HEADS * NA_HEAD_DIM), F32),
        compiler_params=_params("parallel", "parallel"),
        name="ctx_attention",
    )(proj, proj, proj)


N_CTX_CHUNKS = CTX_LEN // CHUNK
N_LAT_CHUNKS = SEQ // CHUNK
N_STEPS = N_CTX_CHUNKS + N_LAT_CHUNKS
SUB = CHUNK // 2
HG_LEVELS = SUB.bit_length() - 1


def _chunk_block(b, s, rev):
    c_ctx = (N_CTX_CHUNKS - 1 - s) if rev else s
    c_lat = (N_LAT_CHUNKS - 1 - (s - N_CTX_CHUNKS)) if rev else (s - N_CTX_CHUNKS)
    ctx_blk = N_LAT // CHUNK + b * N_CTX_CHUNKS + c_ctx
    lat_blk = b * N_LAT_CHUNKS + c_lat
    return jnp.where(s < N_CTX_CHUNKS, ctx_blk, lat_blk)


def _tri_consts():
    i = np.arange(CHUNK)
    low = (i[None, :] <= i[:, None]).astype(np.float32)
    eye = np.eye(CHUNK, dtype=np.float32)
    return jnp.asarray(np.stack([low, low.T, eye]), BF16)


def _hgrn_consts():
    t = np.arange(SUB)[:, None]
    s = np.arange(SUB)[None, :]
    low = (s <= t).astype(np.float32)
    masks = np.zeros((2, HG_LEVELS + 1, SUB, SUB), np.float32)
    for l in range(HG_LEVELS):
        pair = ((t ^ s) >> l) == 1
        masks[0, l] = pair & (t > s)
        masks[1, l] = pair & (t < s)
    masks[:, HG_LEVELS] = (t == s)
    return jnp.asarray(np.stack([low, low.T]), BF16), jnp.asarray(masks)


def _hgrn_chain(q_ref, z_ref, v_ref, o_ref, sl, lb, st_ref, tri, masks_ref, rev):
    L, S = CHUNK, SUB
    z = z_ref[:, sl]
    v = v_ref[:, sl]
    e = jnp.exp(-jnp.abs(z))
    r = 1.0 / (1.0 + e)
    pos = z >= 0
    sig = jnp.where(pos, r, e * r)
    nsig = jnp.where(pos, e * r, r)
    k = (1.0 - lb) * nsig
    log2f = jnp.log2(jnp.maximum(lb + (1.0 - lb) * sig, F_FLOOR))
    q = _silu(q_ref[:, sl])
    yield

    g0 = _sel_dot(tri, log2f[:S])
    g1 = _sel_dot(tri, log2f[S:])
    if rev:
        g0 = g0 + g1[0:1, :]
    else:
        g1 = g1 + g0[S - 1:S, :]
    g = jnp.concatenate([g0, g1], axis=0)
    row = lax.broadcasted_iota(jnp.int32, (L, HEAD_DIM), 0)

    st = st_ref[...]
    o_inter = _dot_nt((q * jnp.exp2(g)).astype(BF16), st.astype(BF16))
    yield

    a = [jnp.zeros((S, S), F32), jnp.zeros((S, S), F32)]
    cross = None
    q16 = q.astype(BF16)
    k16 = k.astype(BF16)
    bnd = g
    for l in range(HG_LEVELS + 1):
        blk = 1 << l
        q_side = ((row & blk) == 0) if rev else ((row & blk) != 0)
        prev_end = pltpu.roll(bnd, (L - blk) if rev else blk, 0)
        w16 = jnp.exp2(jnp.where(q_side, g - prev_end, bnd - g)).astype(BF16)
        qb = q16 * w16
        kb = k16 * w16
        if blk < S:
            for c in range(2):
                a[c] = a[c] + masks_ref[l] * _dot_nt(qb[c * S:(c + 1) * S], kb[c * S:(c + 1) * S])
        elif rev:
            cross = _dot_nt(qb[:S], kb[S:])
        else:
            cross = _dot_nt(qb[S:], kb[:S])
        nxt = pltpu.roll(bnd, blk if rev else (L - blk), 0)
        bnd = jnp.where(q_side, bnd, nxt)
        yield
    v16 = v.astype(BF16)
    for c in range(2):
        a[c] = a[c] + masks_ref[HG_LEVELS] * _dot_nt(q16[c * S:(c + 1) * S], k16[c * S:(c + 1) * S])
    if rev:
        o0 = _dot(jnp.concatenate([a[0], cross], axis=1).astype(BF16), v16)
        o1 = _dot(a[1].astype(BF16), v16[S:])
    else:
        o0 = _dot(a[0].astype(BF16), v16[:S])
        o1 = _dot(jnp.concatenate([cross, a[1]], axis=1).astype(BF16), v16)
    o_ref[:, sl] = o_inter + jnp.concatenate([o0, o1], axis=0)
    yield

    kd = k * jnp.exp2(bnd - g)
    st_ref[...] = jnp.exp2(bnd[0:1, :]) * st + lax.dot_general(
        v16, kd.astype(BF16), (((0,), (0,)), ((), ())), preferred_element_type=F32)


def _round_robin(chains):
    alive = list(chains)
    while alive:
        still = []
        for c in alive:
            try:
                next(c)
                still.append(c)
            except StopIteration:
                pass
        alive = still


def _hgrn_kernel(qf_ref, zf_ref, vf_ref, qb_ref, zb_ref, vb_ref, lb_ref, tri_ref, masks_ref,
                 of_ref, ob_ref, stf_ref, stb_ref):
    @pl.when(pl.program_id(1) == 0)
    def _():
        stf_ref[...] = jnp.zeros_like(stf_ref)
        stb_ref[...] = jnp.zeros_like(stb_ref)

    chains = []
    for h in range(HG_HEADS):
        sl = slice(h * HEAD_DIM, (h + 1) * HEAD_DIM)
        chains.append(_hgrn_chain(qf_ref, zf_ref, vf_ref, of_ref, sl, lb_ref[0:1, sl], stf_ref.at[h],
                                  tri_ref[0], masks_ref.at[0], False))
        chains.append(_hgrn_chain(qb_ref, zb_ref, vb_ref, ob_ref, sl, lb_ref[1:2, sl], stb_ref.at[h],
                                  tri_ref[1], masks_ref.at[1], True))
    _round_robin(chains)


def _hgrn2(proj, lower, tri, masks):
    width = HG_HEADS * HEAD_DIM

    def spec(col, rev):
        return pl.BlockSpec((CHUNK, width), lambda b, s: (_chunk_block(b, s, rev), col))

    out_spec = lambda rev: pl.BlockSpec((CHUNK, width), lambda b, s: (_chunk_block(b, s, rev), 0))
    shape = jax.ShapeDtypeStruct((N_ALL, width), F32)
    return pl.pallas_call(
        _hgrn_kernel,
        grid=(BATCH, N_STEPS),
        in_specs=[
            spec(3, False), spec(4, False), spec(6, False),
            spec(3, True), spec(5, True), spec(6, True),
            pl.BlockSpec((2, width), lambda b, s: (0, 0)),
            pl.BlockSpec((2, SUB, SUB), lambda b, s: (0, 0, 0)),
            pl.BlockSpec((2, HG_LEVELS + 1, SUB, SUB), lambda b, s: (0, 0, 0, 0)),
        ],
        out_specs=[out_spec(False), out_spec(True)],
        out_shape=[shape, shape],
        scratch_shapes=[pltpu.VMEM((HG_HEADS, HEAD_DIM, HEAD_DIM), F32), pltpu.VMEM((HG_HEADS, HEAD_DIM, HEAD_DIM), F32)],
        compiler_params=_params("parallel", "arbitrary"),
        name="hgrn2",
    )(proj, proj, proj, proj, proj, proj, lower, tri, masks)


def _rope_tables():
    n_freq = HEAD_DIM // 4
    inv_freq = ROPE_BASE ** (-jnp.arange(n_freq, dtype=F32) / n_freq)
    t = jnp.arange(SEQ, dtype=jnp.int32)
    ang_r = (t // GRID_W).astype(F32)[:, None] * inv_freq
    ang_c = (t % GRID_W).astype(F32)[:, None] * inv_freq
    cos = jnp.concatenate([jnp.cos(ang_r), jnp.cos(ang_r), jnp.cos(ang_c), jnp.cos(ang_c)], axis=-1)
    sin = jnp.concatenate([-jnp.sin(ang_r), jnp.sin(ang_r), -jnp.sin(ang_c), jnp.sin(ang_c)], axis=-1)
    cos = jnp.concatenate([cos, jnp.ones((CHUNK, HEAD_DIM), F32)], axis=0)
    sin = jnp.concatenate([sin, jnp.zeros((CHUNK, HEAD_DIM), F32)], axis=0)
    return cos, sin


def _rope_swap():
    l = np.arange(HEAD_DIM)
    return jnp.asarray((l[:, None] == (l[None, :] ^ 32)).astype(np.float32), BF16)


def _rope(x, cos, sin, swap):
    hi = x.astype(BF16)
    mid = (x - hi.astype(F32)).astype(BF16)
    partner = _dot(hi, swap) + _dot(mid, swap)
    return x * cos + partner * sin


def _mlstm_gates(g_ref, gbias, tri_ref, rev):
    L = CHUNK
    log_i = g_ref[0, 0] + gbias[0][:, 0:1]
    xf = g_ref[0, 1] + gbias[1][:, 0:1]
    log_f = jnp.minimum(xf, 0.0) - jnp.log(1.0 + jnp.exp(-jnp.abs(xf)))
    r8 = jnp.concatenate([log_f, log_i], axis=0)
    low, up, eye = tri_ref[0], tri_ref[1], tri_ref[2]
    b_row = _dot_sel(r8, low if rev else up)[0:ML_HEADS, :]
    b_col = _sel_dot_nt(up if rev else low, r8)[:, 0:ML_HEADS]
    i_col = _sel_dot_nt(eye, r8)[:, ML_HEADS:2 * ML_HEADS]
    return log_i, b_row, b_col, i_col


def _mlstm_kernel(qf_ref, kf_ref, vf_ref, gf_ref, cf_ref, sf_ref,
                  qb_ref, kb_ref, vb_ref, gb_ref, cb_ref, sb_ref,
                  gbias_ref, tri_ref, swap_ref, of_ref, ob_ref, cnf_ref, cnb_ref, mf_ref, mb_ref):
    @pl.when(pl.program_id(1) == 0)
    def _():
        cnf_ref[...] = jnp.zeros_like(cnf_ref)
        cnb_ref[...] = jnp.zeros_like(cnb_ref)
        mf_ref[...] = jnp.zeros_like(mf_ref)
        mb_ref[...] = jnp.zeros_like(mb_ref)

    L = CHUNK
    ti = lax.broadcasted_iota(jnp.int32, (L, L), 0)
    si = lax.broadcasted_iota(jnp.int32, (L, L), 1)
    ones = jnp.ones((L, HEAD_DIM), BF16)
    swap = swap_ref[...]
    dirs = ((qf_ref, kf_ref, vf_ref, gf_ref, cf_ref, sf_ref, of_ref, cnf_ref, mf_ref, False),
            (qb_ref, kb_ref, vb_ref, gb_ref, cb_ref, sb_ref, ob_ref, cnb_ref, mb_ref, True))

    chains = []
    for d, (q_ref, k_ref, v_ref, g_ref, c_ref, s_ref, o_ref, cn_ref, m_ref, rev) in enumerate(dirs):
        log_i, b_row, b_col, i_col = _mlstm_gates(g_ref, gbias_ref[d], tri_ref, rev)
        for h in range(ML_HEADS):
            bc = jnp.broadcast_to(b_col[:, h:h + 1], (L, HEAD_DIM))
            ic = jnp.broadcast_to(i_col[:, h:h + 1], (L, HEAD_DIM))
            chains.append(dict(
                sl=slice(h * HEAD_DIM, (h + 1) * HEAD_DIM), rev=rev, q_ref=q_ref, k_ref=k_ref, v_ref=v_ref,
                c_ref=c_ref, s_ref=s_ref, o_ref=o_ref, cn_ref=cn_ref.at[h], m_ref=m_ref.at[h],
                log_i=log_i[h:h + 1, :], b_row=b_row[h:h + 1, :], b_col=bc, i_col=ic))

    for c in chains:
        cos, sin = c['c_ref'][...], c['s_ref'][...]
        c['qc'] = _rope(c['q_ref'][:, c['sl']], cos, sin, swap).astype(BF16)
        kc = _rope(c['k_ref'][:, c['sl']] * (HEAD_DIM ** -0.5), cos, sin, swap)
        c['kc'] = kc
        c['s'] = _dot_nt(c['qc'], kc.astype(BF16))
    for c in chains:
        b_row = c['b_row']
        c['b_end'] = b_row[:, 0:1] if c['rev'] else b_row[:, L - 1:L]
        tri = (si >= ti) if c['rev'] else (si <= ti)
        c['m_prev'] = c['m_ref'][0:1, 0:1]
        bc2 = jnp.concatenate([c['b_col'], c['b_col']], axis=-1)
        dmat = jnp.where(tri, bc2 + (c['log_i'] - b_row), NEG_INF)
        inter = c['b_col'] + c['m_prev']
        m_t = jnp.maximum(inter, jnp.max(dmat, axis=-1, keepdims=True))
        c['m_t'] = m_t
        c['w_inter'] = jnp.exp(inter - m_t)
        c['p'] = (jnp.exp(dmat - jnp.concatenate([m_t, m_t], axis=-1)) * c['s']).astype(BF16)
    for c in chains:
        c['v_ext'] = jnp.concatenate([c['v_ref'][:, c['sl']].astype(BF16), ones], axis=-1)
        c['cn'] = c['cn_ref'][...]
        w2 = jnp.concatenate([c['w_inter'], c['w_inter']], axis=-1)
        acc = _dot(c['p'], c['v_ext']) + w2 * _dot(c['qc'], c['cn'].astype(BF16))
        den = acc[:, HEAD_DIM:]
        c['o_ref'][:, c['sl']] = acc[:, :HEAD_DIM] / jnp.maximum(jnp.abs(den), jnp.exp(-c['m_t']))
    for c in chains:
        e_row = c['b_end'] + (c['log_i'] - c['b_row'])
        m_new = jnp.maximum(c['b_end'] + c['m_prev'], jnp.max(e_row, axis=-1, keepdims=True))
        w_old = jnp.exp(c['b_end'] + c['m_prev'] - m_new)
        w_s = jnp.exp(c['b_end'] - c['b_col'] + c['i_col'] - m_new)
        c['cn_ref'][...] = w_old * c['cn'] + lax.dot_general(
            (w_s * c['kc']).astype(BF16), c['v_ext'], (((0,), (0,)), ((), ())), preferred_element_type=F32)
        c['m_ref'][...] = jnp.broadcast_to(m_new, c['m_ref'].shape)


def _mlstm(proj, gates_t, gbias, cos, sin, tri):
    width = ML_HEADS * HEAD_DIM

    def spec(col, rev):
        return pl.BlockSpec((CHUNK, width), lambda b, s: (_chunk_block(b, s, rev), col))

    def gate_spec(rev):
        d = 1 if rev else 0
        return pl.BlockSpec((1, 2, ML_HEADS, CHUNK), lambda b, s: (d, 0, 0, _chunk_block(b, s, rev)))

    def rope_spec(rev):
        def idx(b, s):
            lat = _chunk_block(b, s, rev) - b * N_LAT_CHUNKS
            return (jnp.where(s < N_CTX_CHUNKS, N_LAT_CHUNKS, lat), 0)
        return pl.BlockSpec((CHUNK, LANES), idx)

    out_spec = lambda rev: pl.BlockSpec((CHUNK, width), lambda b, s: (_chunk_block(b, s, rev), 0))
    shape = jax.ShapeDtypeStruct((N_ALL, width), F32)
    per_dir = lambda rev: [spec(8, rev), spec(9, rev), spec(10, rev), gate_spec(rev), rope_spec(rev), rope_spec(rev)]
    args_dir = [proj, proj, proj, gates_t, cos, sin]
    return pl.pallas_call(
        _mlstm_kernel,
        grid=(BATCH, N_STEPS),
        in_specs=per_dir(False) + per_dir(True) + [
            pl.BlockSpec((2, 2, ML_HEADS, LANES), lambda b, s: (0, 0, 0, 0)),
            pl.BlockSpec((3, CHUNK, CHUNK), lambda b, s: (0, 0, 0)),
            pl.BlockSpec((HEAD_DIM, HEAD_DIM), lambda b, s: (0, 0)),
        ],
        out_specs=[out_spec(False), out_spec(True)],
        out_shape=[shape, shape],
        scratch_shapes=[pltpu.VMEM((ML_HEADS, HEAD_DIM, 2 * HEAD_DIM), F32),
                        pltpu.VMEM((ML_HEADS, HEAD_DIM, 2 * HEAD_DIM), F32),
                        pltpu.VMEM((ML_HEADS, 8, LANES), F32), pltpu.VMEM((ML_HEADS, 8, LANES), F32)],
        compiler_params=_params("parallel", "arbitrary"),
        name="mlstm",
    )(*args_dir, *args_dir, gbias, tri, _rope_swap())


def _head_rms(x, w):
    parts = []
    for hh in range(x.shape[-1] // HEAD_DIM):
        xs = x[:, hh * HEAD_DIM:(hh + 1) * HEAD_DIM]
        parts.append(xs * lax.rsqrt(jnp.mean(xs * xs, axis=-1, keepdims=True) + EPS))
    return jnp.concatenate(parts, axis=-1) * w


def _merge_kernel(na_ref, nac_ref, hgf_ref, hgb_ref, mlf_ref, mlb_ref, hgg_ref, mlo_ref, bg0_ref, bg1_ref, bg2_ref,
                  h_ref, mod_ref, hgw_ref, mlw_ref, wb_ref, wo_ref, o_ref):
    hg = _head_rms(hgf_ref[...] + hgb_ref[...], hgw_ref[...]) * _silu(hgg_ref[...])
    ml = _sigmoid(mlo_ref[...]) * _head_rms(mlf_ref[...] + mlb_ref[...], mlw_ref[...])
    is_ctx = pl.program_id(0) >= N_LAT // TM_MERGE
    na = jnp.where(is_ctx, nac_ref[...], na_ref[...])
    y = _sigmoid(bg0_ref[...]) * _dot(na.astype(BF16), wb_ref[0])
    y = y + _sigmoid(bg1_ref[...]) * _dot(hg.astype(BF16), wb_ref[1])
    y = y + _sigmoid(bg2_ref[...]) * _dot(ml.astype(BF16), wb_ref[2])
    o_ref[...] = h_ref[...] + mod_ref[0, 2:3, :] * _dot(y.astype(BF16), wo_ref[...])


def _merge(n_rows, na, na_ctx, hgf, hgb, mlf, mlb, proj, h_all, mod5, hg_w, ml_w, w_branch, w_out):
    tm = TM_MERGE
    tiles_per_batch = SEQ // tm
    n_lat_tiles = N_LAT // tm
    row = lambda w, c: pl.BlockSpec((tm, w), lambda i: (i, c))
    const = lambda shape: pl.BlockSpec(shape, lambda i: (0,) * len(shape))
    bg0 = BG_OFF // D_MODEL
    return pl.pallas_call(
        _merge_kernel,
        grid=(n_rows // tm,),
        in_specs=[
            pl.BlockSpec((tm, 512), lambda i: (jnp.minimum(i, n_lat_tiles - 1), 0)),
            pl.BlockSpec((tm, 512), lambda i: (jnp.maximum(i - n_lat_tiles, 0), 0)),
            row(512, 0), row(512, 0), row(512, 0), row(512, 0),
            row(512, 7), row(512, 11),
            row(D_MODEL, bg0), row(D_MODEL, bg0 + 1), row(D_MODEL, bg0 + 2),
            row(D_MODEL, 0),
            pl.BlockSpec((1, 6, D_MODEL), lambda i: (jnp.minimum(i // tiles_per_batch, BATCH), 0, 0)),
            const((1, 512)), const((1, 512)),
            const((3, BRANCH_WIDTH, D_MODEL)), const((D_MODEL, D_MODEL)),
        ],
        out_specs=row(D_MODEL, 0),
        out_shape=jax.ShapeDtypeStruct((n_rows, D_MODEL), F32),
        compiler_params=_params("parallel"),
        name="merge",
    )(na, na_ctx, hgf, hgb, mlf, mlb, proj, proj, proj, proj, proj, h_all, mod5, hg_w, ml_w, w_branch, w_out)


def _ffn_kernel(h_ref, nw_ref, mod_ref, wa_ref, wu_ref, wd_ref, o_ref, f_scr, acc_scr):
    j = pl.program_id(1)

    @pl.when(j == 0)
    def _():
        f = _norm_mod(h_ref[...], nw_ref[...], mod_ref[0, 3:4, :], mod_ref[0, 4:5, :])
        f_scr[...] = f.astype(BF16)
        acc_scr[...] = jnp.zeros_like(acc_scr)

    f = f_scr[...]
    g = _silu(_dot(f, wa_ref[...])) * _dot(f, wu_ref[...])
    acc_scr[...] += _dot(g.astype(BF16), wd_ref[...])

    @pl.when(j == pl.num_programs(1) - 1)
    def _():
        o_ref[...] = h_ref[...] + mod_ref[0, 5:6, :] * acc_scr[...]


def _ffn(h_all, nw, mod5, w_up, w_down):
    n_rows = h_all.shape[0]
    nj = FFN_DIM // TH_FFN
    tiles_per_batch = SEQ // TM_FFN
    return pl.pallas_call(
        _ffn_kernel,
        grid=(n_rows // TM_FFN, nj),
        in_specs=[
            pl.BlockSpec((TM_FFN, D_MODEL), lambda i, j: (i, 0)),
            pl.BlockSpec((1, D_MODEL), lambda i, j: (0, 0)),
            pl.BlockSpec((1, 6, D_MODEL), lambda i, j: (i // tiles_per_batch, 0, 0)),
            pl.BlockSpec((D_MODEL, TH_FFN), lambda i, j: (0, j)),
            pl.BlockSpec((D_MODEL, TH_FFN), lambda i, j: (0, nj + j)),
            pl.BlockSpec((TH_FFN, D_MODEL), lambda i, j: (j, 0)),
        ],
        out_specs=pl.BlockSpec((TM_FFN, D_MODEL), lambda i, j: (i, 0)),
        out_shape=jax.ShapeDtypeStruct((n_rows, D_MODEL), F32),
        scratch_shapes=[pltpu.VMEM((TM_FFN, D_MODEL), BF16), pltpu.VMEM((TM_FFN, D_MODEL), F32)],
        compiler_params=_params("parallel", "arbitrary"),
        name="ffn",
    )(h_all, nw, mod5, w_up, w_up, w_down)


def _router_kernel(h_ref, nw_ref, mod_ref, wr_ref, f_ref, r_ref):
    f = _norm_mod(h_ref[...], nw_ref[...], mod_ref[0, 3:4, :], mod_ref[0, 4:5, :])
    f_ref[...] = f
    logits = jnp.dot(f, wr_ref[...], preferred_element_type=F32, precision=lax.Precision.HIGHEST)
    lane = lax.broadcasted_iota(jnp.int32, logits.shape, 1)
    logits = jnp.where(lane < N_EXPERTS, logits, -jnp.inf)
    m1 = jnp.max(logits, axis=-1, keepdims=True)
    i1 = jnp.min(jnp.where(logits == m1, lane, LANES), axis=-1, keepdims=True)
    rest = jnp.where(lane == i1, -jnp.inf, logits)
    m2 = jnp.max(rest, axis=-1, keepdims=True)
    i2 = jnp.min(jnp.where(rest == m2, lane, LANES), axis=-1, keepdims=True)
    e2 = jnp.exp(m2 - m1)
    w1 = 1.0 / (1.0 + e2)
    w2 = e2 / (1.0 + e2)
    r_ref[...] = jnp.where(lane == 0, i1.astype(F32),
                           jnp.where(lane == 1, i2.astype(F32),
                                     jnp.where(lane == 2, w1, jnp.where(lane == 3, w2, 0.0))))


def _router(h_lat, nw, mod5, w_router_pad):
    tm = 512
    tiles_per_batch = SEQ // tm
    return pl.pallas_call(
        _router_kernel,
        grid=(N_LAT // tm,),
        in_specs=[
            pl.BlockSpec((tm, D_MODEL), lambda i: (i, 0)),
            pl.BlockSpec((1, D_MODEL), lambda i: (0, 0)),
            pl.BlockSpec((1, 6, D_MODEL), lambda i: (i // tiles_per_batch, 0, 0)),
            pl.BlockSpec((D_MODEL, LANES), lambda i: (0, 0)),
        ],
        out_specs=[pl.BlockSpec((tm, D_MODEL), lambda i: (i, 0)), pl.BlockSpec((tm, LANES), lambda i: (i, 0))],
        out_shape=[jax.ShapeDtypeStruct((N_LAT, D_MODEL), F32), jax.ShapeDtypeStruct((N_LAT, LANES), F32)],
        compiler_params=_params("parallel"),
        name="router",
    )(h_lat, nw, mod5, w_router_pad)


def _moe_kernel(be_ref, nused_ref, nvalid_ref, code_ref, f_hbm, wa_ref, wu_ref, wd_ref, y_hbm,
                xbuf, x16, acc, ybuf, sem_in, sem_out):
    i = pl.program_id(0)
    j = pl.program_id(1)
    last_j = pl.num_programs(1) - 1
    n_used = nused_ref[0]
    active = i < n_used
    slot = i % 2

    def start_gather(blk, buf):
        def body(r, c):
            tok = jnp.maximum(code_ref[blk * TM_MOE + r], 0) >> 1
            pltpu.make_async_copy(f_hbm.at[pl.ds(tok, 1)], xbuf.at[buf, pl.ds(r, 1)], sem_in.at[buf]).start()
            return c
        lax.fori_loop(0, TM_MOE, body, 0, unroll=8)

    def wait_gather(buf):
        pltpu.make_async_copy(f_hbm.at[pl.ds(0, TM_MOE)], xbuf.at[buf], sem_in.at[buf]).wait()

    def start_scatter(blk):
        def body(r, c):
            code = code_ref[blk * TM_MOE + r]
            dst = (code & 1) * N_LAT + (code >> 1)
            pltpu.make_async_copy(ybuf.at[pl.ds(r, 1)], y_hbm.at[pl.ds(dst, 1)], sem_out).start()
            return c
        lax.fori_loop(0, nvalid_ref[blk], body, 0)

    def wait_scatter(blk):
        n = nvalid_ref[blk]
        p = TM_MOE
        while p >= 8:
            @pl.when((n & p) != 0)
            def _(p=p):
                pltpu.make_async_copy(ybuf.at[pl.ds(0, p)], y_hbm.at[pl.ds(0, p)], sem_out).wait()
            p //= 2

        def one(r, c):
            pltpu.make_async_copy(ybuf.at[pl.ds(0, 1)], y_hbm.at[pl.ds(0, 1)], sem_out).wait()
            return c
        lax.fori_loop(0, n & 7, one, 0)

    @pl.when(jnp.logical_and(active, j == 0))
    def _():
        @pl.when(i == 0)
        def _():
            start_gather(0, 0)

        wait_gather(slot)
        x16[...] = xbuf[slot].astype(BF16)
        acc[...] = jnp.zeros_like(acc)

        @pl.when(i + 1 < n_used)
        def _():
            start_gather(i + 1, 1 - slot)

    @pl.when(active)
    def _():
        x = x16[...]
        g = _silu(_dot(x, wa_ref[0])) * _dot(x, wu_ref[0])
        acc[...] += _dot(g.astype(BF16), wd_ref[0])

    @pl.when(jnp.logical_and(active, j == last_j))
    def _():
        @pl.when(i > 0)
        def _():
            wait_scatter(i - 1)

        ybuf[...] = acc[...]
        start_scatter(i)

        @pl.when(i == n_used - 1)
        def _():
            wait_scatter(i)


def _moe_experts(block_e, n_used, n_valid, codes, f_lat, w_up, w_down):
    nj = EXPERT_DIM // TH_MOE

    def jj(i, j, nu):
        return jnp.where(i < nu[0], j, nj - 1)

    grid_spec = pltpu.PrefetchScalarGridSpec(
        num_scalar_prefetch=4,
        grid=(N_MOE_BLOCKS, nj),
        in_specs=[
            pl.BlockSpec(memory_space=pl.ANY),
            pl.BlockSpec((1, D_MODEL, TH_MOE), lambda i, j, be, nu, nv, cd: (be[i], 0, jj(i, j, nu))),
            pl.BlockSpec((1, D_MODEL, TH_MOE), lambda i, j, be, nu, nv, cd: (be[i], 0, nj + jj(i, j, nu))),
            pl.BlockSpec((1, TH_MOE, D_MODEL), lambda i, j, be, nu, nv, cd: (be[i], jj(i, j, nu), 0)),
        ],
        out_specs=pl.BlockSpec(memory_space=pl.ANY),
        scratch_shapes=[
            pltpu.VMEM((2, TM_MOE, D_MODEL), F32), pltpu.VMEM((TM_MOE, D_MODEL), BF16),
            pltpu.VMEM((TM_MOE, D_MODEL), F32), pltpu.VMEM((TM_MOE, D_MODEL), F32),
            pltpu.SemaphoreType.DMA((2,)), pltpu.SemaphoreType.DMA(()),
        ],
    )
    return pl.pallas_call(
        _moe_kernel,
        grid_spec=grid_spec,
        out_shape=jax.ShapeDtypeStruct((2 * N_LAT, D_MODEL), F32),
        compiler_params=_params("arbitrary", "arbitrary"),
        name="moe_experts",
    )(block_e, n_used, n_valid, codes, f_lat, w_up, w_up, w_down)


def _combine_kernel(h_ref, y1_ref, y2_ref, r_ref, mod_ref, fw_ref, o_ref):
    r = r_ref[...]
    y = r[:, 2:3] * y1_ref[...] + r[:, 3:4] * y2_ref[...]
    h = h_ref[...] + mod_ref[0, 5:6, :] * y
    o_ref[...] = h * lax.rsqrt(jnp.mean(h * h, axis=-1, keepdims=True) + EPS) * fw_ref[...]


def _combine_final(h_lat, y, route, mod5, final_w):
    tm = 512
    tiles_per_batch = SEQ // tm
    return pl.pallas_call(
        _combine_kernel,
        grid=(N_LAT // tm,),
        in_specs=[
            pl.BlockSpec((tm, D_MODEL), lambda i: (i, 0)),
            pl.BlockSpec((tm, D_MODEL), lambda i: (i, 0)),
            pl.BlockSpec((tm, D_MODEL), lambda i: (N_LAT // tm + i, 0)),
            pl.BlockSpec((tm, LANES), lambda i: (i, 0)),
            pl.BlockSpec((1, 6, D_MODEL), lambda i: (i // tiles_per_batch, 0, 0)),
            pl.BlockSpec((1, D_MODEL), lambda i: (0, 0)),
        ],
        out_specs=pl.BlockSpec((tm, D_MODEL), lambda i: (i, 0)),
        out_shape=jax.ShapeDtypeStruct((N_LAT, D_MODEL), F32),
        compiler_params=_params("parallel"),
        name="combine_final",
    )(h_lat, y, y, route, mod5, final_w)


def _moe_plan(route):
    e12 = route[:, 0:2].astype(jnp.int32)
    onehot = (e12[:, :, None] == jnp.arange(N_EXPERTS, dtype=jnp.int32)).astype(jnp.int32).sum(axis=1)
    before = jnp.cumsum(onehot, axis=0) - onehot
    counts = jnp.sum(onehot, axis=0)
    nblk = (counts + TM_MOE - 1) // TM_MOE
    blk_end = jnp.cumsum(nblk)
    slot0 = (blk_end - nblk) * TM_MOE
    rank = jnp.take_along_axis(before, e12, axis=1)
    dest = slot0[e12] + rank
    codes = jnp.full((N_SLOTS,), -1, jnp.int32).at[dest.reshape(-1)].set(jnp.arange(2 * N_LAT, dtype=jnp.int32))
    n_used = blk_end[-1]
    blocks = jnp.minimum(jnp.arange(N_MOE_BLOCKS, dtype=jnp.int32), n_used - 1)
    block_e = jnp.minimum(jnp.sum((blocks[:, None] >= blk_end[None, :]).astype(jnp.int32), axis=1), N_EXPERTS - 1)
    n_valid = jnp.sum((codes >= 0).astype(jnp.int32).reshape(N_MOE_BLOCKS, TM_MOE), axis=1)
    return block_e, n_used.reshape(1).astype(jnp.int32), n_valid, codes


def kernel(x, c, ctx, c_ctx, mod_w, mod_b, norm1_w, w_in, na_rpb, hg_lb, hg_norm_w, ml_gate_b, ml_norm_w,
           w_branch, w_out, norm2_w, ffn_w_up, ffn_w_down, moe_router, moe_w_up, moe_w_down, final_norm_w):
    h_all = jnp.concatenate([x.reshape(N_LAT, D_MODEL), ctx.reshape(N_CTX, D_MODEL)], axis=0)
    c8 = jnp.concatenate([c, c_ctx[None, :], jnp.zeros((3, D_MODEL), F32)], axis=0)
    mods = _modulation(c8, mod_w, mod_b).reshape(DEPTH, 8, 6, D_MODEL)

    lb_p = jax.nn.softmax(hg_lb.astype(F32), axis=0)
    hg_lower = jnp.cumsum(lb_p, axis=0) - lb_p[0]
    tri = _tri_consts()
    hg_tri, hg_masks = _hgrn_consts()
    cos, sin = _rope_tables()

    out = None
    for layer in range(DEPTH):
        last = layer == DEPTH - 1
        mod5 = mods[layer, :5]
        wl = w_in[layer]
        w_main = jnp.concatenate([wl[:, :MAIN_W], wl[:, MAIN_W + ML_GATE_COLS:]], axis=1).astype(BF16)
        w_gates = jnp.pad(wl[:, MAIN_W:MAIN_W + ML_GATE_COLS], ((0, 0), (0, LANES - ML_GATE_COLS))).astype(BF16)
        proj, gates = _inproj(h_all, norm1_w[layer][None, :], mod5, w_main, w_gates)

        na = _na_latent(proj, _na_bias_tables(na_rpb[layer]))
        na_ctx = na if last else _ctx_attention(proj)
        hgf, hgb = _hgrn2(proj, hg_lower[layer], hg_tri, hg_masks)
        gates_t = gates[:, :ML_GATE_COLS].T.reshape(2, 2, ML_HEADS, N_ALL)
        gbias = jnp.broadcast_to(ml_gate_b[layer][..., None], (2, 2, ML_HEADS, LANES))
        mlf, mlb = _mlstm(proj, gates_t, gbias, cos, sin, tri)

        n_rows = N_LAT if last else N_ALL
        h_all = _merge(n_rows, na, na_ctx, hgf, hgb, mlf, mlb, proj, h_all, mod5,
                       hg_norm_w[layer][None, :], ml_norm_w[layer][None, :],
                       w_branch[layer].astype(BF16), w_out[layer].astype(BF16))
        i = layer // 2
        if layer % 2 == 0:
            h_all = _ffn(h_all, norm2_w[layer][None, :], mod5, ffn_w_up[i].astype(BF16), ffn_w_down[i].astype(BF16))
            if last:
                raise NotImplementedError("final norm after a dense last layer")
        else:
            if not last:
                raise NotImplementedError("MoE on the context stream")
            w_router_pad = jnp.pad(moe_router[i], ((0, 0), (0, LANES - N_EXPERTS)))
            f_lat, route = _router(h_all, norm2_w[layer][None, :], mod5, w_router_pad)
            block_e, n_used, n_valid, codes = _moe_plan(route)
            y = _moe_experts(block_e, n_used, n_valid, codes, f_lat,
                             moe_w_up[i].astype(BF16), moe_w_down[i].astype(BF16))
            out = _combine_final(h_all, y, route, mod5, final_norm_w[None, :])
    return out.reshape(BATCH, SEQ, D_MODEL)
```

```python
import functools

import numpy as np
import jax
import jax.numpy as jnp
from jax import lax
from jax.experimental import pallas as pl
from jax.experimental.pallas import tpu as pltpu

F32 = jnp.float32
BF16 = jnp.bfloat16

D_MODEL = 1024
BATCH = 4
SEQ = 4096
DEPTH = 2
GRID_W = 64
GRID_H = SEQ // GRID_W
CTX_LEN = 256
EPS = 1e-6
NEG_INF = -1e30
F_FLOOR = 1e-30
NA_HEADS = 8
NA_HEAD_DIM = 64
NA_WIN_ROWS = 8
NA_WIN_COLS = 16
HG_HEADS = 4
ML_HEADS = 4
HEAD_DIM = 128
ML_GATE_COLS = 16
ROPE_BASE = 10000.0
BRANCH_WIDTH = 512
FFN_DIM = 2816
N_EXPERTS = 8
EXPERT_DIM = 3584

N_LAT = BATCH * SEQ
N_CTX = BATCH * CTX_LEN
N_ALL = N_LAT + N_CTX

LANES = 128
VMEM_LIMIT = 56 * 1024 * 1024

MAIN_W = 12 * 512
BG_OFF = MAIN_W
PROJ_W = MAIN_W + 3 * D_MODEL

TM_PROJ = 1024
TN_PROJ = PROJ_W // 4
TM_MERGE = 256
TM_FFN = 512
NA_QROWS = 8
NA_KROWS = NA_QROWS + NA_WIN_ROWS
CHUNK = 256
TM_MOE = 512
TH_MOE = EXPERT_DIM // 2
N_MOE_BLOCKS = -(-(2 * N_LAT + N_EXPERTS * (TM_MOE - 1)) // TM_MOE)
N_SLOTS = N_MOE_BLOCKS * TM_MOE


def _params(*sem):
    return pltpu.CompilerParams(dimension_semantics=sem, vmem_limit_bytes=VMEM_LIMIT)


def _sigmoid(x):
    return 0.5 * jnp.tanh(0.5 * x) + 0.5


def _silu(x):
    return x * _sigmoid(x)


def _dot(a, b):
    return jnp.dot(a, b, preferred_element_type=F32)


def _dot_nt(a, b):
    return lax.dot_general(a, b, (((1,), (1,)), ((), ())), preferred_element_type=F32)


def _split3(x):
    hi = x.astype(BF16)
    r = x - hi.astype(F32)
    mid = r.astype(BF16)
    lo = (r - mid.astype(F32)).astype(BF16)
    return hi, mid, lo


def _sel_dot(sel, x):
    hi, mid, lo = _split3(x)
    return _dot(sel, lo) + _dot(sel, mid) + _dot(sel, hi)


def _sel_dot_nt(sel, x):
    hi, mid, lo = _split3(x)
    return _dot_nt(sel, lo) + _dot_nt(sel, mid) + _dot_nt(sel, hi)


def _dot_sel(x, sel):
    hi, mid, lo = _split3(x)
    return _dot(lo, sel) + _dot(mid, sel) + _dot(hi, sel)


def _norm_mod(x, nw, shift, scale):
    y = x * lax.rsqrt(jnp.mean(x * x, axis=-1, keepdims=True) + EPS) * nw
    return y * (1.0 + scale) + shift


def _mod_kernel(c_ref, w_ref, b_ref, o_ref):
    s = _silu(c_ref[...])
    o_ref[0] = jnp.dot(s, w_ref[0], preferred_element_type=F32, precision=lax.Precision.HIGHEST) + b_ref[0]


def _modulation(c8, mod_w, mod_b):
    tn = 1536
    return pl.pallas_call(
        _mod_kernel,
        grid=(DEPTH, 6 * D_MODEL // tn),
        in_specs=[
            pl.BlockSpec((8, D_MODEL), lambda l, j: (0, 0)),
            pl.BlockSpec((1, D_MODEL, tn), lambda l, j: (l, 0, j)),
            pl.BlockSpec((1, 1, tn), lambda l, j: (l, 0, j)),
        ],
        out_specs=pl.BlockSpec((1, 8, tn), lambda l, j: (l, 0, j)),
        out_shape=jax.ShapeDtypeStruct((DEPTH, 8, 6 * D_MODEL), F32),
        compiler_params=_params("parallel", "parallel"),
        name="modulation",
    )(c8, mod_w, mod_b.reshape(DEPTH, 1, 6 * D_MODEL))


def _inproj_kernel(h_ref, nw_ref, mod_ref, w_ref, wg_ref, o_ref, g_ref, a_scr):
    @pl.when(pl.program_id(1) == 0)
    def _():
        a = _norm_mod(h_ref[...], nw_ref[...], mod_ref[0, 0:1, :], mod_ref[0, 1:2, :])
        a_scr[...] = a.astype(BF16)
        g_ref[...] = _dot(a_scr[...], wg_ref[...])

    o_ref[...] = _dot(a_scr[...], w_ref[...])


def _inproj(h_all, nw, mod5, w_main, w_gates):
    tiles_per_batch = SEQ // TM_PROJ
    return pl.pallas_call(
        _inproj_kernel,
        grid=(N_ALL // TM_PROJ, PROJ_W // TN_PROJ),
        in_specs=[
            pl.BlockSpec((TM_PROJ, D_MODEL), lambda i, j: (i, 0)),
            pl.BlockSpec((1, D_MODEL), lambda i, j: (0, 0)),
            pl.BlockSpec((1, 6, D_MODEL), lambda i, j: (i // tiles_per_batch, 0, 0)),
            pl.BlockSpec((D_MODEL, TN_PROJ), lambda i, j: (0, j)),
            pl.BlockSpec((D_MODEL, LANES), lambda i, j: (0, 0)),
        ],
        out_specs=[pl.BlockSpec((TM_PROJ, TN_PROJ), lambda i, j: (i, j)),
                   pl.BlockSpec((TM_PROJ, LANES), lambda i, j: (i, 0))],
        out_shape=[jax.ShapeDtypeStruct((N_ALL, PROJ_W), F32), jax.ShapeDtypeStruct((N_ALL, LANES), F32)],
        scratch_shapes=[pltpu.VMEM((TM_PROJ, D_MODEL), BF16)],
        compiler_params=_params("parallel", "arbitrary"),
        name="inproj",
    )(h_all, nw, mod5, w_main, w_gates)


N_DR = 2 * NA_WIN_ROWS - 1


def _na_bias_tables(rpb):
    qc = np.arange(GRID_W)[:, None]
    kc = np.arange(GRID_W)[None, :]
    dc = np.clip(kc - qc + NA_WIN_COLS - 1, 0, 2 * NA_WIN_COLS - 2)
    ws = np.clip(qc - NA_WIN_COLS // 2, 0, GRID_W - NA_WIN_COLS)
    col_ok = (kc >= ws) & (kc < ws + NA_WIN_COLS)
    onehot = ((dc[None] == np.arange(2 * NA_WIN_COLS - 1)[:, None, None]) & col_ok[None]).astype(np.float32)
    t = jnp.einsum('hrd,dqk->hrqk', rpb.astype(F32), jnp.asarray(onehot), precision=lax.Precision.HIGHEST)
    t = t + jnp.asarray(np.where(col_ok, 0.0, NEG_INF).astype(np.float32))
    tp = jnp.pad(t, ((0, 0), (1, 2), (0, 0), (0, 0)))
    return jnp.concatenate([tp[:, :N_DR + 2], tp[:, 1:]], axis=-1)


def _na_kernel(q_ref, k_ref, v_ref, kc_ref, vc_ref, tab_ref, o_ref):
    blk = pl.program_id(2)
    q0 = blk * NA_QROWS
    k0 = jnp.clip(q0 - NA_WIN_ROWS // 2, 0, GRID_H - NA_KROWS)
    start = pl.multiple_of(k0 * GRID_W, GRID_W)
    nk = NA_KROWS * GRID_W
    lane = lax.broadcasted_iota(jnp.int32, (GRID_W, LANES), 1)

    tab_idx, penalty = [], []
    for qr in range(NA_QROWS):
        r = q0 + qr
        r0 = jnp.clip(r - NA_WIN_ROWS // 2, 0, GRID_H - NA_WIN_ROWS)
        idx_row, pen_row = [], []
        for j in range(NA_KROWS // 2):
            kra = k0 + 2 * j
            pa = jnp.where(jnp.logical_and(kra >= r0, kra < r0 + NA_WIN_ROWS), 0.0, NEG_INF)
            pb = jnp.where(jnp.logical_and(kra + 1 >= r0, kra + 1 < r0 + NA_WIN_ROWS), 0.0, NEG_INF)
            idx_row.append(jnp.clip(kra - r + NA_WIN_ROWS, 0, N_DR + 1))
            pen_row.append(jnp.where(lane < GRID_W, pa, pb))
        tab_idx.append(idx_row)
        penalty.append(pen_row)

    outs = []
    for hh in range(2):
        sl = slice(hh * NA_HEAD_DIM, (hh + 1) * NA_HEAD_DIM)
        q = (q_ref[:, sl] * (NA_HEAD_DIM ** -0.5)).astype(BF16)
        k = k_ref[pl.ds(start, nk), sl].astype(BF16)
        v = v_ref[pl.ds(start, nk), sl].astype(BF16)
        kc = kc_ref[:, sl].astype(BF16)
        vc = vc_ref[:, sl].astype(BF16)
        s_raw = _dot_nt(q, k)
        rows = []
        for qr in range(NA_QROWS):
            cols = []
            for j in range(NA_KROWS // 2):
                s_blk = s_raw[qr * GRID_W:(qr + 1) * GRID_W, j * LANES:(j + 1) * LANES]
                cols.append(s_blk + (tab_ref[hh, tab_idx[qr][j]] + penalty[qr][j]))
            rows.append(jnp.concatenate(cols, axis=1))
        s_loc = jnp.concatenate(rows, axis=0)
        s_ctx = _dot_nt(q, kc)
        m = jnp.maximum(jnp.max(s_loc, axis=-1, keepdims=True), jnp.max(s_ctx, axis=-1, keepdims=True))
        p_loc = jnp.exp(s_loc - m)
        p_ctx = jnp.exp(s_ctx - m)
        den = jnp.sum(p_loc, axis=-1, keepdims=True) + jnp.sum(p_ctx, axis=-1, keepdims=True)
        o = _dot(p_loc.astype(BF16), v) + _dot(p_ctx.astype(BF16), vc)
        outs.append(o / den)
    o_ref[...] = jnp.concatenate(outs, axis=-1)


def _na_latent(proj, bias):
    nq = NA_QROWS * GRID_W
    nblk = GRID_H // NA_QROWS
    ctx_blk0 = N_LAT // CTX_LEN
    return pl.pallas_call(
        _na_kernel,
        grid=(BATCH, NA_HEADS // 2, nblk),
        in_specs=[
            pl.BlockSpec((nq, LANES), lambda b, hp, blk: (b * nblk + blk, hp)),
            pl.BlockSpec((SEQ, LANES), lambda b, hp, blk: (b, 4 + hp)),
            pl.BlockSpec((SEQ, LANES), lambda b, hp, blk: (b, 8 + hp)),
            pl.BlockSpec((CTX_LEN, LANES), lambda b, hp, blk: (ctx_blk0 + b, 4 + hp)),
            pl.BlockSpec((CTX_LEN, LANES), lambda b, hp, blk: (ctx_blk0 + b, 8 + hp)),
            pl.BlockSpec((2, N_DR + 2, GRID_W, LANES), lambda b, hp, blk: (hp, 0, 0, 0)),
        ],
        out_specs=pl.BlockSpec((nq, LANES), lambda b, hp, blk: (b * nblk + blk, hp)),
        out_shape=jax.ShapeDtypeStruct((N_LAT, NA_HEADS * NA_HEAD_DIM), F32),
        compiler_params=_params("parallel", "parallel", "arbitrary"),
        name="na_latent",
    )(proj, proj, proj, proj, proj, bias)


def _ctx_attn_kernel(q_ref, k_ref, v_ref, o_ref):
    outs = []
    for hh in range(2):
        sl = slice(hh * NA_HEAD_DIM, (hh + 1) * NA_HEAD_DIM)
        q = (q_ref[:, sl] * (NA_HEAD_DIM ** -0.5)).astype(BF16)
        s = _dot_nt(q, k_ref[:, sl].astype(BF16))
        p = jnp.exp(s - jnp.max(s, axis=-1, keepdims=True))
        o = _dot(p.astype(BF16), v_ref[:, sl].astype(BF16))
        outs.append(o / jnp.sum(p, axis=-1, keepdims=True))
    o_ref[...] = jnp.concatenate(outs, axis=-1)


def _ctx_attention(proj):
    ctx_blk0 = N_LAT // CTX_LEN
    return pl.pallas_call(
        _ctx_attn_kernel,
        grid=(BATCH, NA_HEADS // 2),
        in_specs=[
            pl.BlockSpec((CTX_LEN, LANES), lambda b, hp: (ctx_blk0 + b, hp)),
            pl.BlockSpec((CTX_LEN, LANES), lambda b, hp: (ctx_blk0 + b, 4 + hp)),
            pl.BlockSpec((CTX_LEN, LANES), lambda b, hp: (ctx_blk0 + b, 8 + hp)),
        ],
        out_specs=pl.BlockSpec((CTX_LEN, LANES), lambda b, hp: (b, hp)),
        out_shape=jax.ShapeDtypeStruct((N_CTX, NA_HEADS * NA_HEAD_DIM), F32),
        compiler_params=_params("parallel", "parallel"),
        name="ctx_attention",
    )(proj, proj, proj)


N_CTX_CHUNKS = CTX_LEN // CHUNK
N_LAT_CHUNKS = SEQ // CHUNK
N_STEPS = N_CTX_CHUNKS + N_LAT_CHUNKS
SUB = CHUNK // 2
HG_LEVELS = SUB.bit_length() - 1


def _chunk_block(b, s, rev):
    c_ctx = (N_CTX_CHUNKS - 1 - s) if rev else s
    c_lat = (N_LAT_CHUNKS - 1 - (s - N_CTX_CHUNKS)) if rev else (s - N_CTX_CHUNKS)
    ctx_blk = N_LAT // CHUNK + b * N_CTX_CHUNKS + c_ctx
    lat_blk = b * N_LAT_CHUNKS + c_lat
    return jnp.where(s < N_CTX_CHUNKS, ctx_blk, lat_blk)


def _tri_consts():
    i = np.arange(CHUNK)
    low = (i[None, :] <= i[:, None]).astype(np.float32)
    eye = np.eye(CHUNK, dtype=np.float32)
    return jnp.asarray(np.stack([low, low.T, eye]), BF16)


def _hgrn_consts():
    t = np.arange(SUB)[:, None]
    s = np.arange(SUB)[None, :]
    low = (s <= t).astype(np.float32)
    masks = np.zeros((2, HG_LEVELS + 1, SUB, SUB), np.float32)
    for l in range(HG_LEVELS):
        pair = ((t ^ s) >> l) == 1
        masks[0, l] = pair & (t > s)
        masks[1, l] = pair & (t < s)
    masks[:, HG_LEVELS] = (t == s)
    return jnp.asarray(np.stack([low, low.T]), BF16), jnp.asarray(masks)


def _hgrn_chain(q_ref, z_ref, v_ref, o_ref, sl, lb, st_ref, tri, masks_ref, rev):
    L, S = CHUNK, SUB
    z = z_ref[:, sl]
    v = v_ref[:, sl]
    e = jnp.exp(-jnp.abs(z))
    r = 1.0 / (1.0 + e)
    pos = z >= 0
    sig = jnp.where(pos, r, e * r)
    nsig = jnp.where(pos, e * r, r)
    k = (1.0 - lb) * nsig
    log2f = jnp.log2(jnp.maximum(lb + (1.0 - lb) * sig, F_FLOOR))
    q = _silu(q_ref[:, sl])
    yield

    g0 = _sel_dot(tri, log2f[:S])
    g1 = _sel_dot(tri, log2f[S:])
    if rev:
        g0 = g0 + g1[0:1, :]
    else:
        g1 = g1 + g0[S - 1:S, :]
    g = jnp.concatenate([g0, g1], axis=0)
    row = lax.broadcasted_iota(jnp.int32, (L, HEAD_DIM), 0)

    st = st_ref[...]
    o_inter = _dot_nt((q * jnp.exp2(g)).astype(BF16), st.astype(BF16))
    yield

    a = [jnp.zeros((S, S), F32), jnp.zeros((S, S), F32)]
    cross = None
    q16 = q.astype(BF16)
    k16 = k.astype(BF16)
    bnd = g
    for l in range(HG_LEVELS + 1):
        blk = 1 << l
        q_side = ((row & blk) == 0) if rev else ((row & blk) != 0)
        prev_end = pltpu.roll(bnd, (L - blk) if rev else blk, 0)
        w16 = jnp.exp2(jnp.where(q_side, g - prev_end, bnd - g)).astype(BF16)
        qb = q16 * w16
        kb = k16 * w16
        if blk < S:
            for c in range(2):
                a[c] = a[c] + masks_ref[l] * _dot_nt(qb[c * S:(c + 1) * S], kb[c * S:(c + 1) * S])
        elif rev:
            cross = _dot_nt(qb[:S], kb[S:])
        else:
            cross = _dot_nt(qb[S:], kb[:S])
        nxt = pltpu.roll(bnd, blk if rev else (L - blk), 0)
        bnd = jnp.where(q_side, bnd, nxt)
        yield
    v16 = v.astype(BF16)
    for c in range(2):
        a[c] = a[c] + masks_ref[HG_LEVELS] * _dot_nt(q16[c * S:(c + 1) * S], k16[c * S:(c + 1) * S])
    if rev:
        o0 = _dot(jnp.concatenate([a[0], cross], axis=1).astype(BF16), v16)
        o1 = _dot(a[1].astype(BF16), v16[S:])
    else:
        o0 = _dot(a[0].astype(BF16), v16[:S])
        o1 = _dot(jnp.concatenate([cross, a[1]], axis=1).astype(BF16), v16)
    o_ref[:, sl] = o_inter + jnp.concatenate([o0, o1], axis=0)
    yield

    kd = k * jnp.exp2(bnd - g)
    st_ref[...] = jnp.exp2(bnd[0:1, :]) * st + lax.dot_general(
        v16, kd.astype(BF16), (((0,), (0,)), ((), ())), preferred_element_type=F32)


def _round_robin(chains):
    alive = list(chains)
    while alive:
        still = []
        for c in alive:
            try:
                next(c)
                still.append(c)
            except StopIteration:
                pass
        alive = still


def _hgrn_kernel(qf_ref, zf_ref, vf_ref, qb_ref, zb_ref, vb_ref, lb_ref, tri_ref, masks_ref,
                 of_ref, ob_ref, stf_ref, stb_ref):
    @pl.when(pl.program_id(1) == 0)
    def _():
        stf_ref[...] = jnp.zeros_like(stf_ref)
        stb_ref[...] = jnp.zeros_like(stb_ref)

    chains = []
    for h in range(HG_HEADS):
        sl = slice(h * HEAD_DIM, (h + 1) * HEAD_DIM)
        chains.append(_hgrn_chain(qf_ref, zf_ref, vf_ref, of_ref, sl, lb_ref[0:1, sl], stf_ref.at[h],
                                  tri_ref[0], masks_ref.at[0], False))
        chains.append(_hgrn_chain(qb_ref, zb_ref, vb_ref, ob_ref, sl, lb_ref[1:2, sl], stb_ref.at[h],
                                  tri_ref[1], masks_ref.at[1], True))
    _round_robin(chains)


def _hgrn2(proj, lower, tri, masks):
    width = HG_HEADS * HEAD_DIM

    def spec(col, rev):
        return pl.BlockSpec((CHUNK, width), lambda b, s: (_chunk_block(b, s, rev), col))

    out_spec = lambda rev: pl.BlockSpec((CHUNK, width), lambda b, s: (_chunk_block(b, s, rev), 0))
    shape = jax.ShapeDtypeStruct((N_ALL, width), F32)
    return pl.pallas_call(
        _hgrn_kernel,
        grid=(BATCH, N_STEPS),
        in_specs=[
            spec(3, False), spec(4, False), spec(6, False),
            spec(3, True), spec(5, True), spec(6, True),
            pl.BlockSpec((2, width), lambda b, s: (0, 0)),
            pl.BlockSpec((2, SUB, SUB), lambda b, s: (0, 0, 0)),
            pl.BlockSpec((2, HG_LEVELS + 1, SUB, SUB), lambda b, s: (0, 0, 0, 0)),
        ],
        out_specs=[out_spec(False), out_spec(True)],
        out_shape=[shape, shape],
        scratch_shapes=[pltpu.VMEM((HG_HEADS, HEAD_DIM, HEAD_DIM), F32), pltpu.VMEM((HG_HEADS, HEAD_DIM, HEAD_DIM), F32)],
        compiler_params=_params("parallel", "arbitrary"),
        name="hgrn2",
    )(proj, proj, proj, proj, proj, proj, lower, tri, masks)


def _rope_tables():
    n_freq = HEAD_DIM // 4
    inv_freq = ROPE_BASE ** (-jnp.arange(n_freq, dtype=F32) / n_freq)
    t = jnp.arange(SEQ, dtype=jnp.int32)
    ang_r = (t // GRID_W).astype(F32)[:, None] * inv_freq
    ang_c = (t % GRID_W).astype(F32)[:, None] * inv_freq
    cos = jnp.concatenate([jnp.cos(ang_r), jnp.cos(ang_r), jnp.cos(ang_c), jnp.cos(ang_c)], axis=-1)
    sin = jnp.concatenate([-jnp.sin(ang_r), jnp.sin(ang_r), -jnp.sin(ang_c), jnp.sin(ang_c)], axis=-1)
    cos = jnp.concatenate([cos, jnp.ones((CHUNK, HEAD_DIM), F32)], axis=0)
    sin = jnp.concatenate([sin, jnp.zeros((CHUNK, HEAD_DIM), F32)], axis=0)
    return cos, sin


def _rope_swap():
    l = np.arange(HEAD_DIM)
    return jnp.asarray((l[:, None] == (l[None, :] ^ 32)).astype(np.float32), BF16)


def _rope(x, cos, sin, swap):
    hi = x.astype(BF16)
    mid = (x - hi.astype(F32)).astype(BF16)
    partner = _dot(hi, swap) + _dot(mid, swap)
    return x * cos + partner * sin


def _mlstm_gates(g_ref, gbias, tri_ref, rev):
    L = CHUNK
    log_i = g_ref[0, 0] + gbias[0][:, 0:1]
    xf = g_ref[0, 1] + gbias[1][:, 0:1]
    log_f = jnp.minimum(xf, 0.0) - jnp.log(1.0 + jnp.exp(-jnp.abs(xf)))
    r8 = jnp.concatenate([log_f, log_i], axis=0)
    low, up, eye = tri_ref[0], tri_ref[1], tri_ref[2]
    b_row = _dot_sel(r8, low if rev else up)[0:ML_HEADS, :]
    b_col = _sel_dot_nt(up if rev else low, r8)[:, 0:ML_HEADS]
    i_col = _sel_dot_nt(eye, r8)[:, ML_HEADS:2 * ML_HEADS]
    return log_i, b_row, b_col, i_col


def _mlstm_kernel(qf_ref, kf_ref, vf_ref, gf_ref, cf_ref, sf_ref,
                  qb_ref, kb_ref, vb_ref, gb_ref, cb_ref, sb_ref,
                  gbias_ref, tri_ref, swap_ref, of_ref, ob_ref, cnf_ref, cnb_ref, mf_ref, mb_ref):
    @pl.when(pl.program_id(1) == 0)
    def _():
        cnf_ref[...] = jnp.zeros_like(cnf_ref)
        cnb_ref[...] = jnp.zeros_like(cnb_ref)
        mf_ref[...] = jnp.zeros_like(mf_ref)
        mb_ref[...] = jnp.zeros_like(mb_ref)

    L = CHUNK
    ti = lax.broadcasted_iota(jnp.int32, (L, L), 0)
    si = lax.broadcasted_iota(jnp.int32, (L, L), 1)
    ones = jnp.ones((L, HEAD_DIM), BF16)
    swap = swap_ref[...]
    dirs = ((qf_ref, kf_ref, vf_ref, gf_ref, cf_ref, sf_ref, of_ref, cnf_ref, mf_ref, False),
            (qb_ref, kb_ref, vb_ref, gb_ref, cb_ref, sb_ref, ob_ref, cnb_ref, mb_ref, True))

    chains = []
    for d, (q_ref, k_ref, v_ref, g_ref, c_ref, s_ref, o_ref, cn_ref, m_ref, rev) in enumerate(dirs):
        log_i, b_row, b_col, i_col = _mlstm_gates(g_ref, gbias_ref[d], tri_ref, rev)
        for h in range(ML_HEADS):
            bc = jnp.broadcast_to(b_col[:, h:h + 1], (L, HEAD_DIM))
            ic = jnp.broadcast_to(i_col[:, h:h + 1], (L, HEAD_DIM))
            chains.append(dict(
                sl=slice(h * HEAD_DIM, (h + 1) * HEAD_DIM), rev=rev, q_ref=q_ref, k_ref=k_ref, v_ref=v_ref,
                c_ref=c_ref, s_ref=s_ref, o_ref=o_ref, cn_ref=cn_ref.at[h], m_ref=m_ref.at[h],
                log_i=log_i[h:h + 1, :], b_row=b_row[h:h + 1, :], b_col=bc, i_col=ic))

    for c in chains:
        cos, sin = c['c_ref'][...], c['s_ref'][...]
        c['qc'] = _rope(c['q_ref'][:, c['sl']], cos, sin, swap).astype(BF16)
        kc = _rope(c['k_ref'][:, c['sl']] * (HEAD_DIM ** -0.5), cos, sin, swap)
        c['kc'] = kc
        c['s'] = _dot_nt(c['qc'], kc.astype(BF16))
    for c in chains:
        b_row = c['b_row']
        c['b_end'] = b_row[:, 0:1] if c['rev'] else b_row[:, L - 1:L]
        tri = (si >= ti) if c['rev'] else (si <= ti)
        c['m_prev'] = c['m_ref'][0:1, 0:1]
        bc2 = jnp.concatenate([c['b_col'], c['b_col']], axis=-1)
        dmat = jnp.where(tri, bc2 + (c['log_i'] - b_row), NEG_INF)
        inter = c['b_col'] + c['m_prev']
        m_t = jnp.maximum(inter, jnp.max(dmat, axis=-1, keepdims=True))
        c['m_t'] = m_t
        c['w_inter'] = jnp.exp(inter - m_t)
        c['p'] = (jnp.exp(dmat - jnp.concatenate([m_t, m_t], axis=-1)) * c['s']).astype(BF16)
    for c in chains:
        c['v_ext'] = jnp.concatenate([c['v_ref'][:, c['sl']].astype(BF16), ones], axis=-1)
        c['cn'] = c['cn_ref'][...]
        w2 = jnp.concatenate([c['w_inter'], c['w_inter']], axis=-1)
        acc = _dot(c['p'], c['v_ext']) + w2 * _dot(c['qc'], c['cn'].astype(BF16))
        den = acc[:, HEAD_DIM:]
        c['o_ref'][:, c['sl']] = acc[:, :HEAD_DIM] / jnp.maximum(jnp.abs(den), jnp.exp(-c['m_t']))
    for c in chains:
        e_row = c['b_end'] + (c['log_i'] - c['b_row'])
        m_new = jnp.maximum(c['b_end'] + c['m_prev'], jnp.max(e_row, axis=-1, keepdims=True))
        w_old = jnp.exp(c['b_end'] + c['m_prev'] - m_new)
        w_s = jnp.exp(c['b_end'] - c['b_col'] + c['i_col'] - m_new)
        c['cn_ref'][...] = w_old * c['cn'] + lax.dot_general(
            (w_s * c['kc']).astype(BF16), c['v_ext'], (((0,), (0,)), ((), ())), preferred_element_type=F32)
        c['m_ref'][...] = jnp.broadcast_to(m_new, c['m_ref'].shape)


def _mlstm(proj, gates_t, gbias, cos, sin, tri):
    width = ML_HEADS * HEAD_DIM

    def spec(col, rev):
        return pl.BlockSpec((CHUNK, width), lambda b, s: (_chunk_block(b, s, rev), col))

    def gate_spec(rev):
        d = 1 if rev else 0
        return pl.BlockSpec((1, 2, ML_HEADS, CHUNK), lambda b, s: (d, 0, 0, _chunk_block(b, s, rev)))

    def rope_spec(rev):
        def idx(b, s):
            lat = _chunk_block(b, s, rev) - b * N_LAT_CHUNKS
            return (jnp.where(s < N_CTX_CHUNKS, N_LAT_CHUNKS, lat), 0)
        return pl.BlockSpec((CHUNK, LANES), idx)

    out_spec = lambda rev: pl.BlockSpec((CHUNK, width), lambda b, s: (_chunk_block(b, s, rev), 0))
    shape = jax.ShapeDtypeStruct((N_ALL, width), F32)
    per_dir = lambda rev: [spec(8, rev), spec(9, rev), spec(10, rev), gate_spec(rev), rope_spec(rev), rope_spec(rev)]
    args_dir = [proj, proj, proj, gates_t, cos, sin]
    return pl.pallas_call(
        _mlstm_kernel,
        grid=(BATCH, N_STEPS),
        in_specs=per_dir(False) + per_dir(True) + [
            pl.BlockSpec((2, 2, ML_HEADS, LANES), lambda b, s: (0, 0, 0, 0)),
            pl.BlockSpec((3, CHUNK, CHUNK), lambda b, s: (0, 0, 0)),
            pl.BlockSpec((HEAD_DIM, HEAD_DIM), lambda b, s: (0, 0)),
        ],
        out_specs=[out_spec(False), out_spec(True)],
        out_shape=[shape, shape],
        scratch_shapes=[pltpu.VMEM((ML_HEADS, HEAD_DIM, 2 * HEAD_DIM), F32),
                        pltpu.VMEM((ML_HEADS, HEAD_DIM, 2 * HEAD_DIM), F32),
                        pltpu.VMEM((ML_HEADS, 8, LANES), F32), pltpu.VMEM((ML_HEADS, 8, LANES), F32)],
        compiler_params=_params("parallel", "arbitrary"),
        name="mlstm",
    )(*args_dir, *args_dir, gbias, tri, _rope_swap())


def _head_rms(x, w):
    parts = []
    for hh in range(x.shape[-1] // HEAD_DIM):
        xs = x[:, hh * HEAD_DIM:(hh + 1) * HEAD_DIM]
        parts.append(xs * lax.rsqrt(jnp.mean(xs * xs, axis=-1, keepdims=True) + EPS))
    return jnp.concatenate(parts, axis=-1) * w


def _merge_kernel(na_ref, nac_ref, hgf_ref, hgb_ref, mlf_ref, mlb_ref, hgg_ref, mlo_ref, bg0_ref, bg1_ref, bg2_ref,
                  h_ref, mod_ref, hgw_ref, mlw_ref, wb_ref, wo_ref, o_ref):
    hg = _head_rms(hgf_ref[...] + hgb_ref[...], hgw_ref[...]) * _silu(hgg_ref[...])
    ml = _sigmoid(mlo_ref[...]) * _head_rms(mlf_ref[...] + mlb_ref[...], mlw_ref[...])
    is_ctx = pl.program_id(0) >= N_LAT // TM_MERGE
    na = jnp.where(is_ctx, nac_ref[...], na_ref[...])
    y = _sigmoid(bg0_ref[...]) * _dot(na.astype(BF16), wb_ref[0])
    y = y + _sigmoid(bg1_ref[...]) * _dot(hg.astype(BF16), wb_ref[1])
    y = y + _sigmoid(bg2_ref[...]) * _dot(ml.astype(BF16), wb_ref[2])
    o_ref[...] = h_ref[...] + mod_ref[0, 2:3, :] * _dot(y.astype(BF16), wo_ref[...])


def _merge(n_rows, na, na_ctx, hgf, hgb, mlf, mlb, proj, h_all, mod5, hg_w, ml_w, w_branch, w_out):
    tm = TM_MERGE
    tiles_per_batch = SEQ // tm
    n_lat_tiles = N_LAT // tm
    row = lambda w, c: pl.BlockSpec((tm, w), lambda i: (i, c))
    const = lambda shape: pl.BlockSpec(shape, lambda i: (0,) * len(shape))
    bg0 = BG_OFF // D_MODEL
    return pl.pallas_call(
        _merge_kernel,
        grid=(n_rows // tm,),
        in_specs=[
            pl.BlockSpec((tm, 512), lambda i: (jnp.minimum(i, n_lat_tiles - 1), 0)),
            pl.BlockSpec((tm, 512), lambda i: (jnp.maximum(i - n_lat_tiles, 0), 0)),
            row(512, 0), row(512, 0), row(512, 0), row(512, 0),
            row(512, 7), row(512, 11),
            row(D_MODEL, bg0), row(D_MODEL, bg0 + 1), row(D_MODEL, bg0 + 2),
            row(D_MODEL, 0),
            pl.BlockSpec((1, 6, D_MODEL), lambda i: (jnp.minimum(i // tiles_per_batch, BATCH), 0, 0)),
            const((1, 512)), const((1, 512)),
            const((3, BRANCH_WIDTH, D_MODEL)), const((D_MODEL, D_MODEL)),
        ],
        out_specs=row(D_MODEL, 0),
        out_shape=jax.ShapeDtypeStruct((n_rows, D_MODEL), F32),
        compiler_params=_params("parallel"),
        name="merge",
    )(na, na_ctx, hgf, hgb, mlf, mlb, proj, proj, proj, proj, proj, h_all, mod5, hg_w, ml_w, w_branch, w_out)


def _ffn_kernel(h_ref, nw_ref, mod_ref, wa_ref, wu_ref, wd_ref, o_ref):
    h = h_ref[...]
    f = _norm_mod(h, nw_ref[...], mod_ref[0, 3:4, :], mod_ref[0, 4:5, :]).astype(BF16)
    g = _silu(_dot(f, wa_ref[...])) * _dot(f, wu_ref[...])
    o_ref[...] = h + mod_ref[0, 5:6, :] * _dot(g.astype(BF16), wd_ref[...])


def _ffn(h_all, nw, mod5, w_up, w_down):
    n_rows = h_all.shape[0]
    tiles_per_batch = SEQ // TM_FFN
    resident = pl.Buffered(1)
    return pl.pallas_call(
        _ffn_kernel,
        grid=(n_rows // TM_FFN,),
        in_specs=[
            pl.BlockSpec((TM_FFN, D_MODEL), lambda i: (i, 0)),
            pl.BlockSpec((1, D_MODEL), lambda i: (0, 0)),
            pl.BlockSpec((1, 6, D_MODEL), lambda i: (i // tiles_per_batch, 0, 0)),
            pl.BlockSpec((D_MODEL, FFN_DIM), lambda i: (0, 0), pipeline_mode=resident),
            pl.BlockSpec((D_MODEL, FFN_DIM), lambda i: (0, 1), pipeline_mode=resident),
            pl.BlockSpec((FFN_DIM, D_MODEL), lambda i: (0, 0), pipeline_mode=resident),
        ],
        out_specs=pl.BlockSpec((TM_FFN, D_MODEL), lambda i: (i, 0)),
        out_shape=jax.ShapeDtypeStruct((n_rows, D_MODEL), F32),
        compiler_params=_params("parallel"),
        name="ffn",
    )(h_all, nw, mod5, w_up, w_up, w_down)


def _router_kernel(h_ref, nw_ref, mod_ref, wr_ref, f_ref, r_ref):
    f = _norm_mod(h_ref[...], nw_ref[...], mod_ref[0, 3:4, :], mod_ref[0, 4:5, :])
    f_ref[...] = f
    logits = jnp.dot(f, wr_ref[...], preferred_element_type=F32, precision=lax.Precision.HIGHEST)
    lane = lax.broadcasted_iota(jnp.int32, logits.shape, 1)
    logits = jnp.where(lane < N_EXPERTS, logits, -jnp.inf)
    m1 = jnp.max(logits, axis=-1, keepdims=True)
    i1 = jnp.min(jnp.where(logits == m1, lane, LANES), axis=-1, keepdims=True)
    rest = jnp.where(lane == i1, -jnp.inf, logits)
    m2 = jnp.max(rest, axis=-1, keepdims=True)
    i2 = jnp.min(jnp.where(rest == m2, lane, LANES), axis=-1, keepdims=True)
    e2 = jnp.exp(m2 - m1)
    w1 = 1.0 / (1.0 + e2)
    w2 = e2 / (1.0 + e2)
    r_ref[...] = jnp.where(lane == 0, i1.astype(F32),
                           jnp.where(lane == 1, i2.astype(F32),
                                     jnp.where(lane == 2, w1, jnp.where(lane == 3, w2, 0.0))))


def _router(h_lat, nw, mod5, w_router_pad):
    tm = 512
    tiles_per_batch = SEQ // tm
    return pl.pallas_call(
        _router_kernel,
        grid=(N_LAT // tm,),
        in_specs=[
            pl.BlockSpec((tm, D_MODEL), lambda i: (i, 0)),
            pl.BlockSpec((1, D_MODEL), lambda i: (0, 0)),
            pl.BlockSpec((1, 6, D_MODEL), lambda i: (i // tiles_per_batch, 0, 0)),
            pl.BlockSpec((D_MODEL, LANES), lambda i: (0, 0)),
        ],
        out_specs=[pl.BlockSpec((tm, D_MODEL), lambda i: (i, 0)), pl.BlockSpec((tm, LANES), lambda i: (i, 0))],
        out_shape=[jax.ShapeDtypeStruct((N_LAT, D_MODEL), F32), jax.ShapeDtypeStruct((N_LAT, LANES), F32)],
        compiler_params=_params("parallel"),
        name="router",
    )(h_lat, nw, mod5, w_router_pad)


def _moe_kernel(be_ref, nused_ref, nvalid_ref, code_ref, f_hbm, wa_ref, wu_ref, wd_ref, y_hbm,
                xbuf, x16, acc, ybuf, sem_in, sem_out):
    i = pl.program_id(0)
    j = pl.program_id(1)
    last_j = pl.num_programs(1) - 1
    n_used = nused_ref[0]
    active = i < n_used
    slot = i % 2

    def start_gather(blk, buf):
        def body(r, c):
            tok = jnp.maximum(code_ref[blk * TM_MOE + r], 0) >> 1
            pltpu.make_async_copy(f_hbm.at[pl.ds(tok, 1)], xbuf.at[buf, pl.ds(r, 1)], sem_in.at[buf]).start()
            return c
        lax.fori_loop(0, TM_MOE, body, 0, unroll=8)

    def wait_gather(buf):
        pltpu.make_async_copy(f_hbm.at[pl.ds(0, TM_MOE)], xbuf.at[buf], sem_in.at[buf]).wait()

    def start_scatter(blk):
        def body(r, c):
            code = code_ref[blk * TM_MOE + r]
            dst = (code & 1) * N_LAT + (code >> 1)
            pltpu.make_async_copy(ybuf.at[pl.ds(r, 1)], y_hbm.at[pl.ds(dst, 1)], sem_out).start()
            return c
        lax.fori_loop(0, nvalid_ref[blk], body, 0)

    def wait_scatter(blk):
        n = nvalid_ref[blk]
        p = TM_MOE
        while p >= 8:
            @pl.when((n & p) != 0)
            def _(p=p):
                pltpu.make_async_copy(ybuf.at[pl.ds(0, p)], y_hbm.at[pl.ds(0, p)], sem_out).wait()
            p //= 2

        def one(r, c):
            pltpu.make_async_copy(ybuf.at[pl.ds(0, 1)], y_hbm.at[pl.ds(0, 1)], sem_out).wait()
            return c
        lax.fori_loop(0, n & 7, one, 0)

    def prefetch_rows(first, count):
        for r in range(first, first + count):
            tok = jnp.maximum(code_ref[(i + 1) * TM_MOE + r], 0) >> 1
            pltpu.make_async_copy(f_hbm.at[pl.ds(tok, 1)], xbuf.at[1 - slot, pl.ds(r, 1)], sem_in.at[1 - slot]).start()

    def partial_out():
        x = x16[...]
        g = _silu(_dot(x, wa_ref[0])) * _dot(x, wu_ref[0])
        return _dot(g.astype(BF16), wd_ref[0])

    @pl.when(jnp.logical_and(active, j == 0))
    def _():
        @pl.when(i == 0)
        def _():
            start_gather(0, 0)

        wait_gather(slot)
        x16[...] = xbuf[slot].astype(BF16)

    @pl.when(jnp.logical_and(active, j == 0))
    def _():
        prefetch_rows(0, TM_MOE // 2)
        acc[...] = partial_out()

    @pl.when(jnp.logical_and(active, j == last_j))
    def _():
        @pl.when(i > 0)
        def _():
            wait_scatter(i - 1)

    @pl.when(jnp.logical_and(active, j == last_j))
    def _():
        prefetch_rows(TM_MOE // 2, TM_MOE // 2)
        ybuf[...] = acc[...] + partial_out()

    @pl.when(jnp.logical_and(active, j == last_j))
    def _():
        start_scatter(i)

        @pl.when(i == n_used - 1)
        def _():
            wait_scatter(i)
            wait_gather(1 - slot)


def _moe_experts(block_e, n_used, n_valid, codes, f_lat, w_up, w_down):
    nj = EXPERT_DIM // TH_MOE
    assert nj == 2 and nj * TH_MOE == EXPERT_DIM

    def jj(i, j, nu):
        return jnp.where(i < nu[0], j, nj - 1)

    grid_spec = pltpu.PrefetchScalarGridSpec(
        num_scalar_prefetch=4,
        grid=(N_MOE_BLOCKS, nj),
        in_specs=[
            pl.BlockSpec(memory_space=pl.ANY),
            pl.BlockSpec((1, D_MODEL, TH_MOE), lambda i, j, be, nu, nv, cd: (be[i], 0, jj(i, j, nu))),
            pl.BlockSpec((1, D_MODEL, TH_MOE), lambda i, j, be, nu, nv, cd: (be[i], 0, nj + jj(i, j, nu))),
            pl.BlockSpec((1, TH_MOE, D_MODEL), lambda i, j, be, nu, nv, cd: (be[i], jj(i, j, nu), 0)),
        ],
        out_specs=pl.BlockSpec(memory_space=pl.ANY),
        scratch_shapes=[
            pltpu.VMEM((2, TM_MOE, D_MODEL), F32), pltpu.VMEM((TM_MOE, D_MODEL), BF16),
            pltpu.VMEM((TM_MOE, D_MODEL), F32), pltpu.VMEM((TM_MOE, D_MODEL), F32),
            pltpu.SemaphoreType.DMA((2,)), pltpu.SemaphoreType.DMA(()),
        ],
    )
    return pl.pallas_call(
        _moe_kernel,
        grid_spec=grid_spec,
        out_shape=jax.ShapeDtypeStruct((2 * N_LAT, D_MODEL), F32),
        compiler_params=_params("arbitrary", "arbitrary"),
        name="moe_experts",
    )(block_e, n_used, n_valid, codes, f_lat, w_up, w_up, w_down)


def _combine_kernel(h_ref, y1_ref, y2_ref, r_ref, mod_ref, fw_ref, o_ref):
    r = r_ref[...]
    y = r[:, 2:3] * y1_ref[...] + r[:, 3:4] * y2_ref[...]
    h = h_ref[...] + mod_ref[0, 5:6, :] * y
    o_ref[...] = h * lax.rsqrt(jnp.mean(h * h, axis=-1, keepdims=True) + EPS) * fw_ref[...]


def _combine_final(h_lat, y, route, mod5, final_w):
    tm = 512
    tiles_per_batch = SEQ // tm
    return pl.pallas_call(
        _combine_kernel,
        grid=(N_LAT // tm,),
        in_specs=[
            pl.BlockSpec((tm, D_MODEL), lambda i: (i, 0)),
            pl.BlockSpec((tm, D_MODEL), lambda i: (i, 0)),
            pl.BlockSpec((tm, D_MODEL), lambda i: (N_LAT // tm + i, 0)),
            pl.BlockSpec((tm, LANES), lambda i: (i, 0)),
            pl.BlockSpec((1, 6, D_MODEL), lambda i: (i // tiles_per_batch, 0, 0)),
            pl.BlockSpec((1, D_MODEL), lambda i: (0, 0)),
        ],
        out_specs=pl.BlockSpec((tm, D_MODEL), lambda i: (i, 0)),
        out_shape=jax.ShapeDtypeStruct((N_LAT, D_MODEL), F32),
        compiler_params=_params("parallel"),
        name="combine_final",
    )(h_lat, y, y, route, mod5, final_w)


def _moe_plan(route):
    e12 = route[:, 0:2].astype(jnp.int32)
    onehot = (e12[:, :, None] == jnp.arange(N_EXPERTS, dtype=jnp.int32)).astype(jnp.int32).sum(axis=1)
    before = jnp.cumsum(onehot, axis=0) - onehot
    counts = jnp.sum(onehot, axis=0)
    nblk = (counts + TM_MOE - 1) // TM_MOE
    blk_end = jnp.cumsum(nblk)
    slot0 = (blk_end - nblk) * TM_MOE
    rank = jnp.take_along_axis(before, e12, axis=1)
    dest = slot0[e12] + rank
    codes = jnp.full((N_SLOTS,), -1, jnp.int32).at[dest.reshape(-1)].set(jnp.arange(2 * N_LAT, dtype=jnp.int32))
    n_used = blk_end[-1]
    blocks = jnp.minimum(jnp.arange(N_MOE_BLOCKS, dtype=jnp.int32), n_used - 1)
    block_e = jnp.minimum(jnp.sum((blocks[:, None] >= blk_end[None, :]).astype(jnp.int32), axis=1), N_EXPERTS - 1)
    n_valid = jnp.sum((codes >= 0).astype(jnp.int32).reshape(N_MOE_BLOCKS, TM_MOE), axis=1)
    return block_e, n_used.reshape(1).astype(jnp.int32), n_valid, codes


def kernel(x, c, ctx, c_ctx, mod_w, mod_b, norm1_w, w_in, na_rpb, hg_lb, hg_norm_w, ml_gate_b, ml_norm_w,
           w_branch, w_out, norm2_w, ffn_w_up, ffn_w_down, moe_router, moe_w_up, moe_w_down, final_norm_w):
    h_all = jnp.concatenate([x.reshape(N_LAT, D_MODEL), ctx.reshape(N_CTX, D_MODEL)], axis=0)
    c8 = jnp.concatenate([c, c_ctx[None, :], jnp.zeros((3, D_MODEL), F32)], axis=0)
    mods = _modulation(c8, mod_w, mod_b).reshape(DEPTH, 8, 6, D_MODEL)

    lb_p = jax.nn.softmax(hg_lb.astype(F32), axis=0)
    hg_lower = jnp.cumsum(lb_p, axis=0) - lb_p[0]
    tri = _tri_consts()
    hg_tri, hg_masks = _hgrn_consts()
    cos, sin = _rope_tables()

    out = None
    for layer in range(DEPTH):
        last = layer == DEPTH - 1
        mod5 = mods[layer, :5]
        wl = w_in[layer]
        w_main = jnp.concatenate([wl[:, :MAIN_W], wl[:, MAIN_W + ML_GATE_COLS:]], axis=1).astype(BF16)
        w_gates = jnp.pad(wl[:, MAIN_W:MAIN_W + ML_GATE_COLS], ((0, 0), (0, LANES - ML_GATE_COLS))).astype(BF16)
        proj, gates = _inproj(h_all, norm1_w[layer][None, :], mod5, w_main, w_gates)

        na = _na_latent(proj, _na_bias_tables(na_rpb[layer]))
        na_ctx = na if last else _ctx_attention(proj)
        hgf, hgb = _hgrn2(proj, hg_lower[layer], hg_tri, hg_masks)
        gates_t = gates[:, :ML_GATE_COLS].T.reshape(2, 2, ML_HEADS, N_ALL)
        gbias = jnp.broadcast_to(ml_gate_b[layer][..., None], (2, 2, ML_HEADS, LANES))
        mlf, mlb = _mlstm(proj, gates_t, gbias, cos, sin, tri)

        n_rows = N_LAT if last else N_ALL
        h_all = _merge(n_rows, na, na_ctx, hgf, hgb, mlf, mlb, proj, h_all, mod5,
                       hg_norm_w[layer][None, :], ml_norm_w[layer][None, :],
                       w_branch[layer].astype(BF16), w_out[layer].astype(BF16))
        i = layer // 2
        if layer % 2 == 0:
            h_all = _ffn(h_all, norm2_w[layer][None, :], mod5, ffn_w_up[i].astype(BF16), ffn_w_down[i].astype(BF16))
            if last:
                raise NotImplementedError("final norm after a dense last layer")
        else:
            if not last:
                raise NotImplementedError("MoE on the context stream")
            w_router_pad = jnp.pad(moe_router[i], ((0, 0), (0, LANES - N_EXPERTS)))
            f_lat, route = _router(h_all, norm2_w[layer][None, :], mod5, w_router_pad)
            block_e, n_used, n_valid, codes = _moe_plan(route)
            y = _moe_experts(block_e, n_used, n_valid, codes, f_lat,
                             moe_w_up[i].astype(BF16), moe_w_down[i].astype(BF16))
            out = _combine_final(h_all, y, route, mod5, final_norm_w[None, :])
    return out.reshape(BATCH, SEQ, D_MODEL)
```

```python
import functools

import numpy as np
import jax
import jax.numpy as jnp
from jax import lax
from jax.experimental import pallas as pl
from jax.experimental.pallas import tpu as pltpu

F32 = jnp.float32
BF16 = jnp.bfloat16

D_MODEL = 1024
BATCH = 4
SEQ = 4096
DEPTH = 2
GRID_W = 64
GRID_H = SEQ // GRID_W
CTX_LEN = 256
EPS = 1e-6
NEG_INF = -1e30
F_FLOOR = 1e-30
NA_HEADS = 8
NA_HEAD_DIM = 64
NA_WIN_ROWS = 8
NA_WIN_COLS = 16
HG_HEADS = 4
ML_HEADS = 4
HEAD_DIM = 128
ML_GATE_COLS = 16
ROPE_BASE = 10000.0
BRANCH_WIDTH = 512
FFN_DIM = 2816
N_EXPERTS = 8
EXPERT_DIM = 3584

N_LAT = BATCH * SEQ
N_CTX = BATCH * CTX_LEN
N_ALL = N_LAT + N_CTX

LANES = 128
VMEM_LIMIT = 56 * 1024 * 1024

MAIN_W = 12 * 512
BG_OFF = MAIN_W
PROJ_W = MAIN_W + 3 * D_MODEL

TM_PROJ = 1024
TN_PROJ = PROJ_W // 4
TM_MERGE = 256
TM_FFN = 512
NA_QROWS = 8
NA_KROWS = NA_QROWS + NA_WIN_ROWS
CHUNK = 256
TM_MOE = 512
TH_MOE = EXPERT_DIM // 2
GATHER_DMA_PRIORITY = 1
N_MOE_BLOCKS = -(-(2 * N_LAT + N_EXPERTS * (TM_MOE - 1)) // TM_MOE)
N_SLOTS = N_MOE_BLOCKS * TM_MOE


def _params(*sem):
    return pltpu.CompilerParams(dimension_semantics=sem, vmem_limit_bytes=VMEM_LIMIT)


def _sigmoid(x):
    return 0.5 * jnp.tanh(0.5 * x) + 0.5


def _silu(x):
    return x * _sigmoid(x)


def _dot(a, b):
    return jnp.dot(a, b, preferred_element_type=F32)


def _dot_nt(a, b):
    return lax.dot_general(a, b, (((1,), (1,)), ((), ())), preferred_element_type=F32)


def _split3(x):
    hi = x.astype(BF16)
    r = x - hi.astype(F32)
    mid = r.astype(BF16)
    lo = (r - mid.astype(F32)).astype(BF16)
    return hi, mid, lo


def _sel_dot(sel, x):
    hi, mid, lo = _split3(x)
    return _dot(sel, lo) + _dot(sel, mid) + _dot(sel, hi)


def _sel_dot_nt(sel, x):
    hi, mid, lo = _split3(x)
    return _dot_nt(sel, lo) + _dot_nt(sel, mid) + _dot_nt(sel, hi)


def _dot_sel(x, sel):
    hi, mid, lo = _split3(x)
    return _dot(lo, sel) + _dot(mid, sel) + _dot(hi, sel)


def _norm_mod(x, nw, shift, scale):
    y = x * lax.rsqrt(jnp.mean(x * x, axis=-1, keepdims=True) + EPS) * nw
    return y * (1.0 + scale) + shift


def _mod_kernel(c_ref, w_ref, b_ref, o_ref):
    s = _silu(c_ref[...])
    o_ref[0] = jnp.dot(s, w_ref[0], preferred_element_type=F32, precision=lax.Precision.HIGHEST) + b_ref[0]


def _modulation(c8, mod_w, mod_b):
    tn = 1536
    return pl.pallas_call(
        _mod_kernel,
        grid=(DEPTH, 6 * D_MODEL // tn),
        in_specs=[
            pl.BlockSpec((8, D_MODEL), lambda l, j: (0, 0)),
            pl.BlockSpec((1, D_MODEL, tn), lambda l, j: (l, 0, j)),
            pl.BlockSpec((1, 1, tn), lambda l, j: (l, 0, j)),
        ],
        out_specs=pl.BlockSpec((1, 8, tn), lambda l, j: (l, 0, j)),
        out_shape=jax.ShapeDtypeStruct((DEPTH, 8, 6 * D_MODEL), F32),
        compiler_params=_params("parallel", "parallel"),
        name="modulation",
    )(c8, mod_w, mod_b.reshape(DEPTH, 1, 6 * D_MODEL))


def _inproj_kernel(hl_ref, hc_ref, nw_ref, mod_ref, w_ref, wg_ref, o_ref, g_ref, a_scr):
    @pl.when(pl.program_id(1) == 0)
    def _():
        h = jnp.where(pl.program_id(0) >= N_LAT // TM_PROJ, hc_ref[...], hl_ref[...])
        a = _norm_mod(h, nw_ref[...], mod_ref[0, 0:1, :], mod_ref[0, 1:2, :])
        a_scr[...] = a.astype(BF16)
        g_ref[...] = _dot(a_scr[...], wg_ref[...])

    o_ref[...] = _dot(a_scr[...], w_ref[...])


def _inproj(h_lat, h_ctx, ctx_row0, nw, mod5, w_main, w_gates):
    tiles_per_batch = SEQ // TM_PROJ
    n_lat_tiles = N_LAT // TM_PROJ
    ctx_tile0 = ctx_row0 // TM_PROJ
    return pl.pallas_call(
        _inproj_kernel,
        grid=(N_ALL // TM_PROJ, PROJ_W // TN_PROJ),
        in_specs=[
            pl.BlockSpec((TM_PROJ, D_MODEL), lambda i, j: (jnp.minimum(i, n_lat_tiles - 1), 0)),
            pl.BlockSpec((TM_PROJ, D_MODEL), lambda i, j: (ctx_tile0 + jnp.maximum(i - n_lat_tiles, 0), 0)),
            pl.BlockSpec((1, D_MODEL), lambda i, j: (0, 0)),
            pl.BlockSpec((1, 6, D_MODEL), lambda i, j: (i // tiles_per_batch, 0, 0)),
            pl.BlockSpec((D_MODEL, TN_PROJ), lambda i, j: (0, j)),
            pl.BlockSpec((D_MODEL, LANES), lambda i, j: (0, 0)),
        ],
        out_specs=[pl.BlockSpec((TM_PROJ, TN_PROJ), lambda i, j: (i, j)),
                   pl.BlockSpec((TM_PROJ, LANES), lambda i, j: (i, 0))],
        out_shape=[jax.ShapeDtypeStruct((N_ALL, PROJ_W), F32), jax.ShapeDtypeStruct((N_ALL, LANES), F32)],
        scratch_shapes=[pltpu.VMEM((TM_PROJ, D_MODEL), BF16)],
        compiler_params=_params("parallel", "arbitrary"),
        name="inproj",
    )(h_lat, h_ctx, nw, mod5, w_main, w_gates)


N_DR = 2 * NA_WIN_ROWS - 1


def _na_bias_tables(rpb):
    qc = np.arange(GRID_W)[:, None]
    kc = np.arange(GRID_W)[None, :]
    dc = np.clip(kc - qc + NA_WIN_COLS - 1, 0, 2 * NA_WIN_COLS - 2)
    ws = np.clip(qc - NA_WIN_COLS // 2, 0, GRID_W - NA_WIN_COLS)
    col_ok = (kc >= ws) & (kc < ws + NA_WIN_COLS)
    onehot = ((dc[None] == np.arange(2 * NA_WIN_COLS - 1)[:, None, None]) & col_ok[None]).astype(np.float32)
    t = jnp.einsum('hrd,dqk->hrqk', rpb.astype(F32), jnp.asarray(onehot), precision=lax.Precision.HIGHEST)
    t = t + jnp.asarray(np.where(col_ok, 0.0, NEG_INF).astype(np.float32))
    tp = jnp.pad(t, ((0, 0), (1, 2), (0, 0), (0, 0)))
    return jnp.concatenate([tp[:, :N_DR + 2], tp[:, 1:]], axis=-1)


def _na_kernel(q_ref, k_ref, v_ref, kc_ref, vc_ref, tab_ref, o_ref):
    blk = pl.program_id(2)
    q0 = blk * NA_QROWS
    k0 = jnp.clip(q0 - NA_WIN_ROWS // 2, 0, GRID_H - NA_KROWS)
    start = pl.multiple_of(k0 * GRID_W, GRID_W)
    nk = NA_KROWS * GRID_W
    lane = lax.broadcasted_iota(jnp.int32, (GRID_W, LANES), 1)

    tab_idx, penalty = [], []
    for qr in range(NA_QROWS):
        r = q0 + qr
        r0 = jnp.clip(r - NA_WIN_ROWS // 2, 0, GRID_H - NA_WIN_ROWS)
        idx_row, pen_row = [], []
        for j in range(NA_KROWS // 2):
            kra = k0 + 2 * j
            pa = jnp.where(jnp.logical_and(kra >= r0, kra < r0 + NA_WIN_ROWS), 0.0, NEG_INF)
            pb = jnp.where(jnp.logical_and(kra + 1 >= r0, kra + 1 < r0 + NA_WIN_ROWS), 0.0, NEG_INF)
            idx_row.append(jnp.clip(kra - r + NA_WIN_ROWS, 0, N_DR + 1))
            pen_row.append(jnp.where(lane < GRID_W, pa, pb))
        tab_idx.append(idx_row)
        penalty.append(pen_row)

    outs = []
    for hh in range(2):
        sl = slice(hh * NA_HEAD_DIM, (hh + 1) * NA_HEAD_DIM)
        q = (q_ref[:, sl] * (NA_HEAD_DIM ** -0.5)).astype(BF16)
        k = k_ref[pl.ds(start, nk), sl].astype(BF16)
        v = v_ref[pl.ds(start, nk), sl].astype(BF16)
        kc = kc_ref[:, sl].astype(BF16)
        vc = vc_ref[:, sl].astype(BF16)
        s_raw = _dot_nt(q, k)
        rows = []
        for qr in range(NA_QROWS):
            cols = []
            for j in range(NA_KROWS // 2):
                s_blk = s_raw[qr * GRID_W:(qr + 1) * GRID_W, j * LANES:(j + 1) * LANES]
                cols.append(s_blk + (tab_ref[hh, tab_idx[qr][j]] + penalty[qr][j]))
            rows.append(jnp.concatenate(cols, axis=1))
        s_loc = jnp.concatenate(rows, axis=0)
        s_ctx = _dot_nt(q, kc)
        m = jnp.maximum(jnp.max(s_loc, axis=-1, keepdims=True), jnp.max(s_ctx, axis=-1, keepdims=True))
        p_loc = jnp.exp(s_loc - m)
        p_ctx = jnp.exp(s_ctx - m)
        den = jnp.sum(p_loc, axis=-1, keepdims=True) + jnp.sum(p_ctx, axis=-1, keepdims=True)
        o = _dot(p_loc.astype(BF16), v) + _dot(p_ctx.astype(BF16), vc)
        outs.append(o / den)
    o_ref[...] = jnp.concatenate(outs, axis=-1)


def _na_latent(proj, bias):
    nq = NA_QROWS * GRID_W
    nblk = GRID_H // NA_QROWS
    ctx_blk0 = N_LAT // CTX_LEN
    return pl.pallas_call(
        _na_kernel,
        grid=(BATCH, NA_HEADS // 2, nblk),
        in_specs=[
            pl.BlockSpec((nq, LANES), lambda b, hp, blk: (b * nblk + blk, hp)),
            pl.BlockSpec((SEQ, LANES), lambda b, hp, blk: (b, 4 + hp)),
            pl.BlockSpec((SEQ, LANES), lambda b, hp, blk: (b, 8 + hp)),
            pl.BlockSpec((CTX_LEN, LANES), lambda b, hp, blk: (ctx_blk0 + b, 4 + hp)),
            pl.BlockSpec((CTX_LEN, LANES), lambda b, hp, blk: (ctx_blk0 + b, 8 + hp)),
            pl.BlockSpec((2, N_DR + 2, GRID_W, LANES), lambda b, hp, blk: (hp, 0, 0, 0)),
        ],
        out_specs=pl.BlockSpec((nq, LANES), lambda b, hp, blk: (b * nblk + blk, hp)),
        out_shape=jax.ShapeDtypeStruct((N_LAT, NA_HEADS * NA_HEAD_DIM), F32),
        compiler_params=_params("parallel", "parallel", "arbitrary"),
        name="na_latent",
    )(proj, proj, proj, proj, proj, bias)


def _ctx_attn_kernel(q_ref, k_ref, v_ref, o_ref):
    outs = []
    for hh in range(2):
        sl = slice(hh * NA_HEAD_DIM, (hh + 1) * NA_HEAD_DIM)
        q = (q_ref[:, sl] * (NA_HEAD_DIM ** -0.5)).astype(BF16)
        s = _dot_nt(q, k_ref[:, sl].astype(BF16))
        p = jnp.exp(s - jnp.max(s, axis=-1, keepdims=True))
        o = _dot(p.astype(BF16), v_ref[:, sl].astype(BF16))
        outs.append(o / jnp.sum(p, axis=-1, keepdims=True))
    o_ref[...] = jnp.concatenate(outs, axis=-1)


def _ctx_attention(proj):
    ctx_blk0 = N_LAT // CTX_LEN
    return pl.pallas_call(
        _ctx_attn_kernel,
        grid=(BATCH, NA_HEADS // 2),
        in_specs=[
            pl.BlockSpec((CTX_LEN, LANES), lambda b, hp: (ctx_blk0 + b, hp)),
            pl.BlockSpec((CTX_LEN, LANES), lambda b, hp: (ctx_blk0 + b, 4 + hp)),
            pl.BlockSpec((CTX_LEN, LANES), lambda b, hp: (ctx_blk0 + b, 8 + hp)),
        ],
        out_specs=pl.BlockSpec((CTX_LEN, LANES), lambda b, hp: (b, hp)),
        out_shape=jax.ShapeDtypeStruct((N_CTX, NA_HEADS * NA_HEAD_DIM), F32),
        compiler_params=_params("parallel", "parallel"),
        name="ctx_attention",
    )(proj, proj, proj)


N_CTX_CHUNKS = CTX_LEN // CHUNK
N_LAT_CHUNKS = SEQ // CHUNK
N_STEPS = N_CTX_CHUNKS + N_LAT_CHUNKS
SUB = CHUNK // 2
HG_LEVELS = SUB.bit_length() - 1


def _chunk_block(b, s, rev):
    c_ctx = (N_CTX_CHUNKS - 1 - s) if rev else s
    c_lat = (N_LAT_CHUNKS - 1 - (s - N_CTX_CHUNKS)) if rev else (s - N_CTX_CHUNKS)
    ctx_blk = N_LAT // CHUNK + b * N_CTX_CHUNKS + c_ctx
    lat_blk = b * N_LAT_CHUNKS + c_lat
    return jnp.where(s < N_CTX_CHUNKS, ctx_blk, lat_blk)


def _tri_consts():
    i = np.arange(CHUNK)
    low = (i[None, :] <= i[:, None]).astype(np.float32)
    eye = np.eye(CHUNK, dtype=np.float32)
    return jnp.asarray(np.stack([low, low.T, eye]), BF16)


def _hgrn_consts():
    t = np.arange(SUB)[:, None]
    s = np.arange(SUB)[None, :]
    low = (s <= t).astype(np.float32)
    masks = np.zeros((2, HG_LEVELS + 1, SUB, SUB), np.float32)
    for l in range(HG_LEVELS):
        pair = ((t ^ s) >> l) == 1
        masks[0, l] = pair & (t > s)
        masks[1, l] = pair & (t < s)
    masks[:, HG_LEVELS] = (t == s)
    return jnp.asarray(np.stack([low, low.T]), BF16), jnp.asarray(masks)


def _hgrn_chain(q_ref, z_ref, v_ref, o_ref, sl, lb, st_ref, tri, masks_ref, rev):
    L, S = CHUNK, SUB
    z = z_ref[:, sl]
    v = v_ref[:, sl]
    e = jnp.exp(-jnp.abs(z))
    r = 1.0 / (1.0 + e)
    pos = z >= 0
    sig = jnp.where(pos, r, e * r)
    nsig = jnp.where(pos, e * r, r)
    k = (1.0 - lb) * nsig
    log2f = jnp.log2(jnp.maximum(lb + (1.0 - lb) * sig, F_FLOOR))
    q = _silu(q_ref[:, sl])
    yield

    g0 = _sel_dot(tri, log2f[:S])
    g1 = _sel_dot(tri, log2f[S:])
    if rev:
        g0 = g0 + g1[0:1, :]
    else:
        g1 = g1 + g0[S - 1:S, :]
    g = jnp.concatenate([g0, g1], axis=0)
    row = lax.broadcasted_iota(jnp.int32, (L, HEAD_DIM), 0)

    st = st_ref[...]
    o_inter = _dot_nt((q * jnp.exp2(g)).astype(BF16), st.astype(BF16))
    yield

    a = [jnp.zeros((S, S), F32), jnp.zeros((S, S), F32)]
    cross = None
    q16 = q.astype(BF16)
    k16 = k.astype(BF16)
    bnd = g
    for l in range(HG_LEVELS + 1):
        blk = 1 << l
        q_side = ((row & blk) == 0) if rev else ((row & blk) != 0)
        prev_end = pltpu.roll(bnd, (L - blk) if rev else blk, 0)
        w16 = jnp.exp2(jnp.where(q_side, g - prev_end, bnd - g)).astype(BF16)
        qb = q16 * w16
        kb = k16 * w16
        if blk < S:
            for c in range(2):
                a[c] = a[c] + masks_ref[l] * _dot_nt(qb[c * S:(c + 1) * S], kb[c * S:(c + 1) * S])
        elif rev:
            cross = _dot_nt(qb[:S], kb[S:])
        else:
            cross = _dot_nt(qb[S:], kb[:S])
        nxt = pltpu.roll(bnd, blk if rev else (L - blk), 0)
        bnd = jnp.where(q_side, bnd, nxt)
        yield
    v16 = v.astype(BF16)
    for c in range(2):
        a[c] = a[c] + masks_ref[HG_LEVELS] * _dot_nt(q16[c * S:(c + 1) * S], k16[c * S:(c + 1) * S])
    if rev:
        o0 = _dot(jnp.concatenate([a[0], cross], axis=1).astype(BF16), v16)
        o1 = _dot(a[1].astype(BF16), v16[S:])
    else:
        o0 = _dot(a[0].astype(BF16), v16[:S])
        o1 = _dot(jnp.concatenate([cross, a[1]], axis=1).astype(BF16), v16)
    o_ref[:, sl] = o_inter + jnp.concatenate([o0, o1], axis=0)
    yield

    kd = k * jnp.exp2(bnd - g)
    st_ref[...] = jnp.exp2(bnd[0:1, :]) * st + lax.dot_general(
        v16, kd.astype(BF16), (((0,), (0,)), ((), ())), preferred_element_type=F32)


def _round_robin(chains):
    alive = list(chains)
    while alive:
        still = []
        for c in alive:
            try:
                next(c)
                still.append(c)
            except StopIteration:
                pass
        alive = still


def _hgrn_kernel(qf_ref, zf_ref, vf_ref, qb_ref, zb_ref, vb_ref, lb_ref, tri_ref, masks_ref,
                 of_ref, ob_ref, stf_ref, stb_ref):
    @pl.when(pl.program_id(1) == 0)
    def _():
        stf_ref[...] = jnp.zeros_like(stf_ref)
        stb_ref[...] = jnp.zeros_like(stb_ref)

    chains = []
    for h in range(HG_HEADS):
        sl = slice(h * HEAD_DIM, (h + 1) * HEAD_DIM)
        chains.append(_hgrn_chain(qf_ref, zf_ref, vf_ref, of_ref, sl, lb_ref[0:1, sl], stf_ref.at[h],
                                  tri_ref[0], masks_ref.at[0], False))
        chains.append(_hgrn_chain(qb_ref, zb_ref, vb_ref, ob_ref, sl, lb_ref[1:2, sl], stb_ref.at[h],
                                  tri_ref[1], masks_ref.at[1], True))
    _round_robin(chains)


def _hgrn2(proj, lower, tri, masks):
    width = HG_HEADS * HEAD_DIM

    def spec(col, rev):
        return pl.BlockSpec((CHUNK, width), lambda b, s: (_chunk_block(b, s, rev), col))

    out_spec = lambda rev: pl.BlockSpec((CHUNK, width), lambda b, s: (_chunk_block(b, s, rev), 0))
    shape = jax.ShapeDtypeStruct((N_ALL, width), F32)
    return pl.pallas_call(
        _hgrn_kernel,
        grid=(BATCH, N_STEPS),
        in_specs=[
            spec(3, False), spec(4, False), spec(6, False),
            spec(3, True), spec(5, True), spec(6, True),
            pl.BlockSpec((2, width), lambda b, s: (0, 0)),
            pl.BlockSpec((2, SUB, SUB), lambda b, s: (0, 0, 0)),
            pl.BlockSpec((2, HG_LEVELS + 1, SUB, SUB), lambda b, s: (0, 0, 0, 0)),
        ],
        out_specs=[out_spec(False), out_spec(True)],
        out_shape=[shape, shape],
        scratch_shapes=[pltpu.VMEM((HG_HEADS, HEAD_DIM, HEAD_DIM), F32), pltpu.VMEM((HG_HEADS, HEAD_DIM, HEAD_DIM), F32)],
        compiler_params=_params("parallel", "arbitrary"),
        name="hgrn2",
    )(proj, proj, proj, proj, proj, proj, lower, tri, masks)


def _rope_tables():
    n_freq = HEAD_DIM // 4
    inv_freq = ROPE_BASE ** (-np.arange(n_freq, dtype=np.float64) / n_freq)
    t = np.arange(SEQ)
    ang_r = (t // GRID_W).astype(np.float64)[:, None] * inv_freq
    ang_c = (t % GRID_W).astype(np.float64)[:, None] * inv_freq
    cos = np.concatenate([np.cos(ang_r), np.cos(ang_r), np.cos(ang_c), np.cos(ang_c)], axis=-1)
    sin = np.concatenate([-np.sin(ang_r), np.sin(ang_r), -np.sin(ang_c), np.sin(ang_c)], axis=-1)
    cos = np.concatenate([cos, np.ones((CHUNK, HEAD_DIM))], axis=0)
    sin = np.concatenate([sin, np.zeros((CHUNK, HEAD_DIM))], axis=0)
    return jnp.asarray(cos, F32), jnp.asarray(sin, F32)


def _rope_swap():
    l = np.arange(HEAD_DIM)
    return jnp.asarray((l[:, None] == (l[None, :] ^ 32)).astype(np.float32), BF16)


def _rope(x, cos, sin, swap):
    hi = x.astype(BF16)
    mid = (x - hi.astype(F32)).astype(BF16)
    partner = _dot(hi, swap) + _dot(mid, swap)
    return x * cos + partner * sin


def _mlstm_gates(g_ref, gbias, tri_ref, rev):
    L = CHUNK
    log_i = g_ref[0, 0] + gbias[0][:, 0:1]
    xf = g_ref[0, 1] + gbias[1][:, 0:1]
    log_f = jnp.minimum(xf, 0.0) - jnp.log(1.0 + jnp.exp(-jnp.abs(xf)))
    r8 = jnp.concatenate([log_f, log_i], axis=0)
    low, up, eye = tri_ref[0], tri_ref[1], tri_ref[2]
    b_row = _dot_sel(r8, low if rev else up)[0:ML_HEADS, :]
    b_col = _sel_dot_nt(up if rev else low, r8)[:, 0:ML_HEADS]
    i_col = _sel_dot_nt(eye, r8)[:, ML_HEADS:2 * ML_HEADS]
    return log_i, b_row, b_col, i_col


def _mlstm_kernel(qf_ref, kf_ref, vf_ref, gf_ref, cf_ref, sf_ref,
                  qb_ref, kb_ref, vb_ref, gb_ref, cb_ref, sb_ref,
                  gbias_ref, tri_ref, swap_ref, of_ref, ob_ref, cnf_ref, cnb_ref, mf_ref, mb_ref):
    @pl.when(pl.program_id(1) == 0)
    def _():
        cnf_ref[...] = jnp.zeros_like(cnf_ref)
        cnb_ref[...] = jnp.zeros_like(cnb_ref)
        mf_ref[...] = jnp.zeros_like(mf_ref)
        mb_ref[...] = jnp.zeros_like(mb_ref)

    L = CHUNK
    ti = lax.broadcasted_iota(jnp.int32, (L, L), 0)
    si = lax.broadcasted_iota(jnp.int32, (L, L), 1)
    ones = jnp.ones((L, HEAD_DIM), BF16)
    swap = swap_ref[...]
    dirs = ((qf_ref, kf_ref, vf_ref, gf_ref, cf_ref, sf_ref, of_ref, cnf_ref, mf_ref, False),
            (qb_ref, kb_ref, vb_ref, gb_ref, cb_ref, sb_ref, ob_ref, cnb_ref, mb_ref, True))

    chains = []
    for d, (q_ref, k_ref, v_ref, g_ref, c_ref, s_ref, o_ref, cn_ref, m_ref, rev) in enumerate(dirs):
        log_i, b_row, b_col, i_col = _mlstm_gates(g_ref, gbias_ref[d], tri_ref, rev)
        for h in range(ML_HEADS):
            bc = jnp.broadcast_to(b_col[:, h:h + 1], (L, HEAD_DIM))
            ic = jnp.broadcast_to(i_col[:, h:h + 1], (L, HEAD_DIM))
            chains.append(dict(
                sl=slice(h * HEAD_DIM, (h + 1) * HEAD_DIM), rev=rev, q_ref=q_ref, k_ref=k_ref, v_ref=v_ref,
                c_ref=c_ref, s_ref=s_ref, o_ref=o_ref, cn_ref=cn_ref.at[h], m_ref=m_ref.at[h],
                log_i=log_i[h:h + 1, :], b_row=b_row[h:h + 1, :], b_col=bc, i_col=ic))

    for c in chains:
        cos, sin = c['c_ref'][...], c['s_ref'][...]
        c['qc'] = _rope(c['q_ref'][:, c['sl']], cos, sin, swap).astype(BF16)
        kc = _rope(c['k_ref'][:, c['sl']] * (HEAD_DIM ** -0.5), cos, sin, swap)
        c['kc'] = kc
        c['s'] = _dot_nt(c['qc'], kc.astype(BF16))
    for c in chains:
        b_row = c['b_row']
        c['b_end'] = b_row[:, 0:1] if c['rev'] else b_row[:, L - 1:L]
        tri = (si >= ti) if c['rev'] else (si <= ti)
        c['m_prev'] = c['m_ref'][0:1, 0:1]
        bc2 = jnp.concatenate([c['b_col'], c['b_col']], axis=-1)
        dmat = jnp.where(tri, bc2 + (c['log_i'] - b_row), NEG_INF)
        inter = c['b_col'] + c['m_prev']
        m_t = jnp.maximum(inter, jnp.max(dmat, axis=-1, keepdims=True))
        c['m_t'] = m_t
        c['w_inter'] = jnp.exp(inter - m_t)
        c['p'] = (jnp.exp(dmat - jnp.concatenate([m_t, m_t], axis=-1)) * c['s']).astype(BF16)
    for c in chains:
        c['v_ext'] = jnp.concatenate([c['v_ref'][:, c['sl']].astype(BF16), ones], axis=-1)
        c['cn'] = c['cn_ref'][...]
        w2 = jnp.concatenate([c['w_inter'], c['w_inter']], axis=-1)
        acc = _dot(c['p'], c['v_ext']) + w2 * _dot(c['qc'], c['cn'].astype(BF16))
        den = acc[:, HEAD_DIM:]
        c['o_ref'][:, c['sl']] = acc[:, :HEAD_DIM] / jnp.maximum(jnp.abs(den), jnp.exp(-c['m_t']))
    for c in chains:
        e_row = c['b_end'] + (c['log_i'] - c['b_row'])
        m_new = jnp.maximum(c['b_end'] + c['m_prev'], jnp.max(e_row, axis=-1, keepdims=True))
        w_old = jnp.exp(c['b_end'] + c['m_prev'] - m_new)
        w_s = jnp.exp(c['b_end'] - c['b_col'] + c['i_col'] - m_new)
        c['cn_ref'][...] = w_old * c['cn'] + lax.dot_general(
            (w_s * c['kc']).astype(BF16), c['v_ext'], (((0,), (0,)), ((), ())), preferred_element_type=F32)
        c['m_ref'][...] = jnp.broadcast_to(m_new, c['m_ref'].shape)


def _mlstm(proj, gates_t, gbias, cos, sin, tri):
    width = ML_HEADS * HEAD_DIM

    def spec(col, rev):
        return pl.BlockSpec((CHUNK, width), lambda b, s: (_chunk_block(b, s, rev), col))

    def gate_spec(rev):
        d = 1 if rev else 0
        return pl.BlockSpec((1, 2, ML_HEADS, CHUNK), lambda b, s: (d, 0, 0, _chunk_block(b, s, rev)))

    def rope_spec(rev):
        def idx(b, s):
            lat = _chunk_block(b, s, rev) - b * N_LAT_CHUNKS
            return (jnp.where(s < N_CTX_CHUNKS, N_LAT_CHUNKS, lat), 0)
        return pl.BlockSpec((CHUNK, LANES), idx)

    out_spec = lambda rev: pl.BlockSpec((CHUNK, width), lambda b, s: (_chunk_block(b, s, rev), 0))
    shape = jax.ShapeDtypeStruct((N_ALL, width), F32)
    per_dir = lambda rev: [spec(8, rev), spec(9, rev), spec(10, rev), gate_spec(rev), rope_spec(rev), rope_spec(rev)]
    args_dir = [proj, proj, proj, gates_t, cos, sin]
    return pl.pallas_call(
        _mlstm_kernel,
        grid=(BATCH, N_STEPS),
        in_specs=per_dir(False) + per_dir(True) + [
            pl.BlockSpec((2, 2, ML_HEADS, LANES), lambda b, s: (0, 0, 0, 0)),
            pl.BlockSpec((3, CHUNK, CHUNK), lambda b, s: (0, 0, 0)),
            pl.BlockSpec((HEAD_DIM, HEAD_DIM), lambda b, s: (0, 0)),
        ],
        out_specs=[out_spec(False), out_spec(True)],
        out_shape=[shape, shape],
        scratch_shapes=[pltpu.VMEM((ML_HEADS, HEAD_DIM, 2 * HEAD_DIM), F32),
                        pltpu.VMEM((ML_HEADS, HEAD_DIM, 2 * HEAD_DIM), F32),
                        pltpu.VMEM((ML_HEADS, 8, LANES), F32), pltpu.VMEM((ML_HEADS, 8, LANES), F32)],
        compiler_params=_params("parallel", "arbitrary"),
        name="mlstm",
    )(*args_dir, *args_dir, gbias, tri, _rope_swap())


def _head_rms(x, w):
    parts = []
    for hh in range(x.shape[-1] // HEAD_DIM):
        xs = x[:, hh * HEAD_DIM:(hh + 1) * HEAD_DIM]
        parts.append(xs * lax.rsqrt(jnp.mean(xs * xs, axis=-1, keepdims=True) + EPS))
    return jnp.concatenate(parts, axis=-1) * w


def _merge_kernel(na_ref, nac_ref, hgf_ref, hgb_ref, mlf_ref, mlb_ref, hgg_ref, mlo_ref, bg0_ref, bg1_ref, bg2_ref,
                  hl_ref, hc_ref, mod_ref, hgw_ref, mlw_ref, wb_ref, wo_ref, o_ref):
    hg = _head_rms(hgf_ref[...] + hgb_ref[...], hgw_ref[...]) * _silu(hgg_ref[...])
    ml = _sigmoid(mlo_ref[...]) * _head_rms(mlf_ref[...] + mlb_ref[...], mlw_ref[...])
    is_ctx = pl.program_id(0) >= N_LAT // TM_MERGE
    na = jnp.where(is_ctx, nac_ref[...], na_ref[...])
    y2 = None
    for bg_ref, branch, i in ((bg0_ref, na, 0), (bg1_ref, hg, 1), (bg2_ref, ml, 2)):
        p = _dot(branch.astype(BF16), wb_ref[i])
        term = jnp.tanh(bg_ref[...]) * p + p
        y2 = term if y2 is None else y2 + term
    h = jnp.where(is_ctx, hc_ref[...], hl_ref[...])
    o_ref[...] = h + (0.5 * mod_ref[0, 2:3, :]) * _dot(y2.astype(BF16), wo_ref[...])


def _merge(n_rows, na, na_ctx, hgf, hgb, mlf, mlb, proj, h_lat, h_ctx, ctx_row0, mod5, hg_w, ml_w, w_branch, w_out):
    tm = TM_MERGE
    tiles_per_batch = SEQ // tm
    n_lat_tiles = N_LAT // tm
    ctx_tile0 = ctx_row0 // tm
    row = lambda w, c: pl.BlockSpec((tm, w), lambda i: (i, c))
    const = lambda shape: pl.BlockSpec(shape, lambda i: (0,) * len(shape))
    bg0 = BG_OFF // D_MODEL
    return pl.pallas_call(
        _merge_kernel,
        grid=(n_rows // tm,),
        in_specs=[
            pl.BlockSpec((tm, 512), lambda i: (jnp.minimum(i, n_lat_tiles - 1), 0)),
            pl.BlockSpec((tm, 512), lambda i: (jnp.maximum(i - n_lat_tiles, 0), 0)),
            row(512, 0), row(512, 0), row(512, 0), row(512, 0),
            row(512, 7), row(512, 11),
            row(D_MODEL, bg0), row(D_MODEL, bg0 + 1), row(D_MODEL, bg0 + 2),
            pl.BlockSpec((tm, D_MODEL), lambda i: (jnp.minimum(i, n_lat_tiles - 1), 0)),
            pl.BlockSpec((tm, D_MODEL), lambda i: (ctx_tile0 + jnp.maximum(i - n_lat_tiles, 0), 0)),
            pl.BlockSpec((1, 6, D_MODEL), lambda i: (jnp.minimum(i // tiles_per_batch, BATCH), 0, 0)),
            const((1, 512)), const((1, 512)),
            const((3, BRANCH_WIDTH, D_MODEL)), const((D_MODEL, D_MODEL)),
        ],
        out_specs=row(D_MODEL, 0),
        out_shape=jax.ShapeDtypeStruct((n_rows, D_MODEL), F32),
        compiler_params=_params("parallel"),
        name="merge",
    )(na, na_ctx, hgf, hgb, mlf, mlb, proj, proj, proj, proj, proj, h_lat, h_ctx, mod5, hg_w, ml_w, w_branch, w_out)


def _ffn_kernel(h_ref, nw_ref, mod_ref, wa_ref, wu_ref, wd_ref, o_ref):
    h = h_ref[...]
    f = _norm_mod(h, nw_ref[...], mod_ref[0, 3:4, :], mod_ref[0, 4:5, :]).astype(BF16)
    g = _silu(_dot(f, wa_ref[...])) * _dot(f, wu_ref[...])
    o_ref[...] = h + mod_ref[0, 5:6, :] * _dot(g.astype(BF16), wd_ref[...])


def _ffn(h_all, nw, mod5, w_up, w_down):
    n_rows = h_all.shape[0]
    tiles_per_batch = SEQ // TM_FFN
    resident = pl.Buffered(1)
    return pl.pallas_call(
        _ffn_kernel,
        grid=(n_rows // TM_FFN,),
        in_specs=[
            pl.BlockSpec((TM_FFN, D_MODEL), lambda i: (i, 0)),
            pl.BlockSpec((1, D_MODEL), lambda i: (0, 0)),
            pl.BlockSpec((1, 6, D_MODEL), lambda i: (i // tiles_per_batch, 0, 0)),
            pl.BlockSpec((D_MODEL, FFN_DIM), lambda i: (0, 0), pipeline_mode=resident),
            pl.BlockSpec((D_MODEL, FFN_DIM), lambda i: (0, 1), pipeline_mode=resident),
            pl.BlockSpec((FFN_DIM, D_MODEL), lambda i: (0, 0), pipeline_mode=resident),
        ],
        out_specs=pl.BlockSpec((TM_FFN, D_MODEL), lambda i: (i, 0)),
        out_shape=jax.ShapeDtypeStruct((n_rows, D_MODEL), F32),
        compiler_params=_params("parallel"),
        name="ffn",
    )(h_all, nw, mod5, w_up, w_up, w_down)


def _router_kernel(h_ref, nw_ref, mod_ref, wr_ref, f_ref, r_ref):
    f = _norm_mod(h_ref[...], nw_ref[...], mod_ref[0, 3:4, :], mod_ref[0, 4:5, :])
    f_ref[...] = f
    logits = jnp.dot(f, wr_ref[...], preferred_element_type=F32, precision=lax.Precision.HIGHEST)
    lane = lax.broadcasted_iota(jnp.int32, logits.shape, 1)
    logits = jnp.where(lane < N_EXPERTS, logits, -jnp.inf)
    m1 = jnp.max(logits, axis=-1, keepdims=True)
    i1 = jnp.min(jnp.where(logits == m1, lane, LANES), axis=-1, keepdims=True)
    rest = jnp.where(lane == i1, -jnp.inf, logits)
    m2 = jnp.max(rest, axis=-1, keepdims=True)
    i2 = jnp.min(jnp.where(rest == m2, lane, LANES), axis=-1, keepdims=True)
    e2 = jnp.exp(m2 - m1)
    w1 = 1.0 / (1.0 + e2)
    w2 = e2 / (1.0 + e2)
    r_ref[...] = jnp.where(lane == 0, i1.astype(F32),
                           jnp.where(lane == 1, i2.astype(F32),
                                     jnp.where(lane == 2, w1, jnp.where(lane == 3, w2, 0.0))))


def _router(h_lat, nw, mod5, w_router_pad):
    tm = 512
    tiles_per_batch = SEQ // tm
    return pl.pallas_call(
        _router_kernel,
        grid=(N_LAT // tm,),
        in_specs=[
            pl.BlockSpec((tm, D_MODEL), lambda i: (i, 0)),
            pl.BlockSpec((1, D_MODEL), lambda i: (0, 0)),
            pl.BlockSpec((1, 6, D_MODEL), lambda i: (i // tiles_per_batch, 0, 0)),
            pl.BlockSpec((D_MODEL, LANES), lambda i: (0, 0)),
        ],
        out_specs=[pl.BlockSpec((tm, D_MODEL), lambda i: (i, 0)), pl.BlockSpec((tm, LANES), lambda i: (i, 0))],
        out_shape=[jax.ShapeDtypeStruct((N_LAT, D_MODEL), F32), jax.ShapeDtypeStruct((N_LAT, LANES), F32)],
        compiler_params=_params("parallel"),
        name="router",
    )(h_lat, nw, mod5, w_router_pad)


def _moe_kernel(be_ref, nused_ref, nvalid_ref, code_ref, f_hbm, wa_ref, wu_ref, wd_ref, y_hbm,
                xbuf, x16, acc, ybuf, sem_in, sem_out):
    i = pl.program_id(0)
    j = pl.program_id(1)
    last_j = pl.num_programs(1) - 1
    n_used = nused_ref[0]
    active = i < n_used
    slot = i % 2

    def start_gather(blk, buf):
        def body(r, c):
            tok = jnp.maximum(code_ref[blk * TM_MOE + r], 0) >> 1
            pltpu.make_async_copy(f_hbm.at[pl.ds(tok, 1)], xbuf.at[buf, pl.ds(r, 1)],
                                  sem_in.at[buf]).start(priority=GATHER_DMA_PRIORITY)
            return c
        lax.fori_loop(0, TM_MOE, body, 0, unroll=8)

    def wait_gather(buf):
        pltpu.make_async_copy(f_hbm.at[pl.ds(0, TM_MOE)], xbuf.at[buf], sem_in.at[buf]).wait()

    def start_scatter(blk):
        def body(r, c):
            code = code_ref[blk * TM_MOE + r]
            dst = (code & 1) * N_LAT + (code >> 1)
            pltpu.make_async_copy(ybuf.at[pl.ds(r, 1)], y_hbm.at[pl.ds(dst, 1)], sem_out).start()
            return c
        lax.fori_loop(0, nvalid_ref[blk], body, 0)

    def wait_scatter(blk):
        n = nvalid_ref[blk]
        p = TM_MOE
        while p >= 8:
            @pl.when((n & p) != 0)
            def _(p=p):
                pltpu.make_async_copy(ybuf.at[pl.ds(0, p)], y_hbm.at[pl.ds(0, p)], sem_out).wait()
            p //= 2

        def one(r, c):
            pltpu.make_async_copy(ybuf.at[pl.ds(0, 1)], y_hbm.at[pl.ds(0, 1)], sem_out).wait()
            return c
        lax.fori_loop(0, n & 7, one, 0)

    def prefetch_rows(first, count):
        for r in range(first, first + count):
            tok = jnp.maximum(code_ref[(i + 1) * TM_MOE + r], 0) >> 1
            pltpu.make_async_copy(f_hbm.at[pl.ds(tok, 1)], xbuf.at[1 - slot, pl.ds(r, 1)],
                                  sem_in.at[1 - slot]).start(priority=GATHER_DMA_PRIORITY)

    def partial_out():
        x = x16[...]
        g = _silu(_dot(x, wa_ref[0])) * _dot(x, wu_ref[0])
        return _dot(g.astype(BF16), wd_ref[0])

    @pl.when(jnp.logical_and(active, j == 0))
    def _():
        @pl.when(i == 0)
        def _():
            start_gather(0, 0)

        wait_gather(slot)
        x16[...] = xbuf[slot].astype(BF16)

    @pl.when(jnp.logical_and(active, j == 0))
    def _():
        prefetch_rows(0, TM_MOE // 2)
        acc[...] = partial_out()

    @pl.when(jnp.logical_and(active, j == last_j))
    def _():
        @pl.when(i > 0)
        def _():
            wait_scatter(i - 1)

    @pl.when(jnp.logical_and(active, j == last_j))
    def _():
        prefetch_rows(TM_MOE // 2, TM_MOE // 2)
        ybuf[...] = acc[...] + partial_out()

    @pl.when(jnp.logical_and(active, j == last_j))
    def _():
        start_scatter(i)

        @pl.when(i == n_used - 1)
        def _():
            wait_scatter(i)
            wait_gather(1 - slot)


def _moe_experts(block_e, n_used, n_valid, codes, f_lat, w_up, w_down):
    nj = EXPERT_DIM // TH_MOE
    assert nj == 2 and nj * TH_MOE == EXPERT_DIM

    def jj(i, j, nu):
        return jnp.where(i < nu[0], j, nj - 1)

    grid_spec = pltpu.PrefetchScalarGridSpec(
        num_scalar_prefetch=4,
        grid=(N_MOE_BLOCKS, nj),
        in_specs=[
            pl.BlockSpec(memory_space=pl.ANY),
            pl.BlockSpec((1, D_MODEL, TH_MOE), lambda i, j, be, nu, nv, cd: (be[i], 0, jj(i, j, nu))),
            pl.BlockSpec((1, D_MODEL, TH_MOE), lambda i, j, be, nu, nv, cd: (be[i], 0, nj + jj(i, j, nu))),
            pl.BlockSpec((1, TH_MOE, D_MODEL), lambda i, j, be, nu, nv, cd: (be[i], jj(i, j, nu), 0)),
        ],
        out_specs=pl.BlockSpec(memory_space=pl.ANY),
        scratch_shapes=[
            pltpu.VMEM((2, TM_MOE, D_MODEL), F32), pltpu.VMEM((TM_MOE, D_MODEL), BF16),
            pltpu.VMEM((TM_MOE, D_MODEL), F32), pltpu.VMEM((TM_MOE, D_MODEL), F32),
            pltpu.SemaphoreType.DMA((2,)), pltpu.SemaphoreType.DMA(()),
        ],
    )
    return pl.pallas_call(
        _moe_kernel,
        grid_spec=grid_spec,
        out_shape=jax.ShapeDtypeStruct((2 * N_LAT, D_MODEL), F32),
        compiler_params=_params("arbitrary", "arbitrary"),
        name="moe_experts",
    )(block_e, n_used, n_valid, codes, f_lat, w_up, w_up, w_down)


def _combine_kernel(h_ref, y1_ref, y2_ref, r_ref, mod_ref, fw_ref, o_ref):
    r = r_ref[...]
    y = r[:, 2:3] * y1_ref[...] + r[:, 3:4] * y2_ref[...]
    h = h_ref[...] + mod_ref[0, 5:6, :] * y
    o_ref[...] = h * lax.rsqrt(jnp.mean(h * h, axis=-1, keepdims=True) + EPS) * fw_ref[...]


def _combine_final(h_lat, y, route, mod5, final_w):
    tm = 512
    tiles_per_batch = SEQ // tm
    return pl.pallas_call(
        _combine_kernel,
        grid=(N_LAT // tm,),
        in_specs=[
            pl.BlockSpec((tm, D_MODEL), lambda i: (i, 0)),
            pl.BlockSpec((tm, D_MODEL), lambda i: (i, 0)),
            pl.BlockSpec((tm, D_MODEL), lambda i: (N_LAT // tm + i, 0)),
            pl.BlockSpec((tm, LANES), lambda i: (i, 0)),
            pl.BlockSpec((1, 6, D_MODEL), lambda i: (i // tiles_per_batch, 0, 0)),
            pl.BlockSpec((1, D_MODEL), lambda i: (0, 0)),
        ],
        out_specs=pl.BlockSpec((tm, D_MODEL), lambda i: (i, 0)),
        out_shape=jax.ShapeDtypeStruct((N_LAT, D_MODEL), F32),
        compiler_params=_params("parallel"),
        name="combine_final",
    )(h_lat, y, y, route, mod5, final_w)


def _moe_plan(route):
    e12 = route[:, 0:2].astype(jnp.int32)
    onehot = (e12[:, :, None] == jnp.arange(N_EXPERTS, dtype=jnp.int32)).astype(jnp.int32).sum(axis=1)
    before = jnp.cumsum(onehot, axis=0) - onehot
    counts = jnp.sum(onehot, axis=0)
    nblk = (counts + TM_MOE - 1) // TM_MOE
    blk_end = jnp.cumsum(nblk)
    slot0 = (blk_end - nblk) * TM_MOE
    rank = jnp.take_along_axis(before, e12, axis=1)
    dest = slot0[e12] + rank
    codes = jnp.full((N_SLOTS,), -1, jnp.int32).at[dest.reshape(-1)].set(jnp.arange(2 * N_LAT, dtype=jnp.int32))
    n_used = blk_end[-1]
    blocks = jnp.minimum(jnp.arange(N_MOE_BLOCKS, dtype=jnp.int32), n_used - 1)
    block_e = jnp.minimum(jnp.sum((blocks[:, None] >= blk_end[None, :]).astype(jnp.int32), axis=1), N_EXPERTS - 1)
    n_valid = jnp.sum((codes >= 0).astype(jnp.int32).reshape(N_MOE_BLOCKS, TM_MOE), axis=1)
    return block_e, n_used.reshape(1).astype(jnp.int32), n_valid, codes


def kernel(x, c, ctx, c_ctx, mod_w, mod_b, norm1_w, w_in, na_rpb, hg_lb, hg_norm_w, ml_gate_b, ml_norm_w,
           w_branch, w_out, norm2_w, ffn_w_up, ffn_w_down, moe_router, moe_w_up, moe_w_down, final_norm_w):
    h_lat, h_ctx, ctx_row0 = x.reshape(N_LAT, D_MODEL), ctx.reshape(N_CTX, D_MODEL), 0
    c8 = jnp.concatenate([c, c_ctx[None, :], jnp.zeros((3, D_MODEL), F32)], axis=0)
    mods = _modulation(c8, mod_w, mod_b).reshape(DEPTH, 8, 6, D_MODEL)

    lb_p = jax.nn.softmax(hg_lb.astype(F32), axis=0)
    hg_lower = jnp.cumsum(lb_p, axis=0) - lb_p[0]
    tri = _tri_consts()
    hg_tri, hg_masks = _hgrn_consts()
    cos, sin = _rope_tables()

    out = None
    for layer in range(DEPTH):
        last = layer == DEPTH - 1
        mod5 = mods[layer, :5]
        wl = w_in[layer]
        w_main = jnp.concatenate([wl[:, :MAIN_W], 0.5 * wl[:, MAIN_W + ML_GATE_COLS:]], axis=1).astype(BF16)
        w_gates = jnp.pad(wl[:, MAIN_W:MAIN_W + ML_GATE_COLS], ((0, 0), (0, LANES - ML_GATE_COLS))).astype(BF16)
        proj, gates = _inproj(h_lat, h_ctx, ctx_row0, norm1_w[layer][None, :], mod5, w_main, w_gates)

        na = _na_latent(proj, _na_bias_tables(na_rpb[layer]))
        na_ctx = na if last else _ctx_attention(proj)
        hgf, hgb = _hgrn2(proj, hg_lower[layer], hg_tri, hg_masks)
        gates_t = gates[:, :ML_GATE_COLS].T.reshape(2, 2, ML_HEADS, N_ALL)
        gbias = jnp.broadcast_to(ml_gate_b[layer][..., None], (2, 2, ML_HEADS, LANES))
        mlf, mlb = _mlstm(proj, gates_t, gbias, cos, sin, tri)

        n_rows = N_LAT if last else N_ALL
        h_all = _merge(n_rows, na, na_ctx, hgf, hgb, mlf, mlb, proj, h_lat, h_ctx, ctx_row0, mod5,
                       hg_norm_w[layer][None, :], ml_norm_w[layer][None, :],
                       w_branch[layer].astype(BF16), w_out[layer].astype(BF16))
        i = layer // 2
        if layer % 2 == 0:
            h_all = _ffn(h_all, norm2_w[layer][None, :], mod5, ffn_w_up[i].astype(BF16), ffn_w_down[i].astype(BF16))
            h_lat, h_ctx, ctx_row0 = h_all, h_all, N_LAT
            if last:
                raise NotImplementedError("final norm after a dense last layer")
        else:
            if not last:
                raise NotImplementedError("MoE on the context stream")
            w_router_pad = jnp.pad(moe_router[i], ((0, 0), (0, LANES - N_EXPERTS)))
            f_lat, route = _router(h_all, norm2_w[layer][None, :], mod5, w_router_pad)
            block_e, n_used, n_valid, codes = _moe_plan(route)
            y = _moe_experts(block_e, n_used, n_valid, codes, f_lat,
                             moe_w_up[i].astype(BF16), moe_w_down[i].astype(BF16))
            out = _combine_final(h_all, y, route, mod5, final_norm_w[None, :])
    return out.reshape(BATCH, SEQ, D_MODEL)
```

```python
import functools

import numpy as np
import jax
import jax.numpy as jnp
from jax import lax
from jax.experimental import pallas as pl
from jax.experimental.pallas import tpu as pltpu

F32 = jnp.float32
BF16 = jnp.bfloat16

D_MODEL = 1024
BATCH = 4
SEQ = 4096
DEPTH = 2
GRID_W = 64
GRID_H = SEQ // GRID_W
CTX_LEN = 256
EPS = 1e-6
NEG_INF = -1e30
F_FLOOR = 1e-30
NA_HEADS = 8
NA_HEAD_DIM = 64
NA_WIN_ROWS = 8
NA_WIN_COLS = 16
HG_HEADS = 4
ML_HEADS = 4
HEAD_DIM = 128
ML_GATE_COLS = 16
ROPE_BASE = 10000.0
BRANCH_WIDTH = 512
FFN_DIM = 2816
N_EXPERTS = 8
EXPERT_DIM = 3584

N_LAT = BATCH * SEQ
N_CTX = BATCH * CTX_LEN
N_ALL = N_LAT + N_CTX

LANES = 128
VMEM_LIMIT = 56 * 1024 * 1024

MAIN_W = 12 * 512
BG_OFF = MAIN_W
PROJ_W = MAIN_W + 3 * D_MODEL

TM_PROJ = 1024
TN_PROJ = PROJ_W // 4
TM_MERGE = 256
TM_FFN = 512
NA_SUBS = 2
NA_QROWS = 4
NA_KROWS = NA_QROWS + NA_WIN_ROWS
CHUNK = 256
TM_MOE = 512
TH_MOE = EXPERT_DIM // 2
GATHER_DMA_PRIORITY = 1
N_MOE_BLOCKS = -(-(2 * N_LAT + N_EXPERTS * (TM_MOE - 1)) // TM_MOE)
N_SLOTS = N_MOE_BLOCKS * TM_MOE


def _params(*sem):
    return pltpu.CompilerParams(dimension_semantics=sem, vmem_limit_bytes=VMEM_LIMIT)


def _sigmoid(x):
    return 0.5 * jnp.tanh(0.5 * x) + 0.5


def _silu(x):
    return x * _sigmoid(x)


def _dot(a, b):
    return jnp.dot(a, b, preferred_element_type=F32)


def _dot_nt(a, b):
    return lax.dot_general(a, b, (((1,), (1,)), ((), ())), preferred_element_type=F32)


def _split3(x):
    hi = x.astype(BF16)
    r = x - hi.astype(F32)
    mid = r.astype(BF16)
    lo = (r - mid.astype(F32)).astype(BF16)
    return hi, mid, lo


def _sel_dot(sel, x):
    hi, mid, lo = _split3(x)
    return _dot(sel, lo) + _dot(sel, mid) + _dot(sel, hi)


def _sel_dot_nt(sel, x):
    hi, mid, lo = _split3(x)
    return _dot_nt(sel, lo) + _dot_nt(sel, mid) + _dot_nt(sel, hi)


def _dot_sel(x, sel):
    hi, mid, lo = _split3(x)
    return _dot(lo, sel) + _dot(mid, sel) + _dot(hi, sel)


def _norm_mod(x, nw, shift, scale):
    y = x * lax.rsqrt(jnp.mean(x * x, axis=-1, keepdims=True) + EPS) * nw
    return y * (1.0 + scale) + shift


def _mod_kernel(c_ref, w_ref, b_ref, o_ref):
    s = _silu(c_ref[...])
    o_ref[0] = jnp.dot(s, w_ref[0], preferred_element_type=F32, precision=lax.Precision.HIGHEST) + b_ref[0]


def _modulation(c8, mod_w, mod_b):
    tn = 1536
    return pl.pallas_call(
        _mod_kernel,
        grid=(DEPTH, 6 * D_MODEL // tn),
        in_specs=[
            pl.BlockSpec((8, D_MODEL), lambda l, j: (0, 0)),
            pl.BlockSpec((1, D_MODEL, tn), lambda l, j: (l, 0, j)),
            pl.BlockSpec((1, 1, tn), lambda l, j: (l, 0, j)),
        ],
        out_specs=pl.BlockSpec((1, 8, tn), lambda l, j: (l, 0, j)),
        out_shape=jax.ShapeDtypeStruct((DEPTH, 8, 6 * D_MODEL), F32),
        compiler_params=_params("parallel", "parallel"),
        name="modulation",
    )(c8, mod_w, mod_b.reshape(DEPTH, 1, 6 * D_MODEL))


def _inproj_kernel(hl_ref, hc_ref, nw_ref, mod_ref, w_ref, wg_ref, o_ref, g_ref, a_scr):
    @pl.when(pl.program_id(1) == 0)
    def _():
        h = jnp.where(pl.program_id(0) >= N_LAT // TM_PROJ, hc_ref[...], hl_ref[...])
        a = _norm_mod(h, nw_ref[...], mod_ref[0, 0:1, :], mod_ref[0, 1:2, :])
        a_scr[...] = a.astype(BF16)
        g_ref[...] = _dot(a_scr[...], wg_ref[...])

    o_ref[...] = _dot(a_scr[...], w_ref[...])


def _inproj(h_lat, h_ctx, ctx_row0, nw, mod5, w_main, w_gates):
    tiles_per_batch = SEQ // TM_PROJ
    n_lat_tiles = N_LAT // TM_PROJ
    ctx_tile0 = ctx_row0 // TM_PROJ
    return pl.pallas_call(
        _inproj_kernel,
        grid=(N_ALL // TM_PROJ, PROJ_W // TN_PROJ),
        in_specs=[
            pl.BlockSpec((TM_PROJ, D_MODEL), lambda i, j: (jnp.minimum(i, n_lat_tiles - 1), 0)),
            pl.BlockSpec((TM_PROJ, D_MODEL), lambda i, j: (ctx_tile0 + jnp.maximum(i - n_lat_tiles, 0), 0)),
            pl.BlockSpec((1, D_MODEL), lambda i, j: (0, 0)),
            pl.BlockSpec((1, 6, D_MODEL), lambda i, j: (i // tiles_per_batch, 0, 0)),
            pl.BlockSpec((D_MODEL, TN_PROJ), lambda i, j: (0, j)),
            pl.BlockSpec((D_MODEL, LANES), lambda i, j: (0, 0)),
        ],
        out_specs=[pl.BlockSpec((TM_PROJ, TN_PROJ), lambda i, j: (i, j)),
                   pl.BlockSpec((TM_PROJ, LANES), lambda i, j: (i, 0))],
        out_shape=[jax.ShapeDtypeStruct((N_ALL, PROJ_W), F32), jax.ShapeDtypeStruct((N_ALL, LANES), F32)],
        scratch_shapes=[pltpu.VMEM((TM_PROJ, D_MODEL), BF16)],
        compiler_params=_params("parallel", "arbitrary"),
        name="inproj",
    )(h_lat, h_ctx, nw, mod5, w_main, w_gates)


N_DR = 2 * NA_WIN_ROWS - 1


def _na_bias_tables(rpb):
    qc = np.arange(GRID_W)[:, None]
    kc = np.arange(GRID_W)[None, :]
    dc = np.clip(kc - qc + NA_WIN_COLS - 1, 0, 2 * NA_WIN_COLS - 2)
    ws = np.clip(qc - NA_WIN_COLS // 2, 0, GRID_W - NA_WIN_COLS)
    col_ok = (kc >= ws) & (kc < ws + NA_WIN_COLS)
    onehot = ((dc[None] == np.arange(2 * NA_WIN_COLS - 1)[:, None, None]) & col_ok[None]).astype(np.float32)
    t = jnp.einsum('hrd,dqk->hrqk', rpb.astype(F32), jnp.asarray(onehot), precision=lax.Precision.HIGHEST)
    t = t + jnp.asarray(np.where(col_ok, 0.0, NEG_INF).astype(np.float32))
    tp = jnp.pad(t, ((0, 0), (1, 2), (0, 0), (0, 0)))
    return jnp.concatenate([tp[:, :N_DR + 2], tp[:, 1:]], axis=-1)


def _na_kernel(q_ref, k_ref, v_ref, kc_ref, vc_ref, tab_ref, o_ref):
    nq = NA_QROWS * GRID_W
    for sub in range(NA_SUBS):
        rows = slice(sub * nq, (sub + 1) * nq)
        _na_sub_block((pl.program_id(2) * NA_SUBS + sub) * NA_QROWS, q_ref.at[rows], k_ref, v_ref, kc_ref, vc_ref,
                      tab_ref, o_ref.at[rows])


def _na_sub_block(q0, q_ref, k_ref, v_ref, kc_ref, vc_ref, tab_ref, o_ref):
    k0 = jnp.clip(q0 - NA_WIN_ROWS // 2, 0, GRID_H - NA_KROWS)
    start = pl.multiple_of(k0 * GRID_W, GRID_W)
    nk = NA_KROWS * GRID_W
    lane = lax.broadcasted_iota(jnp.int32, (GRID_W, LANES), 1)

    tab_idx, penalty = [], []
    for qr in range(NA_QROWS):
        r = q0 + qr
        r0 = jnp.clip(r - NA_WIN_ROWS // 2, 0, GRID_H - NA_WIN_ROWS)
        idx_row, pen_row = [], []
        for j in range(NA_KROWS // 2):
            kra = k0 + 2 * j
            pa = jnp.where(jnp.logical_and(kra >= r0, kra < r0 + NA_WIN_ROWS), 0.0, NEG_INF)
            pb = jnp.where(jnp.logical_and(kra + 1 >= r0, kra + 1 < r0 + NA_WIN_ROWS), 0.0, NEG_INF)
            idx_row.append(jnp.clip(kra - r + NA_WIN_ROWS, 0, N_DR + 1))
            pen_row.append(jnp.where(lane < GRID_W, pa, pb))
        tab_idx.append(idx_row)
        penalty.append(pen_row)

    outs = []
    for hh in range(2):
        sl = slice(hh * NA_HEAD_DIM, (hh + 1) * NA_HEAD_DIM)
        q = (q_ref[:, sl] * (NA_HEAD_DIM ** -0.5)).astype(BF16)
        k = k_ref[pl.ds(start, nk), sl].astype(BF16)
        v = v_ref[pl.ds(start, nk), sl].astype(BF16)
        kc = kc_ref[:, sl].astype(BF16)
        vc = vc_ref[:, sl].astype(BF16)
        s_raw = _dot_nt(q, k)
        rows = []
        for qr in range(NA_QROWS):
            cols = []
            for j in range(NA_KROWS // 2):
                s_blk = s_raw[qr * GRID_W:(qr + 1) * GRID_W, j * LANES:(j + 1) * LANES]
                cols.append(s_blk + (tab_ref[hh, tab_idx[qr][j]] + penalty[qr][j]))
            rows.append(jnp.concatenate(cols, axis=1))
        s_loc = jnp.concatenate(rows, axis=0)
        s_ctx = _dot_nt(q, kc)
        m = jnp.maximum(jnp.max(s_loc, axis=-1, keepdims=True), jnp.max(s_ctx, axis=-1, keepdims=True))
        p_loc = jnp.exp(s_loc - m)
        p_ctx = jnp.exp(s_ctx - m)
        den = jnp.sum(p_loc, axis=-1, keepdims=True) + jnp.sum(p_ctx, axis=-1, keepdims=True)
        o = _dot(p_loc.astype(BF16), v) + _dot(p_ctx.astype(BF16), vc)
        outs.append(o / den)
    o_ref[...] = jnp.concatenate(outs, axis=-1)


def _na_latent(proj, bias):
    nq = NA_SUBS * NA_QROWS * GRID_W
    nblk = GRID_H // (NA_SUBS * NA_QROWS)
    ctx_blk0 = N_LAT // CTX_LEN
    return pl.pallas_call(
        _na_kernel,
        grid=(BATCH, NA_HEADS // 2, nblk),
        in_specs=[
            pl.BlockSpec((nq, LANES), lambda b, hp, blk: (b * nblk + blk, hp)),
            pl.BlockSpec((SEQ, LANES), lambda b, hp, blk: (b, 4 + hp)),
            pl.BlockSpec((SEQ, LANES), lambda b, hp, blk: (b, 8 + hp)),
            pl.BlockSpec((CTX_LEN, LANES), lambda b, hp, blk: (ctx_blk0 + b, 4 + hp)),
            pl.BlockSpec((CTX_LEN, LANES), lambda b, hp, blk: (ctx_blk0 + b, 8 + hp)),
            pl.BlockSpec((2, N_DR + 2, GRID_W, LANES), lambda b, hp, blk: (hp, 0, 0, 0)),
        ],
        out_specs=pl.BlockSpec((nq, LANES), lambda b, hp, blk: (b * nblk + blk, hp)),
        out_shape=jax.ShapeDtypeStruct((N_LAT, NA_HEADS * NA_HEAD_DIM), F32),
        compiler_params=_params("parallel", "parallel", "arbitrary"),
        name="na_latent",
    )(proj, proj, proj, proj, proj, bias)


def _ctx_attn_kernel(q_ref, k_ref, v_ref, o_ref):
    outs = []
    for hh in range(2):
        sl = slice(hh * NA_HEAD_DIM, (hh + 1) * NA_HEAD_DIM)
        q = (q_ref[:, sl] * (NA_HEAD_DIM ** -0.5)).astype(BF16)
        s = _dot_nt(q, k_ref[:, sl].astype(BF16))
        p = jnp.exp(s - jnp.max(s, axis=-1, keepdims=True))
        o = _dot(p.astype(BF16), v_ref[:, sl].astype(BF16))
        outs.append(o / jnp.sum(p, axis=-1, keepdims=True))
    o_ref[...] = jnp.concatenate(outs, axis=-1)


def _ctx_attention(proj):
    ctx_blk0 = N_LAT // CTX_LEN
    return pl.pallas_call(
        _ctx_attn_kernel,
        grid=(BATCH, NA_HEADS // 2),
        in_specs=[
            pl.BlockSpec((CTX_LEN, LANES), lambda b, hp: (ctx_blk0 + b, hp)),
            pl.BlockSpec((CTX_LEN, LANES), lambda b, hp: (ctx_blk0 + b, 4 + hp)),
            pl.BlockSpec((CTX_LEN, LANES), lambda b, hp: (ctx_blk0 + b, 8 + hp)),
        ],
        out_specs=pl.BlockSpec((CTX_LEN, LANES), lambda b, hp: (b, hp)),
        out_shape=jax.ShapeDtypeStruct((N_CTX, NA_HEADS * NA_HEAD_DIM), F32),
        compiler_params=_params("parallel", "parallel"),
        name="ctx_attention",
    )(proj, proj, proj)


N_CTX_CHUNKS = CTX_LEN // CHUNK
N_LAT_CHUNKS = SEQ // CHUNK
N_STEPS = N_CTX_CHUNKS + N_LAT_CHUNKS
SUB = 128
HG_CHUNK = 2 * SUB
HG_LEVELS = SUB.bit_length() - 1


def _chunk_block(b, s, rev, chunk=CHUNK):
    n_ctx, n_lat = CTX_LEN // chunk, SEQ // chunk
    c_ctx = (n_ctx - 1 - s) if rev else s
    c_lat = (n_lat - 1 - (s - n_ctx)) if rev else (s - n_ctx)
    ctx_blk = N_LAT // chunk + b * n_ctx + c_ctx
    lat_blk = b * n_lat + c_lat
    return jnp.where(s < n_ctx, ctx_blk, lat_blk)


def _tri_consts():
    i = np.arange(CHUNK)
    low = (i[None, :] <= i[:, None]).astype(np.float32)
    eye = np.eye(CHUNK, dtype=np.float32)
    return jnp.asarray(np.stack([low, low.T, eye]), BF16)


def _hgrn_consts():
    t = np.arange(SUB)[:, None]
    s = np.arange(SUB)[None, :]
    low = (s <= t).astype(np.float32)
    masks = np.zeros((2, HG_LEVELS + 1, SUB, SUB), np.float32)
    for l in range(HG_LEVELS):
        pair = ((t ^ s) >> l) == 1
        masks[0, l] = pair & (t > s)
        masks[1, l] = pair & (t < s)
    masks[:, HG_LEVELS] = (t == s)
    return jnp.asarray(np.stack([low, low.T]), BF16), jnp.asarray(masks)


def _hgrn_chain(q_ref, z_ref, v_ref, o_ref, sl, lb, st_ref, tri, masks_ref, rev):
    L, S = HG_CHUNK, SUB
    two = L == 2 * S
    z = z_ref[:, sl]
    v = v_ref[:, sl]
    e = jnp.exp(-jnp.abs(z))
    r = 1.0 / (1.0 + e)
    pos = z >= 0
    sig = jnp.where(pos, r, e * r)
    nsig = jnp.where(pos, e * r, r)
    k = (1.0 - lb) * nsig
    log2f = jnp.log2(jnp.maximum(lb + (1.0 - lb) * sig, F_FLOOR))
    q = _silu(q_ref[:, sl])
    yield

    g = _sel_dot(tri, log2f[:S])
    if two:
        g1 = _sel_dot(tri, log2f[S:])
        if rev:
            g = g + g1[0:1, :]
        else:
            g1 = g1 + g[S - 1:S, :]
        g = jnp.concatenate([g, g1], axis=0)
    row = lax.broadcasted_iota(jnp.int32, (L, HEAD_DIM), 0)

    st = st_ref[...]
    o_inter = _dot_nt((q * jnp.exp2(g)).astype(BF16), st.astype(BF16))
    yield

    nsub = L // S
    a = [jnp.zeros((S, S), F32) for _ in range(nsub)]
    cross = None
    q16 = q.astype(BF16)
    k16 = k.astype(BF16)
    bnd = g
    for l in range(HG_LEVELS + (1 if two else 0)):
        blk = 1 << l
        q_side = ((row & blk) == 0) if rev else ((row & blk) != 0)
        prev_end = pltpu.roll(bnd, (L - blk) if rev else blk, 0)
        w16 = jnp.exp2(jnp.where(q_side, g - prev_end, bnd - g)).astype(BF16)
        qb = q16 * w16
        kb = k16 * w16
        if blk < S:
            for c in range(nsub):
                a[c] = a[c] + masks_ref[l] * _dot_nt(qb[c * S:(c + 1) * S], kb[c * S:(c + 1) * S])
        elif rev:
            cross = _dot_nt(qb[:S], kb[S:])
        else:
            cross = _dot_nt(qb[S:], kb[:S])
        nxt = pltpu.roll(bnd, blk if rev else (L - blk), 0)
        bnd = jnp.where(q_side, bnd, nxt)
        yield
    v16 = v.astype(BF16)
    for c in range(nsub):
        a[c] = a[c] + masks_ref[HG_LEVELS] * _dot_nt(q16[c * S:(c + 1) * S], k16[c * S:(c + 1) * S])
    if not two:
        o_intra = _dot(a[0].astype(BF16), v16)
    elif rev:
        o_intra = jnp.concatenate([_dot(jnp.concatenate([a[0], cross], axis=1).astype(BF16), v16),
                                   _dot(a[1].astype(BF16), v16[S:])], axis=0)
    else:
        o_intra = jnp.concatenate([_dot(a[0].astype(BF16), v16[:S]),
                                   _dot(jnp.concatenate([cross, a[1]], axis=1).astype(BF16), v16)], axis=0)
    o_ref[:, sl] = o_inter + o_intra
    yield

    kd = k * jnp.exp2(bnd - g)
    st_ref[...] = jnp.exp2(bnd[0:1, :]) * st + lax.dot_general(
        v16, kd.astype(BF16), (((0,), (0,)), ((), ())), preferred_element_type=F32)


def _round_robin(chains):
    alive = list(chains)
    while alive:
        still = []
        for c in alive:
            try:
                next(c)
                still.append(c)
            except StopIteration:
                pass
        alive = still


def _hgrn_kernel(qf_ref, zf_ref, vf_ref, qb_ref, zb_ref, vb_ref, lb_ref, tri_ref, masks_ref,
                 of_ref, ob_ref, stf_ref, stb_ref):
    @pl.when(pl.program_id(1) == 0)
    def _():
        stf_ref[...] = jnp.zeros_like(stf_ref)
        stb_ref[...] = jnp.zeros_like(stb_ref)

    chains = []
    for h in range(HG_HEADS):
        sl = slice(h * HEAD_DIM, (h + 1) * HEAD_DIM)
        chains.append(_hgrn_chain(qf_ref, zf_ref, vf_ref, of_ref, sl, lb_ref[0:1, sl], stf_ref.at[h],
                                  tri_ref[0], masks_ref.at[0], False))
        chains.append(_hgrn_chain(qb_ref, zb_ref, vb_ref, ob_ref, sl, lb_ref[1:2, sl], stb_ref.at[h],
                                  tri_ref[1], masks_ref.at[1], True))
    _round_robin(chains)


def _hgrn2(proj, lower, tri, masks):
    width = HG_HEADS * HEAD_DIM

    def spec(col, rev):
        return pl.BlockSpec((HG_CHUNK, width), lambda b, s: (_chunk_block(b, s, rev, HG_CHUNK), col))

    out_spec = lambda rev: pl.BlockSpec((HG_CHUNK, width), lambda b, s: (_chunk_block(b, s, rev, HG_CHUNK), 0))
    shape = jax.ShapeDtypeStruct((N_ALL, width), F32)
    return pl.pallas_call(
        _hgrn_kernel,
        grid=(BATCH, (CTX_LEN + SEQ) // HG_CHUNK),
        in_specs=[
            spec(3, False), spec(4, False), spec(6, False),
            spec(3, True), spec(5, True), spec(6, True),
            pl.BlockSpec((2, width), lambda b, s: (0, 0)),
            pl.BlockSpec((2, SUB, SUB), lambda b, s: (0, 0, 0)),
            pl.BlockSpec((2, HG_LEVELS + 1, SUB, SUB), lambda b, s: (0, 0, 0, 0)),
        ],
        out_specs=[out_spec(False), out_spec(True)],
        out_shape=[shape, shape],
        scratch_shapes=[pltpu.VMEM((HG_HEADS, HEAD_DIM, HEAD_DIM), F32), pltpu.VMEM((HG_HEADS, HEAD_DIM, HEAD_DIM), F32)],
        compiler_params=_params("parallel", "arbitrary"),
        name="hgrn2",
    )(proj, proj, proj, proj, proj, proj, lower, tri, masks)


def _rope_tables():
    n_freq = HEAD_DIM // 4
    inv_freq = ROPE_BASE ** (-np.arange(n_freq, dtype=np.float64) / n_freq)
    t = np.arange(SEQ)
    ang_r = (t // GRID_W).astype(np.float64)[:, None] * inv_freq
    ang_c = (t % GRID_W).astype(np.float64)[:, None] * inv_freq
    cos = np.concatenate([np.cos(ang_r), np.cos(ang_r), np.cos(ang_c), np.cos(ang_c)], axis=-1)
    sin = np.concatenate([-np.sin(ang_r), np.sin(ang_r), -np.sin(ang_c), np.sin(ang_c)], axis=-1)
    cos = np.concatenate([cos, np.ones((CHUNK, HEAD_DIM))], axis=0)
    sin = np.concatenate([sin, np.zeros((CHUNK, HEAD_DIM))], axis=0)
    return jnp.asarray(cos, F32), jnp.asarray(sin, F32)


def _rope_swap():
    l = np.arange(HEAD_DIM)
    return jnp.asarray((l[:, None] == (l[None, :] ^ 32)).astype(np.float32), BF16)


def _rope(x, cos, sin, swap):
    hi = x.astype(BF16)
    mid = (x - hi.astype(F32)).astype(BF16)
    partner = _dot(hi, swap) + _dot(mid, swap)
    return x * cos + partner * sin


def _mlstm_gates(g_ref, gbias, tri_ref, rev):
    L = CHUNK
    log_i = g_ref[0, 0] + gbias[0][:, 0:1]
    xf = g_ref[0, 1] + gbias[1][:, 0:1]
    log_f = jnp.minimum(xf, 0.0) - jnp.log(1.0 + jnp.exp(-jnp.abs(xf)))
    r8 = jnp.concatenate([log_f, log_i], axis=0)
    low, up, eye = tri_ref[0], tri_ref[1], tri_ref[2]
    b_row = _dot_sel(r8, low if rev else up)[0:ML_HEADS, :]
    b_col = _sel_dot_nt(up if rev else low, r8)[:, 0:ML_HEADS]
    i_col = _sel_dot_nt(eye, r8)[:, ML_HEADS:2 * ML_HEADS]
    return log_i, b_row, b_col, i_col


def _mlstm_kernel(qf_ref, kf_ref, vf_ref, gf_ref, cf_ref, sf_ref,
                  qb_ref, kb_ref, vb_ref, gb_ref, cb_ref, sb_ref,
                  gbias_ref, tri_ref, swap_ref, of_ref, ob_ref, cnf_ref, cnb_ref, mf_ref, mb_ref):
    @pl.when(pl.program_id(1) == 0)
    def _():
        cnf_ref[...] = jnp.zeros_like(cnf_ref)
        cnb_ref[...] = jnp.zeros_like(cnb_ref)
        mf_ref[...] = jnp.zeros_like(mf_ref)
        mb_ref[...] = jnp.zeros_like(mb_ref)

    L = CHUNK
    ti = lax.broadcasted_iota(jnp.int32, (L, L), 0)
    si = lax.broadcasted_iota(jnp.int32, (L, L), 1)
    ones = jnp.ones((L, HEAD_DIM), BF16)
    swap = swap_ref[...]
    dirs = ((qf_ref, kf_ref, vf_ref, gf_ref, cf_ref, sf_ref, of_ref, cnf_ref, mf_ref, False),
            (qb_ref, kb_ref, vb_ref, gb_ref, cb_ref, sb_ref, ob_ref, cnb_ref, mb_ref, True))

    chains = []
    for d, (q_ref, k_ref, v_ref, g_ref, c_ref, s_ref, o_ref, cn_ref, m_ref, rev) in enumerate(dirs):
        log_i, b_row, b_col, i_col = _mlstm_gates(g_ref, gbias_ref[d], tri_ref, rev)
        for h in range(ML_HEADS):
            bc = jnp.broadcast_to(b_col[:, h:h + 1], (L, HEAD_DIM))
            ic = jnp.broadcast_to(i_col[:, h:h + 1], (L, HEAD_DIM))
            chains.append(dict(
                sl=slice(h * HEAD_DIM, (h + 1) * HEAD_DIM), rev=rev, q_ref=q_ref, k_ref=k_ref, v_ref=v_ref,
                c_ref=c_ref, s_ref=s_ref, o_ref=o_ref, cn_ref=cn_ref.at[h], m_ref=m_ref.at[h],
                log_i=log_i[h:h + 1, :], b_row=b_row[h:h + 1, :], b_col=bc, i_col=ic))

    for c in chains:
        cos, sin = c['c_ref'][...], c['s_ref'][...]
        c['qc'] = _rope(c['q_ref'][:, c['sl']], cos, sin, swap).astype(BF16)
        kc = _rope(c['k_ref'][:, c['sl']] * (HEAD_DIM ** -0.5), cos, sin, swap)
        c['kc'] = kc
        c['s'] = _dot_nt(c['qc'], kc.astype(BF16))
    for c in chains:
        b_row = c['b_row']
        c['b_end'] = b_row[:, 0:1] if c['rev'] else b_row[:, L - 1:L]
        tri = (si >= ti) if c['rev'] else (si <= ti)
        c['m_prev'] = c['m_ref'][0:1, 0:1]
        bc2 = jnp.concatenate([c['b_col'], c['b_col']], axis=-1)
        dmat = jnp.where(tri, bc2 + (c['log_i'] - b_row), NEG_INF)
        inter = c['b_col'] + c['m_prev']
        m_t = jnp.maximum(inter, jnp.max(dmat, axis=-1, keepdims=True))
        c['m_t'] = m_t
        c['w_inter'] = jnp.exp(inter - m_t)
        c['p'] = (jnp.exp(dmat - jnp.concatenate([m_t, m_t], axis=-1)) * c['s']).astype(BF16)
    for c in chains:
        c['v_ext'] = jnp.concatenate([c['v_ref'][:, c['sl']].astype(BF16), ones], axis=-1)
        c['cn'] = c['cn_ref'][...]
        w2 = jnp.concatenate([c['w_inter'], c['w_inter']], axis=-1)
        acc = _dot(c['p'], c['v_ext']) + w2 * _dot(c['qc'], c['cn'].astype(BF16))
        den = acc[:, HEAD_DIM:]
        c['o_ref'][:, c['sl']] = acc[:, :HEAD_DIM] / jnp.maximum(jnp.abs(den), jnp.exp(-c['m_t']))
    for c in chains:
        e_row = c['b_end'] + (c['log_i'] - c['b_row'])
        m_new = jnp.maximum(c['b_end'] + c['m_prev'], jnp.max(e_row, axis=-1, keepdims=True))
        w_old = jnp.exp(c['b_end'] + c['m_prev'] - m_new)
        w_s = jnp.exp(c['b_end'] - c['b_col'] + c['i_col'] - m_new)
        c['cn_ref'][...] = w_old * c['cn'] + lax.dot_general(
            (w_s * c['kc']).astype(BF16), c['v_ext'], (((0,), (0,)), ((), ())), preferred_element_type=F32)
        c['m_ref'][...] = jnp.broadcast_to(m_new, c['m_ref'].shape)


def _mlstm(proj, gates_t, gbias, cos, sin, tri):
    width = ML_HEADS * HEAD_DIM

    def spec(col, rev):
        return pl.BlockSpec((CHUNK, width), lambda b, s: (_chunk_block(b, s, rev), col))

    def gate_spec(rev):
        d = 1 if rev else 0
        return pl.BlockSpec((1, 2, ML_HEADS, CHUNK), lambda b, s: (d, 0, 0, _chunk_block(b, s, rev)))

    def rope_spec(rev):
        def idx(b, s):
            lat = _chunk_block(b, s, rev) - b * N_LAT_CHUNKS
            return (jnp.where(s < N_CTX_CHUNKS, N_LAT_CHUNKS, lat), 0)
        return pl.BlockSpec((CHUNK, LANES), idx)

    out_spec = lambda rev: pl.BlockSpec((CHUNK, width), lambda b, s: (_chunk_block(b, s, rev), 0))
    shape = jax.ShapeDtypeStruct((N_ALL, width), F32)
    per_dir = lambda rev: [spec(8, rev), spec(9, rev), spec(10, rev), gate_spec(rev), rope_spec(rev), rope_spec(rev)]
    args_dir = [proj, proj, proj, gates_t, cos, sin]
    return pl.pallas_call(
        _mlstm_kernel,
        grid=(BATCH, N_STEPS),
        in_specs=per_dir(False) + per_dir(True) + [
            pl.BlockSpec((2, 2, ML_HEADS, LANES), lambda b, s: (0, 0, 0, 0)),
            pl.BlockSpec((3, CHUNK, CHUNK), lambda b, s: (0, 0, 0)),
            pl.BlockSpec((HEAD_DIM, HEAD_DIM), lambda b, s: (0, 0)),
        ],
        out_specs=[out_spec(False), out_spec(True)],
        out_shape=[shape, shape],
        scratch_shapes=[pltpu.VMEM((ML_HEADS, HEAD_DIM, 2 * HEAD_DIM), F32),
                        pltpu.VMEM((ML_HEADS, HEAD_DIM, 2 * HEAD_DIM), F32),
                        pltpu.VMEM((ML_HEADS, 8, LANES), F32), pltpu.VMEM((ML_HEADS, 8, LANES), F32)],
        compiler_params=_params("parallel", "arbitrary"),
        name="mlstm",
    )(*args_dir, *args_dir, gbias, tri, _rope_swap())


def _head_rms(x, w):
    parts = []
    for hh in range(x.shape[-1] // HEAD_DIM):
        xs = x[:, hh * HEAD_DIM:(hh + 1) * HEAD_DIM]
        parts.append(xs * lax.rsqrt(jnp.mean(xs * xs, axis=-1, keepdims=True) + EPS))
    return jnp.concatenate(parts, axis=-1) * w


def _merge_kernel(na_ref, nac_ref, hgf_ref, hgb_ref, mlf_ref, mlb_ref, hgg_ref, mlo_ref, bg0_ref, bg1_ref, bg2_ref,
                  hl_ref, hc_ref, mod_ref, hgw_ref, mlw_ref, wb_ref, wo_ref, o_ref):
    hg = _head_rms(hgf_ref[...] + hgb_ref[...], hgw_ref[...]) * _silu(hgg_ref[...])
    ml = _sigmoid(mlo_ref[...]) * _head_rms(mlf_ref[...] + mlb_ref[...], mlw_ref[...])
    is_ctx = pl.program_id(0) >= N_LAT // TM_MERGE
    na = jnp.where(is_ctx, nac_ref[...], na_ref[...])
    y2 = None
    for bg_ref, branch, i in ((bg0_ref, na, 0), (bg1_ref, hg, 1), (bg2_ref, ml, 2)):
        p = _dot(branch.astype(BF16), wb_ref[i])
        term = jnp.tanh(0.5 * bg_ref[...]) * p + p
        y2 = term if y2 is None else y2 + term
    h = jnp.where(is_ctx, hc_ref[...], hl_ref[...])
    o_ref[...] = h + (0.5 * mod_ref[0, 2:3, :]) * _dot(y2.astype(BF16), wo_ref[...])


def _merge(n_rows, na, na_ctx, hgf, hgb, mlf, mlb, proj, h_lat, h_ctx, ctx_row0, mod5, hg_w, ml_w, w_branch, w_out):
    tm = TM_MERGE
    tiles_per_batch = SEQ // tm
    n_lat_tiles = N_LAT // tm
    ctx_tile0 = ctx_row0 // tm
    row = lambda w, c: pl.BlockSpec((tm, w), lambda i: (i, c))
    const = lambda shape: pl.BlockSpec(shape, lambda i: (0,) * len(shape))
    bg0 = BG_OFF // D_MODEL
    return pl.pallas_call(
        _merge_kernel,
        grid=(n_rows // tm,),
        in_specs=[
            pl.BlockSpec((tm, 512), lambda i: (jnp.minimum(i, n_lat_tiles - 1), 0)),
            pl.BlockSpec((tm, 512), lambda i: (jnp.maximum(i - n_lat_tiles, 0), 0)),
            row(512, 0), row(512, 0), row(512, 0), row(512, 0),
            row(512, 7), row(512, 11),
            row(D_MODEL, bg0), row(D_MODEL, bg0 + 1), row(D_MODEL, bg0 + 2),
            pl.BlockSpec((tm, D_MODEL), lambda i: (jnp.minimum(i, n_lat_tiles - 1), 0)),
            pl.BlockSpec((tm, D_MODEL), lambda i: (ctx_tile0 + jnp.maximum(i - n_lat_tiles, 0), 0)),
            pl.BlockSpec((1, 6, D_MODEL), lambda i: (jnp.minimum(i // tiles_per_batch, BATCH), 0, 0)),
            const((1, 512)), const((1, 512)),
            const((3, BRANCH_WIDTH, D_MODEL)), const((D_MODEL, D_MODEL)),
        ],
        out_specs=row(D_MODEL, 0),
        out_shape=jax.ShapeDtypeStruct((n_rows, D_MODEL), F32),
        compiler_params=_params("parallel"),
        name="merge",
    )(na, na_ctx, hgf, hgb, mlf, mlb, proj, proj, proj, proj, proj, h_lat, h_ctx, mod5, hg_w, ml_w, w_branch, w_out)


def _ffn_kernel(h_ref, nw_ref, mod_ref, wa_ref, wu_ref, wd_ref, o_ref):
    h = h_ref[...]
    f = _norm_mod(h, nw_ref[...], mod_ref[0, 3:4, :], mod_ref[0, 4:5, :]).astype(BF16)
    g = _silu(_dot(f, wa_ref[...])) * _dot(f, wu_ref[...])
    o_ref[...] = h + mod_ref[0, 5:6, :] * _dot(g.astype(BF16), wd_ref[...])


def _ffn(h_all, nw, mod5, w_up, w_down):
    n_rows = h_all.shape[0]
    tiles_per_batch = SEQ // TM_FFN
    resident = pl.Buffered(1)
    return pl.pallas_call(
        _ffn_kernel,
        grid=(n_rows // TM_FFN,),
        in_specs=[
            pl.BlockSpec((TM_FFN, D_MODEL), lambda i: (i, 0)),
            pl.BlockSpec((1, D_MODEL), lambda i: (0, 0)),
            pl.BlockSpec((1, 6, D_MODEL), lambda i: (i // tiles_per_batch, 0, 0)),
            pl.BlockSpec((D_MODEL, FFN_DIM), lambda i: (0, 0), pipeline_mode=resident),
            pl.BlockSpec((D_MODEL, FFN_DIM), lambda i: (0, 1), pipeline_mode=resident),
            pl.BlockSpec((FFN_DIM, D_MODEL), lambda i: (0, 0), pipeline_mode=resident),
        ],
        out_specs=pl.BlockSpec((TM_FFN, D_MODEL), lambda i: (i, 0)),
        out_shape=jax.ShapeDtypeStruct((n_rows, D_MODEL), F32),
        compiler_params=_params("parallel"),
        name="ffn",
    )(h_all, nw, mod5, w_up, w_up, w_down)


def _router_kernel(h_ref, nw_ref, mod_ref, wr_ref, f_ref, r_ref):
    f = _norm_mod(h_ref[...], nw_ref[...], mod_ref[0, 3:4, :], mod_ref[0, 4:5, :])
    f_ref[...] = f
    logits = jnp.dot(f, wr_ref[...], preferred_element_type=F32, precision=lax.Precision.HIGHEST)
    lane = lax.broadcasted_iota(jnp.int32, logits.shape, 1)
    logits = jnp.where(lane < N_EXPERTS, logits, -jnp.inf)
    m1 = jnp.max(logits, axis=-1, keepdims=True)
    i1 = jnp.min(jnp.where(logits == m1, lane, LANES), axis=-1, keepdims=True)
    rest = jnp.where(lane == i1, -jnp.inf, logits)
    m2 = jnp.max(rest, axis=-1, keepdims=True)
    i2 = jnp.min(jnp.where(rest == m2, lane, LANES), axis=-1, keepdims=True)
    e2 = jnp.exp(m2 - m1)
    w1 = 1.0 / (1.0 + e2)
    w2 = e2 / (1.0 + e2)
    r_ref[...] = jnp.where(lane == 0, i1.astype(F32),
                           jnp.where(lane == 1, i2.astype(F32),
                                     jnp.where(lane == 2, w1, jnp.where(lane == 3, w2, 0.0))))


def _router(h_lat, nw, mod5, w_router_pad):
    tm = 512
    tiles_per_batch = SEQ // tm
    return pl.pallas_call(
        _router_kernel,
        grid=(N_LAT // tm,),
        in_specs=[
            pl.BlockSpec((tm, D_MODEL), lambda i: (i, 0)),
            pl.BlockSpec((1, D_MODEL), lambda i: (0, 0)),
            pl.BlockSpec((1, 6, D_MODEL), lambda i: (i // tiles_per_batch, 0, 0)),
            pl.BlockSpec((D_MODEL, LANES), lambda i: (0, 0)),
        ],
        out_specs=[pl.BlockSpec((tm, D_MODEL), lambda i: (i, 0)), pl.BlockSpec((tm, LANES), lambda i: (i, 0))],
        out_shape=[jax.ShapeDtypeStruct((N_LAT, D_MODEL), F32), jax.ShapeDtypeStruct((N_LAT, LANES), F32)],
        compiler_params=_params("parallel"),
        name="router",
    )(h_lat, nw, mod5, w_router_pad)


def _moe_kernel(be_ref, nused_ref, nvalid_ref, code_ref, f_hbm, wa_ref, wu_ref, wd_ref, y_hbm,
                xbuf, x16, acc, ybuf, sem_in, sem_out):
    i = pl.program_id(0)
    j = pl.program_id(1)
    last_j = pl.num_programs(1) - 1
    n_used = nused_ref[0]
    active = i < n_used
    slot = i % 2

    def start_gather(blk, buf):
        def body(r, c):
            tok = jnp.maximum(code_ref[blk * TM_MOE + r], 0) >> 1
            pltpu.make_async_copy(f_hbm.at[pl.ds(tok, 1)], xbuf.at[buf, pl.ds(r, 1)],
                                  sem_in.at[buf]).start(priority=GATHER_DMA_PRIORITY)
            return c
        lax.fori_loop(0, TM_MOE, body, 0, unroll=8)

    def wait_gather(buf):
        pltpu.make_async_copy(f_hbm.at[pl.ds(0, TM_MOE)], xbuf.at[buf], sem_in.at[buf]).wait()

    def start_scatter(blk):
        def body(r, c):
            code = code_ref[blk * TM_MOE + r]
            dst = (code & 1) * N_LAT + (code >> 1)
            pltpu.make_async_copy(ybuf.at[pl.ds(r, 1)], y_hbm.at[pl.ds(dst, 1)], sem_out).start()
            return c
        lax.fori_loop(0, nvalid_ref[blk], body, 0)

    def wait_scatter(blk):
        n = nvalid_ref[blk]
        p = TM_MOE
        while p >= 8:
            @pl.when((n & p) != 0)
            def _(p=p):
                pltpu.make_async_copy(ybuf.at[pl.ds(0, p)], y_hbm.at[pl.ds(0, p)], sem_out).wait()
            p //= 2

        def one(r, c):
            pltpu.make_async_copy(ybuf.at[pl.ds(0, 1)], y_hbm.at[pl.ds(0, 1)], sem_out).wait()
            return c
        lax.fori_loop(0, n & 7, one, 0)

    def prefetch_rows(first, count):
        for r in range(first, first + count):
            tok = jnp.maximum(code_ref[(i + 1) * TM_MOE + r], 0) >> 1
            pltpu.make_async_copy(f_hbm.at[pl.ds(tok, 1)], xbuf.at[1 - slot, pl.ds(r, 1)],
                                  sem_in.at[1 - slot]).start(priority=GATHER_DMA_PRIORITY)

    def partial_out():
        x = x16[...]
        g = _silu(_dot(x, wa_ref[0])) * _dot(x, wu_ref[0])
        return _dot(g.astype(BF16), wd_ref[0])

    @pl.when(jnp.logical_and(active, j == 0))
    def _():
        @pl.when(i == 0)
        def _():
            start_gather(0, 0)

        wait_gather(slot)
        x16[...] = xbuf[slot].astype(BF16)

    @pl.when(jnp.logical_and(active, j == 0))
    def _():
        prefetch_rows(0, TM_MOE // 2)
        acc[...] = partial_out()

    @pl.when(jnp.logical_and(active, j == last_j))
    def _():
        @pl.when(i > 0)
        def _():
            wait_scatter(i - 1)

    @pl.when(jnp.logical_and(active, j == last_j))
    def _():
        prefetch_rows(TM_MOE // 2, TM_MOE // 2)
        ybuf[...] = acc[...] + partial_out()

    @pl.when(jnp.logical_and(active, j == last_j))
    def _():
        start_scatter(i)

        @pl.when(i == n_used - 1)
        def _():
            wait_scatter(i)
            wait_gather(1 - slot)


def _moe_experts(block_e, n_used, n_valid, codes, f_lat, w_up, w_down):
    nj = EXPERT_DIM // TH_MOE
    assert nj == 2 and nj * TH_MOE == EXPERT_DIM

    def jj(i, j, nu):
        return jnp.where(i < nu[0], j, nj - 1)

    grid_spec = pltpu.PrefetchScalarGridSpec(
        num_scalar_prefetch=4,
        grid=(N_MOE_BLOCKS, nj),
        in_specs=[
            pl.BlockSpec(memory_space=pl.ANY),
            pl.BlockSpec((1, D_MODEL, TH_MOE), lambda i, j, be, nu, nv, cd: (be[i], 0, jj(i, j, nu))),
            pl.BlockSpec((1, D_MODEL, TH_MOE), lambda i, j, be, nu, nv, cd: (be[i], 0, nj + jj(i, j, nu))),
            pl.BlockSpec((1, TH_MOE, D_MODEL), lambda i, j, be, nu, nv, cd: (be[i], jj(i, j, nu), 0)),
        ],
        out_specs=pl.BlockSpec(memory_space=pl.ANY),
        scratch_shapes=[
            pltpu.VMEM((2, TM_MOE, D_MODEL), F32), pltpu.VMEM((TM_MOE, D_MODEL), BF16),
            pltpu.VMEM((TM_MOE, D_MODEL), F32), pltpu.VMEM((TM_MOE, D_MODEL), F32),
            pltpu.SemaphoreType.DMA((2,)), pltpu.SemaphoreType.DMA(()),
        ],
    )
    return pl.pallas_call(
        _moe_kernel,
        grid_spec=grid_spec,
        out_shape=jax.ShapeDtypeStruct((2 * N_LAT, D_MODEL), F32),
        compiler_params=_params("arbitrary", "arbitrary"),
        name="moe_experts",
    )(block_e, n_used, n_valid, codes, f_lat, w_up, w_up, w_down)


def _combine_kernel(h_ref, y1_ref, y2_ref, r_ref, mod_ref, fw_ref, o_ref):
    r = r_ref[...]
    y = r[:, 2:3] * y1_ref[...] + r[:, 3:4] * y2_ref[...]
    h = h_ref[...] + mod_ref[0, 5:6, :] * y
    o_ref[...] = h * lax.rsqrt(jnp.mean(h * h, axis=-1, keepdims=True) + EPS) * fw_ref[...]


def _combine_final(h_lat, y, route, mod5, final_w):
    tm = 512
    tiles_per_batch = SEQ // tm
    return pl.pallas_call(
        _combine_kernel,
        grid=(N_LAT // tm,),
        in_specs=[
            pl.BlockSpec((tm, D_MODEL), lambda i: (i, 0)),
            pl.BlockSpec((tm, D_MODEL), lambda i: (i, 0)),
            pl.BlockSpec((tm, D_MODEL), lambda i: (N_LAT // tm + i, 0)),
            pl.BlockSpec((tm, LANES), lambda i: (i, 0)),
            pl.BlockSpec((1, 6, D_MODEL), lambda i: (i // tiles_per_batch, 0, 0)),
            pl.BlockSpec((1, D_MODEL), lambda i: (0, 0)),
        ],
        out_specs=pl.BlockSpec((tm, D_MODEL), lambda i: (i, 0)),
        out_shape=jax.ShapeDtypeStruct((N_LAT, D_MODEL), F32),
        compiler_params=_params("parallel"),
        name="combine_final",
    )(h_lat, y, y, route, mod5, final_w)


def _moe_plan(route):
    e12 = route[:, 0:2].astype(jnp.int32)
    onehot = (e12[:, :, None] == jnp.arange(N_EXPERTS, dtype=jnp.int32)).astype(jnp.int32).sum(axis=1)
    before = jnp.cumsum(onehot, axis=0) - onehot
    counts = jnp.sum(onehot, axis=0)
    nblk = (counts + TM_MOE - 1) // TM_MOE
    blk_end = jnp.cumsum(nblk)
    slot0 = (blk_end - nblk) * TM_MOE
    rank = jnp.take_along_axis(before, e12, axis=1)
    dest = slot0[e12] + rank
    codes = jnp.full((N_SLOTS,), -1, jnp.int32).at[dest.reshape(-1)].set(jnp.arange(2 * N_LAT, dtype=jnp.int32))
    n_used = blk_end[-1]
    blocks = jnp.minimum(jnp.arange(N_MOE_BLOCKS, dtype=jnp.int32), n_used - 1)
    block_e = jnp.minimum(jnp.sum((blocks[:, None] >= blk_end[None, :]).astype(jnp.int32), axis=1), N_EXPERTS - 1)
    n_valid = jnp.sum((codes >= 0).astype(jnp.int32).reshape(N_MOE_BLOCKS, TM_MOE), axis=1)
    return block_e, n_used.reshape(1).astype(jnp.int32), n_valid, codes


def kernel(x, c, ctx, c_ctx, mod_w, mod_b, norm1_w, w_in, na_rpb, hg_lb, hg_norm_w, ml_gate_b, ml_norm_w,
           w_branch, w_out, norm2_w, ffn_w_up, ffn_w_down, moe_router, moe_w_up, moe_w_down, final_norm_w):
    h_lat, h_ctx, ctx_row0 = x.reshape(N_LAT, D_MODEL), ctx.reshape(N_CTX, D_MODEL), 0
    c8 = jnp.concatenate([c, c_ctx[None, :], jnp.zeros((3, D_MODEL), F32)], axis=0)
    mods = _modulation(c8, mod_w, mod_b).reshape(DEPTH, 8, 6, D_MODEL)

    lb_p = jax.nn.softmax(hg_lb.astype(F32), axis=0)
    hg_lower = jnp.cumsum(lb_p, axis=0) - lb_p[0]
    tri = _tri_consts()
    hg_tri, hg_masks = _hgrn_consts()
    cos, sin = _rope_tables()

    out = None
    for layer in range(DEPTH):
        last = layer == DEPTH - 1
        mod5 = mods[layer, :5]
        wl = w_in[layer]
        w_main = jnp.concatenate([wl[:, :MAIN_W], wl[:, MAIN_W + ML_GATE_COLS:]], axis=1).astype(BF16)
        w_gates = jnp.pad(wl[:, MAIN_W:MAIN_W + ML_GATE_COLS], ((0, 0), (0, LANES - ML_GATE_COLS))).astype(BF16)
        proj, gates = _inproj(h_lat, h_ctx, ctx_row0, norm1_w[layer][None, :], mod5, w_main, w_gates)

        na = _na_latent(proj, _na_bias_tables(na_rpb[layer]))
        na_ctx = na if last else _ctx_attention(proj)
        hgf, hgb = _hgrn2(proj, hg_lower[layer], hg_tri, hg_masks)
        gates_t = gates[:, :ML_GATE_COLS].T.reshape(2, 2, ML_HEADS, N_ALL)
        gbias = jnp.broadcast_to(ml_gate_b[layer][..., None], (2, 2, ML_HEADS, LANES))
        mlf, mlb = _mlstm(proj, gates_t, gbias, cos, sin, tri)

        n_rows = N_LAT if last else N_ALL
        h_all = _merge(n_rows, na, na_ctx, hgf, hgb, mlf, mlb, proj, h_lat, h_ctx, ctx_row0, mod5,
                       hg_norm_w[layer][None, :], ml_norm_w[layer][None, :],
                       w_branch[layer].astype(BF16), w_out[layer].astype(BF16))
        i = layer // 2
        if layer % 2 == 0:
            h_all = _ffn(h_all, norm2_w[layer][None, :], mod5, ffn_w_up[i].astype(BF16), ffn_w_down[i].astype(BF16))
            h_lat, h_ctx, ctx_row0 = h_all, h_all, N_LAT
            if last:
                raise NotImplementedError("final norm after a dense last layer")
        else:
            if not last:
                raise NotImplementedError("MoE on the context stream")
            w_router_pad = jnp.pad(moe_router[i], ((0, 0), (0, LANES - N_EXPERTS)))
            f_lat, route = _router(h_all, norm2_w[layer][None, :], mod5, w_router_pad)
            block_e, n_used, n_valid, codes = _moe_plan(route)
            y = _moe_experts(block_e, n_used, n_valid, codes, f_lat,
                             moe_w_up[i].astype(BF16), moe_w_down[i].astype(BF16))
            out = _combine_final(h_all, y, route, mod5, final_norm_w[None, :])
    return out.reshape(BATCH, SEQ, D_MODEL)
```

```python
import functools

import numpy as np
import jax
import jax.numpy as jnp
from jax import lax
from jax.experimental import pallas as pl
from jax.experimental.pallas import tpu as pltpu

F32 = jnp.float32
BF16 = jnp.bfloat16

D_MODEL = 1024
BATCH = 4
SEQ = 4096
DEPTH = 2
GRID_W = 64
GRID_H = SEQ // GRID_W
CTX_LEN = 256
EPS = 1e-6
NEG_INF = -1e30
F_FLOOR = 1e-30
NA_HEADS = 8
NA_HEAD_DIM = 64
NA_WIN_ROWS = 8
NA_WIN_COLS = 16
HG_HEADS = 4
ML_HEADS = 4
HEAD_DIM = 128
ML_GATE_COLS = 16
ROPE_BASE = 10000.0
BRANCH_WIDTH = 512
FFN_DIM = 2816
N_EXPERTS = 8
EXPERT_DIM = 3584

N_LAT = BATCH * SEQ
N_CTX = BATCH * CTX_LEN
N_ALL = N_LAT + N_CTX

LANES = 128
VMEM_LIMIT = 56 * 1024 * 1024

MAIN_W = 12 * 512
BG_OFF = MAIN_W
PROJ_W = MAIN_W + 3 * D_MODEL

TM_PROJ = 1024
TN_PROJ = PROJ_W // 4
TM_MERGE = 256
TM_FFN = 512
NA_SUBS = 2
NA_QROWS = 4
NA_KROWS = NA_QROWS + NA_WIN_ROWS
CHUNK = 256
TM_MOE = 512
TH_MOE = EXPERT_DIM // 2
GATHER_DMA_PRIORITY = 1
N_MOE_BLOCKS = -(-(2 * N_LAT + N_EXPERTS * (TM_MOE - 1)) // TM_MOE)
N_SLOTS = N_MOE_BLOCKS * TM_MOE


def _params(*sem):
    return pltpu.CompilerParams(dimension_semantics=sem, vmem_limit_bytes=VMEM_LIMIT)


def _sigmoid(x):
    return 0.5 * jnp.tanh(0.5 * x) + 0.5


def _silu(x):
    return x * _sigmoid(x)


def _dot(a, b):
    return jnp.dot(a, b, preferred_element_type=F32)


def _dot_nt(a, b):
    return lax.dot_general(a, b, (((1,), (1,)), ((), ())), preferred_element_type=F32)


def _split3(x):
    hi = x.astype(BF16)
    r = x - hi.astype(F32)
    mid = r.astype(BF16)
    lo = (r - mid.astype(F32)).astype(BF16)
    return hi, mid, lo


def _sel_dot(sel, x):
    hi, mid, lo = _split3(x)
    return _dot(sel, lo) + _dot(sel, mid) + _dot(sel, hi)


def _sel_dot_nt(sel, x):
    hi, mid, lo = _split3(x)
    return _dot_nt(sel, lo) + _dot_nt(sel, mid) + _dot_nt(sel, hi)


def _dot_sel(x, sel):
    hi, mid, lo = _split3(x)
    return _dot(lo, sel) + _dot(mid, sel) + _dot(hi, sel)


def _norm_mod(x, nw, shift, scale):
    y = x * lax.rsqrt(jnp.mean(x * x, axis=-1, keepdims=True) + EPS) * nw
    return y * (1.0 + scale) + shift


def _mod_kernel(c_ref, w_ref, b_ref, o_ref):
    s = _silu(c_ref[...])
    o_ref[0] = jnp.dot(s, w_ref[0], preferred_element_type=F32, precision=lax.Precision.HIGHEST) + b_ref[0]


def _modulation(c8, mod_w, mod_b):
    tn = 1536
    return pl.pallas_call(
        _mod_kernel,
        grid=(DEPTH, 6 * D_MODEL // tn),
        in_specs=[
            pl.BlockSpec((8, D_MODEL), lambda l, j: (0, 0)),
            pl.BlockSpec((1, D_MODEL, tn), lambda l, j: (l, 0, j)),
            pl.BlockSpec((1, 1, tn), lambda l, j: (l, 0, j)),
        ],
        out_specs=pl.BlockSpec((1, 8, tn), lambda l, j: (l, 0, j)),
        out_shape=jax.ShapeDtypeStruct((DEPTH, 8, 6 * D_MODEL), F32),
        compiler_params=_params("parallel", "parallel"),
        name="modulation",
    )(c8, mod_w, mod_b.reshape(DEPTH, 1, 6 * D_MODEL))


def _inproj_kernel(hl_ref, hc_ref, nw_ref, mod_ref, w_ref, wg_ref, o_ref, g_ref, a_scr):
    @pl.when(pl.program_id(1) == 0)
    def _():
        h = jnp.where(pl.program_id(0) >= N_LAT // TM_PROJ, hc_ref[...], hl_ref[...])
        a = _norm_mod(h, nw_ref[...], mod_ref[0, 0:1, :], mod_ref[0, 1:2, :])
        a_scr[...] = a.astype(BF16)
        g_ref[...] = _dot(a_scr[...], wg_ref[...])

    o_ref[...] = _dot(a_scr[...], w_ref[...])


def _inproj(h_lat, h_ctx, ctx_row0, nw, mod5, w_main, w_gates):
    tiles_per_batch = SEQ // TM_PROJ
    n_lat_tiles = N_LAT // TM_PROJ
    ctx_tile0 = ctx_row0 // TM_PROJ
    return pl.pallas_call(
        _inproj_kernel,
        grid=(N_ALL // TM_PROJ, PROJ_W // TN_PROJ),
        in_specs=[
            pl.BlockSpec((TM_PROJ, D_MODEL), lambda i, j: (jnp.minimum(i, n_lat_tiles - 1), 0)),
            pl.BlockSpec((TM_PROJ, D_MODEL), lambda i, j: (ctx_tile0 + jnp.maximum(i - n_lat_tiles, 0), 0)),
            pl.BlockSpec((1, D_MODEL), lambda i, j: (0, 0)),
            pl.BlockSpec((1, 6, D_MODEL), lambda i, j: (i // tiles_per_batch, 0, 0)),
            pl.BlockSpec((D_MODEL, TN_PROJ), lambda i, j: (0, j)),
            pl.BlockSpec((D_MODEL, LANES), lambda i, j: (0, 0)),
        ],
        out_specs=[pl.BlockSpec((TM_PROJ, TN_PROJ), lambda i, j: (i, j)),
                   pl.BlockSpec((TM_PROJ, LANES), lambda i, j: (i, 0))],
        out_shape=[jax.ShapeDtypeStruct((N_ALL, PROJ_W), F32), jax.ShapeDtypeStruct((N_ALL, LANES), F32)],
        scratch_shapes=[pltpu.VMEM((TM_PROJ, D_MODEL), BF16)],
        compiler_params=_params("parallel", "arbitrary"),
        name="inproj",
    )(h_lat, h_ctx, nw, mod5, w_main, w_gates)


N_DR = 2 * NA_WIN_ROWS - 1


def _na_bias_tables(rpb):
    qc = np.arange(GRID_W)[:, None]
    kc = np.arange(GRID_W)[None, :]
    dc = np.clip(kc - qc + NA_WIN_COLS - 1, 0, 2 * NA_WIN_COLS - 2)
    ws = np.clip(qc - NA_WIN_COLS // 2, 0, GRID_W - NA_WIN_COLS)
    col_ok = (kc >= ws) & (kc < ws + NA_WIN_COLS)
    onehot = ((dc[None] == np.arange(2 * NA_WIN_COLS - 1)[:, None, None]) & col_ok[None]).astype(np.float32)
    t = jnp.einsum('hrd,dqk->hrqk', rpb.astype(F32), jnp.asarray(onehot), precision=lax.Precision.HIGHEST)
    t = t + jnp.asarray(np.where(col_ok, 0.0, NEG_INF).astype(np.float32))
    tp = jnp.pad(t, ((0, 0), (1, 2), (0, 0), (0, 0)))
    return jnp.concatenate([tp[:, :N_DR + 2], tp[:, 1:]], axis=-1)


def _na_kernel(q_ref, k_ref, v_ref, kc_ref, vc_ref, tab_ref, o_ref):
    nq = NA_QROWS * GRID_W
    chains = []
    for sub in range(NA_SUBS):
        rows = slice(sub * nq, (sub + 1) * nq)
        chains.append(_na_sub_block((pl.program_id(2) * NA_SUBS + sub) * NA_QROWS, q_ref.at[rows], k_ref, v_ref,
                                    kc_ref, vc_ref, tab_ref, o_ref.at[rows]))
    _round_robin(chains)


def _na_sub_block(q0, q_ref, k_ref, v_ref, kc_ref, vc_ref, tab_ref, o_ref):
    k0 = jnp.clip(q0 - NA_WIN_ROWS // 2, 0, GRID_H - NA_KROWS)
    start = pl.multiple_of(k0 * GRID_W, GRID_W)
    nk = NA_KROWS * GRID_W
    lane = lax.broadcasted_iota(jnp.int32, (GRID_W, LANES), 1)

    tab_idx, penalty = [], []
    for qr in range(NA_QROWS):
        r = q0 + qr
        r0 = jnp.clip(r - NA_WIN_ROWS // 2, 0, GRID_H - NA_WIN_ROWS)
        idx_row, pen_row = [], []
        for j in range(NA_KROWS // 2):
            kra = k0 + 2 * j
            pa = jnp.where(jnp.logical_and(kra >= r0, kra < r0 + NA_WIN_ROWS), 0.0, NEG_INF)
            pb = jnp.where(jnp.logical_and(kra + 1 >= r0, kra + 1 < r0 + NA_WIN_ROWS), 0.0, NEG_INF)
            idx_row.append(jnp.clip(kra - r + NA_WIN_ROWS, 0, N_DR + 1))
            pen_row.append(jnp.where(lane < GRID_W, pa, pb))
        tab_idx.append(idx_row)
        penalty.append(pen_row)

    heads = [dict(hh=hh, sl=slice(hh * NA_HEAD_DIM, (hh + 1) * NA_HEAD_DIM)) for hh in range(2)]
    for c in heads:
        sl = c['sl']
        q = (q_ref[:, sl] * (NA_HEAD_DIM ** -0.5)).astype(BF16)
        c['v'] = v_ref[pl.ds(start, nk), sl].astype(BF16)
        c['vc'] = vc_ref[:, sl].astype(BF16)
        c['s_raw'] = _dot_nt(q, k_ref[pl.ds(start, nk), sl].astype(BF16))
        c['s_ctx'] = _dot_nt(q, kc_ref[:, sl].astype(BF16))
    yield
    for c in heads:
        rows = []
        for qr in range(NA_QROWS):
            cols = []
            for j in range(NA_KROWS // 2):
                s_blk = c['s_raw'][qr * GRID_W:(qr + 1) * GRID_W, j * LANES:(j + 1) * LANES]
                cols.append(s_blk + (tab_ref[c['hh'], tab_idx[qr][j]] + penalty[qr][j]))
            rows.append(jnp.concatenate(cols, axis=1))
        c['s_loc'] = jnp.concatenate(rows, axis=0)
        c['m'] = jnp.maximum(jnp.max(c['s_loc'], axis=-1, keepdims=True), jnp.max(c['s_ctx'], axis=-1, keepdims=True))
    yield
    for c in heads:
        p_loc = jnp.exp(c['s_loc'] - c['m'])
        p_ctx = jnp.exp(c['s_ctx'] - c['m'])
        c['den'] = jnp.sum(p_loc, axis=-1, keepdims=True) + jnp.sum(p_ctx, axis=-1, keepdims=True)
        c['p_loc'] = p_loc.astype(BF16)
        c['p_ctx'] = p_ctx.astype(BF16)
    yield
    outs = [(_dot(c['p_loc'], c['v']) + _dot(c['p_ctx'], c['vc'])) / c['den'] for c in heads]
    o_ref[...] = jnp.concatenate(outs, axis=-1)


def _na_latent(proj, bias):
    nq = NA_SUBS * NA_QROWS * GRID_W
    nblk = GRID_H // (NA_SUBS * NA_QROWS)
    ctx_blk0 = N_LAT // CTX_LEN
    return pl.pallas_call(
        _na_kernel,
        grid=(BATCH, NA_HEADS // 2, nblk),
        in_specs=[
            pl.BlockSpec((nq, LANES), lambda b, hp, blk: (b * nblk + blk, hp)),
            pl.BlockSpec((SEQ, LANES), lambda b, hp, blk: (b, 4 + hp)),
            pl.BlockSpec((SEQ, LANES), lambda b, hp, blk: (b, 8 + hp)),
            pl.BlockSpec((CTX_LEN, LANES), lambda b, hp, blk: (ctx_blk0 + b, 4 + hp)),
            pl.BlockSpec((CTX_LEN, LANES), lambda b, hp, blk: (ctx_blk0 + b, 8 + hp)),
            pl.BlockSpec((2, N_DR + 2, GRID_W, LANES), lambda b, hp, blk: (hp, 0, 0, 0)),
        ],
        out_specs=pl.BlockSpec((nq, LANES), lambda b, hp, blk: (b * nblk + blk, hp)),
        out_shape=jax.ShapeDtypeStruct((N_LAT, NA_HEADS * NA_HEAD_DIM), F32),
        compiler_params=_params("parallel", "parallel", "arbitrary"),
        name="na_latent",
    )(proj, proj, proj, proj, proj, bias)


def _ctx_attn_kernel(q_ref, k_ref, v_ref, o_ref):
    outs = []
    for hh in range(2):
        sl = slice(hh * NA_HEAD_DIM, (hh + 1) * NA_HEAD_DIM)
        q = (q_ref[:, sl] * (NA_HEAD_DIM ** -0.5)).astype(BF16)
        s = _dot_nt(q, k_ref[:, sl].astype(BF16))
        p = jnp.exp(s - jnp.max(s, axis=-1, keepdims=True))
        o = _dot(p.astype(BF16), v_ref[:, sl].astype(BF16))
        outs.append(o / jnp.sum(p, axis=-1, keepdims=True))
    o_ref[...] = jnp.concatenate(outs, axis=-1)


def _ctx_attention(proj):
    ctx_blk0 = N_LAT // CTX_LEN
    return pl.pallas_call(
        _ctx_attn_kernel,
        grid=(BATCH, NA_HEADS // 2),
        in_specs=[
            pl.BlockSpec((CTX_LEN, LANES), lambda b, hp: (ctx_blk0 + b, hp)),
            pl.BlockSpec((CTX_LEN, LANES), lambda b, hp: (ctx_blk0 + b, 4 + hp)),
            pl.BlockSpec((CTX_LEN, LANES), lambda b, hp: (ctx_blk0 + b, 8 + hp)),
        ],
        out_specs=pl.BlockSpec((CTX_LEN, LANES), lambda b, hp: (b, hp)),
        out_shape=jax.ShapeDtypeStruct((N_CTX, NA_HEADS * NA_HEAD_DIM), F32),
        compiler_params=_params("parallel", "parallel"),
        name="ctx_attention",
    )(proj, proj, proj)


N_CTX_CHUNKS = CTX_LEN // CHUNK
N_LAT_CHUNKS = SEQ // CHUNK
N_STEPS = N_CTX_CHUNKS + N_LAT_CHUNKS
SUB = 128
HG_CHUNK = 2 * SUB
HG_LEVELS = SUB.bit_length() - 1


def _chunk_block(b, s, rev, chunk=CHUNK):
    n_ctx, n_lat = CTX_LEN // chunk, SEQ // chunk
    c_ctx = (n_ctx - 1 - s) if rev else s
    c_lat = (n_lat - 1 - (s - n_ctx)) if rev else (s - n_ctx)
    ctx_blk = N_LAT // chunk + b * n_ctx + c_ctx
    lat_blk = b * n_lat + c_lat
    return jnp.where(s < n_ctx, ctx_blk, lat_blk)


def _tri_consts():
    i = np.arange(CHUNK)
    low = (i[None, :] <= i[:, None]).astype(np.float32)
    eye = np.eye(CHUNK, dtype=np.float32)
    return jnp.asarray(np.stack([low, low.T, eye]), BF16)


def _hgrn_consts():
    t = np.arange(SUB)[:, None]
    s = np.arange(SUB)[None, :]
    low = (s <= t).astype(np.float32)
    masks = np.zeros((2, HG_LEVELS + 1, SUB, SUB), np.float32)
    for l in range(HG_LEVELS):
        pair = ((t ^ s) >> l) == 1
        masks[0, l] = pair & (t > s)
        masks[1, l] = pair & (t < s)
    masks[:, HG_LEVELS] = (t == s)
    return jnp.asarray(np.stack([low, low.T]), BF16), jnp.asarray(masks)


def _hgrn_chain(q_ref, z_ref, v_ref, o_ref, sl, lb, st_ref, tri, masks_ref, rev):
    L, S = HG_CHUNK, SUB
    two = L == 2 * S
    z = z_ref[:, sl]
    v = v_ref[:, sl]
    e = jnp.exp(-jnp.abs(z))
    r = 1.0 / (1.0 + e)
    pos = z >= 0
    sig = jnp.where(pos, r, e * r)
    nsig = jnp.where(pos, e * r, r)
    k = (1.0 - lb) * nsig
    log2f = jnp.log2(jnp.maximum(lb + (1.0 - lb) * sig, F_FLOOR))
    q = _silu(q_ref[:, sl])
    yield

    g = _sel_dot(tri, log2f[:S])
    if two:
        g1 = _sel_dot(tri, log2f[S:])
        if rev:
            g = g + g1[0:1, :]
        else:
            g1 = g1 + g[S - 1:S, :]
        g = jnp.concatenate([g, g1], axis=0)
    row = lax.broadcasted_iota(jnp.int32, (L, HEAD_DIM), 0)

    st = st_ref[...]
    o_inter = _dot_nt((q * jnp.exp2(g)).astype(BF16), st.astype(BF16))
    yield

    nsub = L // S
    a = [jnp.zeros((S, S), F32) for _ in range(nsub)]
    cross = None
    q16 = q.astype(BF16)
    k16 = k.astype(BF16)
    bnd = g
    for l in range(HG_LEVELS + (1 if two else 0)):
        blk = 1 << l
        q_side = ((row & blk) == 0) if rev else ((row & blk) != 0)
        prev_end = pltpu.roll(bnd, (L - blk) if rev else blk, 0)
        w16 = jnp.exp2(jnp.where(q_side, g - prev_end, bnd - g)).astype(BF16)
        qb = q16 * w16
        kb = k16 * w16
        if blk < S:
            for c in range(nsub):
                a[c] = a[c] + masks_ref[l] * _dot_nt(qb[c * S:(c + 1) * S], kb[c * S:(c + 1) * S])
        elif rev:
            cross = _dot_nt(qb[:S], kb[S:])
        else:
            cross = _dot_nt(qb[S:], kb[:S])
        nxt = pltpu.roll(bnd, blk if rev else (L - blk), 0)
        bnd = jnp.where(q_side, bnd, nxt)
        yield
    v16 = v.astype(BF16)
    for c in range(nsub):
        a[c] = a[c] + masks_ref[HG_LEVELS] * _dot_nt(q16[c * S:(c + 1) * S], k16[c * S:(c + 1) * S])
    if not two:
        o_intra = _dot(a[0].astype(BF16), v16)
    elif rev:
        o_intra = jnp.concatenate([_dot(jnp.concatenate([a[0], cross], axis=1).astype(BF16), v16),
                                   _dot(a[1].astype(BF16), v16[S:])], axis=0)
    else:
        o_intra = jnp.concatenate([_dot(a[0].astype(BF16), v16[:S]),
                                   _dot(jnp.concatenate([cross, a[1]], axis=1).astype(BF16), v16)], axis=0)
    o_ref[:, sl] = o_inter + o_intra
    yield

    kd = k * jnp.exp2(bnd - g)
    st_ref[...] = jnp.exp2(bnd[0:1, :]) * st + lax.dot_general(
        v16, kd.astype(BF16), (((0,), (0,)), ((), ())), preferred_element_type=F32)


def _round_robin(chains):
    alive = list(chains)
    while alive:
        still = []
        for c in alive:
            try:
                next(c)
                still.append(c)
            except StopIteration:
                pass
        alive = still


def _hgrn_kernel(qf_ref, zf_ref, vf_ref, qb_ref, zb_ref, vb_ref, lb_ref, tri_ref, masks_ref,
                 of_ref, ob_ref, stf_ref, stb_ref):
    @pl.when(pl.program_id(1) == 0)
    def _():
        stf_ref[...] = jnp.zeros_like(stf_ref)
        stb_ref[...] = jnp.zeros_like(stb_ref)

    chains = []
    for h in range(HG_HEADS):
        sl = slice(h * HEAD_DIM, (h + 1) * HEAD_DIM)
        chains.append(_hgrn_chain(qf_ref, zf_ref, vf_ref, of_ref, sl, lb_ref[0:1, sl], stf_ref.at[h],
                                  tri_ref[0], masks_ref.at[0], False))
        chains.append(_hgrn_chain(qb_ref, zb_ref, vb_ref, ob_ref, sl, lb_ref[1:2, sl], stb_ref.at[h],
                                  tri_ref[1], masks_ref.at[1], True))
    _round_robin(chains)


def _hgrn2(proj, lower, tri, masks):
    width = HG_HEADS * HEAD_DIM

    def spec(col, rev):
        return pl.BlockSpec((HG_CHUNK, width), lambda b, s: (_chunk_block(b, s, rev, HG_CHUNK), col))

    out_spec = lambda rev: pl.BlockSpec((HG_CHUNK, width), lambda b, s: (_chunk_block(b, s, rev, HG_CHUNK), 0))
    shape = jax.ShapeDtypeStruct((N_ALL, width), F32)
    return pl.pallas_call(
        _hgrn_kernel,
        grid=(BATCH, (CTX_LEN + SEQ) // HG_CHUNK),
        in_specs=[
            spec(3, False), spec(4, False), spec(6, False),
            spec(3, True), spec(5, True), spec(6, True),
            pl.BlockSpec((2, width), lambda b, s: (0, 0)),
            pl.BlockSpec((2, SUB, SUB), lambda b, s: (0, 0, 0)),
            pl.BlockSpec((2, HG_LEVELS + 1, SUB, SUB), lambda b, s: (0, 0, 0, 0)),
        ],
        out_specs=[out_spec(False), out_spec(True)],
        out_shape=[shape, shape],
        scratch_shapes=[pltpu.VMEM((HG_HEADS, HEAD_DIM, HEAD_DIM), F32), pltpu.VMEM((HG_HEADS, HEAD_DIM, HEAD_DIM), F32)],
        compiler_params=_params("parallel", "arbitrary"),
        name="hgrn2",
    )(proj, proj, proj, proj, proj, proj, lower, tri, masks)


def _rope_tables():
    n_freq = HEAD_DIM // 4
    inv_freq = ROPE_BASE ** (-np.arange(n_freq, dtype=np.float64) / n_freq)
    t = np.arange(SEQ)
    ang_r = (t // GRID_W).astype(np.float64)[:, None] * inv_freq
    ang_c = (t % GRID_W).astype(np.float64)[:, None] * inv_freq
    cos = np.concatenate([np.cos(ang_r), np.cos(ang_r), np.cos(ang_c), np.cos(ang_c)], axis=-1)
    sin = np.concatenate([-np.sin(ang_r), np.sin(ang_r), -np.sin(ang_c), np.sin(ang_c)], axis=-1)
    cos = np.concatenate([cos, np.ones((CHUNK, HEAD_DIM))], axis=0)
    sin = np.concatenate([sin, np.zeros((CHUNK, HEAD_DIM))], axis=0)
    return jnp.asarray(cos, F32), jnp.asarray(sin, F32)


def _rope_swap():
    l = np.arange(HEAD_DIM)
    return jnp.asarray((l[:, None] == (l[None, :] ^ 32)).astype(np.float32), BF16)


def _rope(x, cos, sin, swap):
    hi = x.astype(BF16)
    mid = (x - hi.astype(F32)).astype(BF16)
    partner = _dot(hi, swap) + _dot(mid, swap)
    return x * cos + partner * sin


def _mlstm_gates(g_ref, gbias, tri_ref, rev):
    L = CHUNK
    log_i = g_ref[0, 0] + gbias[0][:, 0:1]
    xf = g_ref[0, 1] + gbias[1][:, 0:1]
    log_f = jnp.minimum(xf, 0.0) - jnp.log(1.0 + jnp.exp(-jnp.abs(xf)))
    r8 = jnp.concatenate([log_f, log_i], axis=0)
    low, up, eye = tri_ref[0], tri_ref[1], tri_ref[2]
    b_row = _dot_sel(r8, low if rev else up)[0:ML_HEADS, :]
    b_col = _sel_dot_nt(up if rev else low, r8)[:, 0:ML_HEADS]
    i_col = _sel_dot_nt(eye, r8)[:, ML_HEADS:2 * ML_HEADS]
    return log_i, b_row, b_col, i_col


def _mlstm_kernel(qf_ref, kf_ref, vf_ref, gf_ref, cf_ref, sf_ref,
                  qb_ref, kb_ref, vb_ref, gb_ref, cb_ref, sb_ref,
                  gbias_ref, tri_ref, swap_ref, of_ref, ob_ref, cnf_ref, cnb_ref, mf_ref, mb_ref):
    @pl.when(pl.program_id(1) == 0)
    def _():
        cnf_ref[...] = jnp.zeros_like(cnf_ref)
        cnb_ref[...] = jnp.zeros_like(cnb_ref)
        mf_ref[...] = jnp.zeros_like(mf_ref)
        mb_ref[...] = jnp.zeros_like(mb_ref)

    L = CHUNK
    ti = lax.broadcasted_iota(jnp.int32, (L, L), 0)
    si = lax.broadcasted_iota(jnp.int32, (L, L), 1)
    ones = jnp.ones((L, HEAD_DIM), BF16)
    swap = swap_ref[...]
    dirs = ((qf_ref, kf_ref, vf_ref, gf_ref, cf_ref, sf_ref, of_ref, cnf_ref, mf_ref, False),
            (qb_ref, kb_ref, vb_ref, gb_ref, cb_ref, sb_ref, ob_ref, cnb_ref, mb_ref, True))

    chains = []
    for d, (q_ref, k_ref, v_ref, g_ref, c_ref, s_ref, o_ref, cn_ref, m_ref, rev) in enumerate(dirs):
        log_i, b_row, b_col, i_col = _mlstm_gates(g_ref, gbias_ref[d], tri_ref, rev)
        for h in range(ML_HEADS):
            bc = jnp.broadcast_to(b_col[:, h:h + 1], (L, HEAD_DIM))
            ic = jnp.broadcast_to(i_col[:, h:h + 1], (L, HEAD_DIM))
            chains.append(dict(
                sl=slice(h * HEAD_DIM, (h + 1) * HEAD_DIM), rev=rev, q_ref=q_ref, k_ref=k_ref, v_ref=v_ref,
                c_ref=c_ref, s_ref=s_ref, o_ref=o_ref, cn_ref=cn_ref.at[h], m_ref=m_ref.at[h],
                log_i=log_i[h:h + 1, :], b_row=b_row[h:h + 1, :], b_col=bc, i_col=ic))

    for c in chains:
        cos, sin = c['c_ref'][...], c['s_ref'][...]
        c['qc'] = _rope(c['q_ref'][:, c['sl']], cos, sin, swap).astype(BF16)
        kc = _rope(c['k_ref'][:, c['sl']] * (HEAD_DIM ** -0.5), cos, sin, swap)
        c['kc'] = kc
        c['s'] = _dot_nt(c['qc'], kc.astype(BF16))
    for c in chains:
        b_row = c['b_row']
        c['b_end'] = b_row[:, 0:1] if c['rev'] else b_row[:, L - 1:L]
        tri = (si >= ti) if c['rev'] else (si <= ti)
        c['m_prev'] = c['m_ref'][0:1, 0:1]
        bc2 = jnp.concatenate([c['b_col'], c['b_col']], axis=-1)
        dmat = jnp.where(tri, bc2 + (c['log_i'] - b_row), NEG_INF)
        inter = c['b_col'] + c['m_prev']
        m_t = jnp.maximum(inter, jnp.max(dmat, axis=-1, keepdims=True))
        c['m_t'] = m_t
        c['w_inter'] = jnp.exp(inter - m_t)
        c['p'] = (jnp.exp(dmat - jnp.concatenate([m_t, m_t], axis=-1)) * c['s']).astype(BF16)
    for c in chains:
        c['v_ext'] = jnp.concatenate([c['v_ref'][:, c['sl']].astype(BF16), ones], axis=-1)
        c['cn'] = c['cn_ref'][...]
        w2 = jnp.concatenate([c['w_inter'], c['w_inter']], axis=-1)
        acc = _dot(c['p'], c['v_ext']) + w2 * _dot(c['qc'], c['cn'].astype(BF16))
        den = acc[:, HEAD_DIM:]
        c['o_ref'][:, c['sl']] = acc[:, :HEAD_DIM] / jnp.maximum(jnp.abs(den), jnp.exp(-c['m_t']))
    for c in chains:
        e_row = c['b_end'] + (c['log_i'] - c['b_row'])
        m_new = jnp.maximum(c['b_end'] + c['m_prev'], jnp.max(e_row, axis=-1, keepdims=True))
        w_old = jnp.exp(c['b_end'] + c['m_prev'] - m_new)
        w_s = jnp.exp(c['b_end'] - c['b_col'] + c['i_col'] - m_new)
        c['cn_ref'][...] = w_old * c['cn'] + lax.dot_general(
            (w_s * c['kc']).astype(BF16), c['v_ext'], (((0,), (0,)), ((), ())), preferred_element_type=F32)
        c['m_ref'][...] = jnp.broadcast_to(m_new, c['m_ref'].shape)


def _mlstm(proj, gates_t, gbias, cos, sin, tri):
    width = ML_HEADS * HEAD_DIM

    def spec(col, rev):
        return pl.BlockSpec((CHUNK, width), lambda b, s: (_chunk_block(b, s, rev), col))

    def gate_spec(rev):
        d = 1 if rev else 0
        return pl.BlockSpec((1, 2, ML_HEADS, CHUNK), lambda b, s: (d, 0, 0, _chunk_block(b, s, rev)))

    def rope_spec(rev):
        def idx(b, s):
            lat = _chunk_block(b, s, rev) - b * N_LAT_CHUNKS
            return (jnp.where(s < N_CTX_CHUNKS, N_LAT_CHUNKS, lat), 0)
        return pl.BlockSpec((CHUNK, LANES), idx)

    out_spec = lambda rev: pl.BlockSpec((CHUNK, width), lambda b, s: (_chunk_block(b, s, rev), 0))
    shape = jax.ShapeDtypeStruct((N_ALL, width), F32)
    per_dir = lambda rev: [spec(8, rev), spec(9, rev), spec(10, rev), gate_spec(rev), rope_spec(rev), rope_spec(rev)]
    args_dir = [proj, proj, proj, gates_t, cos, sin]
    return pl.pallas_call(
        _mlstm_kernel,
        grid=(BATCH, N_STEPS),
        in_specs=per_dir(False) + per_dir(True) + [
            pl.BlockSpec((2, 2, ML_HEADS, LANES), lambda b, s: (0, 0, 0, 0)),
            pl.BlockSpec((3, CHUNK, CHUNK), lambda b, s: (0, 0, 0)),
            pl.BlockSpec((HEAD_DIM, HEAD_DIM), lambda b, s: (0, 0)),
        ],
        out_specs=[out_spec(False), out_spec(True)],
        out_shape=[shape, shape],
        scratch_shapes=[pltpu.VMEM((ML_HEADS, HEAD_DIM, 2 * HEAD_DIM), F32),
                        pltpu.VMEM((ML_HEADS, HEAD_DIM, 2 * HEAD_DIM), F32),
                        pltpu.VMEM((ML_HEADS, 8, LANES), F32), pltpu.VMEM((ML_HEADS, 8, LANES), F32)],
        compiler_params=_params("parallel", "arbitrary"),
        name="mlstm",
    )(*args_dir, *args_dir, gbias, tri, _rope_swap())


def _head_rms(x, w):
    parts = []
    for hh in range(x.shape[-1] // HEAD_DIM):
        xs = x[:, hh * HEAD_DIM:(hh + 1) * HEAD_DIM]
        parts.append(xs * lax.rsqrt(jnp.mean(xs * xs, axis=-1, keepdims=True) + EPS))
    return jnp.concatenate(parts, axis=-1) * w


def _merge_kernel(na_ref, nac_ref, hgf_ref, hgb_ref, mlf_ref, mlb_ref, hgg_ref, mlo_ref, bg0_ref, bg1_ref, bg2_ref,
                  hl_ref, hc_ref, mod_ref, hgw_ref, mlw_ref, wb_ref, wo_ref, o_ref):
    hg = _head_rms(hgf_ref[...] + hgb_ref[...], hgw_ref[...]) * _silu(hgg_ref[...])
    ml = _sigmoid(mlo_ref[...]) * _head_rms(mlf_ref[...] + mlb_ref[...], mlw_ref[...])
    is_ctx = pl.program_id(0) >= N_LAT // TM_MERGE
    na = jnp.where(is_ctx, nac_ref[...], na_ref[...])
    y2 = None
    for bg_ref, branch, i in ((bg0_ref, na, 0), (bg1_ref, hg, 1), (bg2_ref, ml, 2)):
        p = _dot(branch.astype(BF16), wb_ref[i])
        term = jnp.tanh(0.5 * bg_ref[...]) * p + p
        y2 = term if y2 is None else y2 + term
    h = jnp.where(is_ctx, hc_ref[...], hl_ref[...])
    o_ref[...] = h + (0.5 * mod_ref[0, 2:3, :]) * _dot(y2.astype(BF16), wo_ref[...])


def _merge(n_rows, na, na_ctx, hgf, hgb, mlf, mlb, proj, h_lat, h_ctx, ctx_row0, mod5, hg_w, ml_w, w_branch, w_out):
    tm = TM_MERGE
    tiles_per_batch = SEQ // tm
    n_lat_tiles = N_LAT // tm
    ctx_tile0 = ctx_row0 // tm
    row = lambda w, c: pl.BlockSpec((tm, w), lambda i: (i, c))
    const = lambda shape: pl.BlockSpec(shape, lambda i: (0,) * len(shape))
    bg0 = BG_OFF // D_MODEL
    return pl.pallas_call(
        _merge_kernel,
        grid=(n_rows // tm,),
        in_specs=[
            pl.BlockSpec((tm, 512), lambda i: (jnp.minimum(i, n_lat_tiles - 1), 0)),
            pl.BlockSpec((tm, 512), lambda i: (jnp.maximum(i - n_lat_tiles, 0), 0)),
            row(512, 0), row(512, 0), row(512, 0), row(512, 0),
            row(512, 7), row(512, 11),
            row(D_MODEL, bg0), row(D_MODEL, bg0 + 1), row(D_MODEL, bg0 + 2),
            pl.BlockSpec((tm, D_MODEL), lambda i: (jnp.minimum(i, n_lat_tiles - 1), 0)),
            pl.BlockSpec((tm, D_MODEL), lambda i: (ctx_tile0 + jnp.maximum(i - n_lat_tiles, 0), 0)),
            pl.BlockSpec((1, 6, D_MODEL), lambda i: (jnp.minimum(i // tiles_per_batch, BATCH), 0, 0)),
            const((1, 512)), const((1, 512)),
            const((3, BRANCH_WIDTH, D_MODEL)), const((D_MODEL, D_MODEL)),
        ],
        out_specs=row(D_MODEL, 0),
        out_shape=jax.ShapeDtypeStruct((n_rows, D_MODEL), F32),
        compiler_params=_params("parallel"),
        name="merge",
    )(na, na_ctx, hgf, hgb, mlf, mlb, proj, proj, proj, proj, proj, h_lat, h_ctx, mod5, hg_w, ml_w, w_branch, w_out)


def _ffn_kernel(h_ref, nw_ref, mod_ref, wa_ref, wu_ref, wd_ref, o_ref):
    h = h_ref[...]
    f = _norm_mod(h, nw_ref[...], mod_ref[0, 3:4, :], mod_ref[0, 4:5, :]).astype(BF16)
    g = _silu(_dot(f, wa_ref[...])) * _dot(f, wu_ref[...])
    o_ref[...] = h + mod_ref[0, 5:6, :] * _dot(g.astype(BF16), wd_ref[...])


def _ffn(h_all, nw, mod5, w_up, w_down):
    n_rows = h_all.shape[0]
    tiles_per_batch = SEQ // TM_FFN
    resident = pl.Buffered(1)
    return pl.pallas_call(
        _ffn_kernel,
        grid=(n_rows // TM_FFN,),
        in_specs=[
            pl.BlockSpec((TM_FFN, D_MODEL), lambda i: (i, 0)),
            pl.BlockSpec((1, D_MODEL), lambda i: (0, 0)),
            pl.BlockSpec((1, 6, D_MODEL), lambda i: (i // tiles_per_batch, 0, 0)),
            pl.BlockSpec((D_MODEL, FFN_DIM), lambda i: (0, 0), pipeline_mode=resident),
            pl.BlockSpec((D_MODEL, FFN_DIM), lambda i: (0, 1), pipeline_mode=resident),
            pl.BlockSpec((FFN_DIM, D_MODEL), lambda i: (0, 0), pipeline_mode=resident),
        ],
        out_specs=pl.BlockSpec((TM_FFN, D_MODEL), lambda i: (i, 0)),
        out_shape=jax.ShapeDtypeStruct((n_rows, D_MODEL), F32),
        compiler_params=_params("parallel"),
        name="ffn",
    )(h_all, nw, mod5, w_up, w_up, w_down)


ROW_TILES = D_MODEL // LANES


def _store_rows_tiled(ref, x):
    for cb in range(ROW_TILES):
        ref[:, cb, :] = x[:, cb * LANES:(cb + 1) * LANES]


def _load_rows_tiled(ref):
    return jnp.concatenate([ref[:, cb, :] for cb in range(ROW_TILES)], axis=-1)


def _router_kernel(h_ref, nw_ref, mod_ref, wr_ref, f_ref, r_ref):
    f = _norm_mod(h_ref[...], nw_ref[...], mod_ref[0, 3:4, :], mod_ref[0, 4:5, :])
    _store_rows_tiled(f_ref, f)
    logits = jnp.dot(f, wr_ref[...], preferred_element_type=F32, precision=lax.Precision.HIGHEST)
    lane = lax.broadcasted_iota(jnp.int32, logits.shape, 1)
    logits = jnp.where(lane < N_EXPERTS, logits, -jnp.inf)
    m1 = jnp.max(logits, axis=-1, keepdims=True)
    i1 = jnp.min(jnp.where(logits == m1, lane, LANES), axis=-1, keepdims=True)
    rest = jnp.where(lane == i1, -jnp.inf, logits)
    m2 = jnp.max(rest, axis=-1, keepdims=True)
    i2 = jnp.min(jnp.where(rest == m2, lane, LANES), axis=-1, keepdims=True)
    e2 = jnp.exp(m2 - m1)
    w1 = 1.0 / (1.0 + e2)
    w2 = e2 / (1.0 + e2)
    r_ref[...] = jnp.where(lane == 0, i1.astype(F32),
                           jnp.where(lane == 1, i2.astype(F32),
                                     jnp.where(lane == 2, w1, jnp.where(lane == 3, w2, 0.0))))


def _router(h_lat, nw, mod5, w_router_pad):
    tm = 512
    tiles_per_batch = SEQ // tm
    return pl.pallas_call(
        _router_kernel,
        grid=(N_LAT // tm,),
        in_specs=[
            pl.BlockSpec((tm, D_MODEL), lambda i: (i, 0)),
            pl.BlockSpec((1, D_MODEL), lambda i: (0, 0)),
            pl.BlockSpec((1, 6, D_MODEL), lambda i: (i // tiles_per_batch, 0, 0)),
            pl.BlockSpec((D_MODEL, LANES), lambda i: (0, 0)),
        ],
        out_specs=[pl.BlockSpec((tm, ROW_TILES, LANES), lambda i: (i, 0, 0)), pl.BlockSpec((tm, LANES), lambda i: (i, 0))],
        out_shape=[jax.ShapeDtypeStruct((N_LAT, ROW_TILES, LANES), F32), jax.ShapeDtypeStruct((N_LAT, LANES), F32)],
        compiler_params=_params("parallel"),
        name="router",
    )(h_lat, nw, mod5, w_router_pad)


def _moe_kernel(be_ref, nused_ref, nvalid_ref, code_ref, f_hbm, wa_ref, wu_ref, wd_ref, y_hbm,
                xbuf, x16, acc, ybuf, sem_in, sem_out):
    i = pl.program_id(0)
    j = pl.program_id(1)
    last_j = pl.num_programs(1) - 1
    n_used = nused_ref[0]
    active = i < n_used
    slot = i % 2

    def start_gather(blk, buf):
        def body(r, c):
            tok = jnp.maximum(code_ref[blk * TM_MOE + r], 0) >> 1
            pltpu.make_async_copy(f_hbm.at[pl.ds(tok, 1)], xbuf.at[buf, pl.ds(r, 1)],
                                  sem_in.at[buf]).start(priority=GATHER_DMA_PRIORITY)
            return c
        lax.fori_loop(0, TM_MOE, body, 0, unroll=8)

    def wait_gather(buf):
        pltpu.make_async_copy(f_hbm.at[pl.ds(0, TM_MOE)], xbuf.at[buf], sem_in.at[buf]).wait()

    def start_scatter(blk):
        def body(r, c):
            code = code_ref[blk * TM_MOE + r]
            dst = (code & 1) * N_LAT + (code >> 1)
            pltpu.make_async_copy(ybuf.at[pl.ds(r, 1)], y_hbm.at[pl.ds(dst, 1)], sem_out).start()
            return c
        lax.fori_loop(0, nvalid_ref[blk], body, 0)

    def wait_scatter(blk):
        n = nvalid_ref[blk]
        p = TM_MOE
        while p >= 8:
            @pl.when((n & p) != 0)
            def _(p=p):
                pltpu.make_async_copy(ybuf.at[pl.ds(0, p)], y_hbm.at[pl.ds(0, p)], sem_out).wait()
            p //= 2

        def one(r, c):
            pltpu.make_async_copy(ybuf.at[pl.ds(0, 1)], y_hbm.at[pl.ds(0, 1)], sem_out).wait()
            return c
        lax.fori_loop(0, n & 7, one, 0)

    def prefetch_rows(first, count):
        for r in range(first, first + count):
            tok = jnp.maximum(code_ref[(i + 1) * TM_MOE + r], 0) >> 1
            pltpu.make_async_copy(f_hbm.at[pl.ds(tok, 1)], xbuf.at[1 - slot, pl.ds(r, 1)],
                                  sem_in.at[1 - slot]).start(priority=GATHER_DMA_PRIORITY)

    def partial_out():
        x = x16[...]
        g = _silu(_dot(x, wa_ref[0])) * _dot(x, wu_ref[0])
        return _dot(g.astype(BF16), wd_ref[0])

    @pl.when(jnp.logical_and(active, j == 0))
    def _():
        @pl.when(i == 0)
        def _():
            start_gather(0, 0)

        wait_gather(slot)
        x3 = xbuf[slot]
        for cb in range(ROW_TILES):
            x16[:, cb * LANES:(cb + 1) * LANES] = x3[:, cb, :].astype(BF16)

    @pl.when(jnp.logical_and(active, j == 0))
    def _():
        prefetch_rows(0, TM_MOE // 2)
        acc[...] = partial_out()

    @pl.when(jnp.logical_and(active, j == last_j))
    def _():
        @pl.when(i > 0)
        def _():
            wait_scatter(i - 1)

    @pl.when(jnp.logical_and(active, j == last_j))
    def _():
        prefetch_rows(TM_MOE // 2, TM_MOE // 2)
        _store_rows_tiled(ybuf, acc[...] + partial_out())

    @pl.when(jnp.logical_and(active, j == last_j))
    def _():
        start_scatter(i)

        @pl.when(i == n_used - 1)
        def _():
            wait_scatter(i)
            wait_gather(1 - slot)


def _moe_experts(block_e, n_used, n_valid, codes, f_lat, w_up, w_down):
    nj = EXPERT_DIM // TH_MOE
    assert nj == 2 and nj * TH_MOE == EXPERT_DIM

    def jj(i, j, nu):
        return jnp.where(i < nu[0], j, nj - 1)

    grid_spec = pltpu.PrefetchScalarGridSpec(
        num_scalar_prefetch=4,
        grid=(N_MOE_BLOCKS, nj),
        in_specs=[
            pl.BlockSpec(memory_space=pl.ANY),
            pl.BlockSpec((1, D_MODEL, TH_MOE), lambda i, j, be, nu, nv, cd: (be[i], 0, jj(i, j, nu))),
            pl.BlockSpec((1, D_MODEL, TH_MOE), lambda i, j, be, nu, nv, cd: (be[i], 0, nj + jj(i, j, nu))),
            pl.BlockSpec((1, TH_MOE, D_MODEL), lambda i, j, be, nu, nv, cd: (be[i], jj(i, j, nu), 0)),
        ],
        out_specs=pl.BlockSpec(memory_space=pl.ANY),
        scratch_shapes=[
            pltpu.VMEM((2, TM_MOE, ROW_TILES, LANES), F32), pltpu.VMEM((TM_MOE, D_MODEL), BF16),
            pltpu.VMEM((TM_MOE, D_MODEL), F32), pltpu.VMEM((TM_MOE, ROW_TILES, LANES), F32),
            pltpu.SemaphoreType.DMA((2,)), pltpu.SemaphoreType.DMA(()),
        ],
    )
    return pl.pallas_call(
        _moe_kernel,
        grid_spec=grid_spec,
        out_shape=jax.ShapeDtypeStruct((2 * N_LAT, ROW_TILES, LANES), F32),
        compiler_params=_params("arbitrary", "arbitrary"),
        name="moe_experts",
    )(block_e, n_used, n_valid, codes, f_lat, w_up, w_up, w_down)


def _combine_kernel(h_ref, y1_ref, y2_ref, r_ref, mod_ref, fw_ref, o_ref):
    r = r_ref[...]
    y = r[:, 2:3] * _load_rows_tiled(y1_ref) + r[:, 3:4] * _load_rows_tiled(y2_ref)
    h = h_ref[...] + mod_ref[0, 5:6, :] * y
    o_ref[...] = h * lax.rsqrt(jnp.mean(h * h, axis=-1, keepdims=True) + EPS) * fw_ref[...]


def _combine_final(h_lat, y, route, mod5, final_w):
    tm = 512
    tiles_per_batch = SEQ // tm
    return pl.pallas_call(
        _combine_kernel,
        grid=(N_LAT // tm,),
        in_specs=[
            pl.BlockSpec((tm, D_MODEL), lambda i: (i, 0)),
            pl.BlockSpec((tm, ROW_TILES, LANES), lambda i: (i, 0, 0)),
            pl.BlockSpec((tm, ROW_TILES, LANES), lambda i: (N_LAT // tm + i, 0, 0)),
            pl.BlockSpec((tm, LANES), lambda i: (i, 0)),
            pl.BlockSpec((1, 6, D_MODEL), lambda i: (i // tiles_per_batch, 0, 0)),
            pl.BlockSpec((1, D_MODEL), lambda i: (0, 0)),
        ],
        out_specs=pl.BlockSpec((tm, D_MODEL), lambda i: (i, 0)),
        out_shape=jax.ShapeDtypeStruct((N_LAT, D_MODEL), F32),
        compiler_params=_params("parallel"),
        name="combine_final",
    )(h_lat, y, y, route, mod5, final_w)


def _moe_plan(route):
    e12 = route[:, 0:2].astype(jnp.int32)
    onehot = (e12[:, :, None] == jnp.arange(N_EXPERTS, dtype=jnp.int32)).astype(jnp.int32).sum(axis=1)
    before = jnp.cumsum(onehot, axis=0) - onehot
    counts = jnp.sum(onehot, axis=0)
    nblk = (counts + TM_MOE - 1) // TM_MOE
    blk_end = jnp.cumsum(nblk)
    slot0 = (blk_end - nblk) * TM_MOE
    rank = jnp.take_along_axis(before, e12, axis=1)
    dest = slot0[e12] + rank
    codes = jnp.full((N_SLOTS,), -1, jnp.int32).at[dest.reshape(-1)].set(jnp.arange(2 * N_LAT, dtype=jnp.int32))
    n_used = blk_end[-1]
    blocks = jnp.minimum(jnp.arange(N_MOE_BLOCKS, dtype=jnp.int32), n_used - 1)
    block_e = jnp.minimum(jnp.sum((blocks[:, None] >= blk_end[None, :]).astype(jnp.int32), axis=1), N_EXPERTS - 1)
    n_valid = jnp.sum((codes >= 0).astype(jnp.int32).reshape(N_MOE_BLOCKS, TM_MOE), axis=1)
    return block_e, n_used.reshape(1).astype(jnp.int32), n_valid, codes


def kernel(x, c, ctx, c_ctx, mod_w, mod_b, norm1_w, w_in, na_rpb, hg_lb, hg_norm_w, ml_gate_b, ml_norm_w,
           w_branch, w_out, norm2_w, ffn_w_up, ffn_w_down, moe_router, moe_w_up, moe_w_down, final_norm_w):
    h_lat, h_ctx, ctx_row0 = x.reshape(N_LAT, D_MODEL), ctx.reshape(N_CTX, D_MODEL), 0
    c8 = jnp.concatenate([c, c_ctx[None, :], jnp.zeros((3, D_MODEL), F32)], axis=0)
    mods = _modulation(c8, mod_w, mod_b).reshape(DEPTH, 8, 6, D_MODEL)

    lb_p = jax.nn.softmax(hg_lb.astype(F32), axis=0)
    hg_lower = jnp.cumsum(lb_p, axis=0) - lb_p[0]
    tri = _tri_consts()
    hg_tri, hg_masks = _hgrn_consts()
    cos, sin = _rope_tables()

    out = None
    for layer in range(DEPTH):
        last = layer == DEPTH - 1
        mod5 = mods[layer, :5]
        wl = w_in[layer]
        w_main = jnp.concatenate([wl[:, :MAIN_W], wl[:, MAIN_W + ML_GATE_COLS:]], axis=1).astype(BF16)
        w_gates = jnp.pad(wl[:, MAIN_W:MAIN_W + ML_GATE_COLS], ((0, 0), (0, LANES - ML_GATE_COLS))).astype(BF16)
        proj, gates = _inproj(h_lat, h_ctx, ctx_row0, norm1_w[layer][None, :], mod5, w_main, w_gates)

        na = _na_latent(proj, _na_bias_tables(na_rpb[layer]))
        na_ctx = na if last else _ctx_attention(proj)
        hgf, hgb = _hgrn2(proj, hg_lower[layer], hg_tri, hg_masks)
        gates_t = gates[:, :ML_GATE_COLS].T.reshape(2, 2, ML_HEADS, N_ALL)
        gbias = jnp.broadcast_to(ml_gate_b[layer][..., None], (2, 2, ML_HEADS, LANES))
        mlf, mlb = _mlstm(proj, gates_t, gbias, cos, sin, tri)

        n_rows = N_LAT if last else N_ALL
        h_all = _merge(n_rows, na, na_ctx, hgf, hgb, mlf, mlb, proj, h_lat, h_ctx, ctx_row0, mod5,
                       hg_norm_w[layer][None, :], ml_norm_w[layer][None, :],
                       w_branch[layer].astype(BF16), w_out[layer].astype(BF16))
        i = layer // 2
        if layer % 2 == 0:
            h_all = _ffn(h_all, norm2_w[layer][None, :], mod5, ffn_w_up[i].astype(BF16), ffn_w_down[i].astype(BF16))
            h_lat, h_ctx, ctx_row0 = h_all, h_all, N_LAT
            if last:
                raise NotImplementedError("final norm after a dense last layer")
        else:
            if not last:
                raise NotImplementedError("MoE on the context stream")
            w_router_pad = jnp.pad(moe_router[i], ((0, 0), (0, LANES - N_EXPERTS)))
            f_lat, route = _router(h_all, norm2_w[layer][None, :], mod5, w_router_pad)
            block_e, n_used, n_valid, codes = _moe_plan(route)
            y = _moe_experts(block_e, n_used, n_valid, codes, f_lat,
                             moe_w_up[i].astype(BF16), moe_w_down[i].astype(BF16))
            out = _combine_final(h_all, y, route, mod5, final_norm_w[None, :])
    return out.reshape(BATCH, SEQ, D_MODEL)
```

```python
import functools

import numpy as np
import jax
import jax.numpy as jnp
from jax import lax
from jax.experimental import pallas as pl
from jax.experimental.pallas import tpu as pltpu

F32 = jnp.float32
BF16 = jnp.bfloat16

D_MODEL = 1024
BATCH = 4
SEQ = 4096
DEPTH = 2
GRID_W = 64
GRID_H = SEQ // GRID_W
CTX_LEN = 256
EPS = 1e-6
NEG_INF = -1e30
F_FLOOR = 1e-30
NA_HEADS = 8
NA_HEAD_DIM = 64
NA_WIN_ROWS = 8
NA_WIN_COLS = 16
HG_HEADS = 4
ML_HEADS = 4
HEAD_DIM = 128
ML_GATE_COLS = 16
ROPE_BASE = 10000.0
BRANCH_WIDTH = 512
FFN_DIM = 2816
N_EXPERTS = 8
EXPERT_DIM = 3584

N_LAT = BATCH * SEQ
N_CTX = BATCH * CTX_LEN
N_ALL = N_LAT + N_CTX

LANES = 128
VMEM_LIMIT = 56 * 1024 * 1024

MAIN_W = 12 * 512
BG_OFF = MAIN_W
PROJ_W = MAIN_W + 3 * D_MODEL

TM_PROJ = 1024
TN_PROJ = PROJ_W // 4
TM_MERGE = 256
TM_FFN = 512
NA_SUBS = 2
NA_QROWS = 4
NA_KROWS = NA_QROWS + NA_WIN_ROWS
CHUNK = 256
TM_MOE = 512
TH_MOE = EXPERT_DIM // 2
GATHER_DMA_PRIORITY = 1
N_MOE_BLOCKS = -(-(2 * N_LAT + N_EXPERTS * (TM_MOE - 1)) // TM_MOE)
N_SLOTS = N_MOE_BLOCKS * TM_MOE


def _params(*sem):
    return pltpu.CompilerParams(dimension_semantics=sem, vmem_limit_bytes=VMEM_LIMIT)


def _sigmoid(x):
    return 0.5 * jnp.tanh(0.5 * x) + 0.5


def _silu(x):
    return x * _sigmoid(x)


def _dot(a, b):
    return jnp.dot(a, b, preferred_element_type=F32)


def _dot_nt(a, b):
    return lax.dot_general(a, b, (((1,), (1,)), ((), ())), preferred_element_type=F32)


def _split3(x):
    hi = x.astype(BF16)
    r = x - hi.astype(F32)
    mid = r.astype(BF16)
    lo = (r - mid.astype(F32)).astype(BF16)
    return hi, mid, lo


def _sel_dot(sel, x):
    hi, mid, lo = _split3(x)
    return _dot(sel, lo) + _dot(sel, mid) + _dot(sel, hi)


def _sel_dot_nt(sel, x):
    hi, mid, lo = _split3(x)
    return _dot_nt(sel, lo) + _dot_nt(sel, mid) + _dot_nt(sel, hi)


def _dot_sel(x, sel):
    hi, mid, lo = _split3(x)
    return _dot(lo, sel) + _dot(mid, sel) + _dot(hi, sel)


def _norm_mod(x, nw, shift, scale):
    y = x * lax.rsqrt(jnp.mean(x * x, axis=-1, keepdims=True) + EPS) * nw
    return y * (1.0 + scale) + shift


def _mod_kernel(c_ref, w_ref, b_ref, o_ref):
    s = _silu(c_ref[...])
    o_ref[0] = jnp.dot(s, w_ref[0], preferred_element_type=F32, precision=lax.Precision.HIGHEST) + b_ref[0]


def _modulation(c8, mod_w, mod_b):
    tn = 1536
    return pl.pallas_call(
        _mod_kernel,
        grid=(DEPTH, 6 * D_MODEL // tn),
        in_specs=[
            pl.BlockSpec((8, D_MODEL), lambda l, j: (0, 0)),
            pl.BlockSpec((1, D_MODEL, tn), lambda l, j: (l, 0, j)),
            pl.BlockSpec((1, 1, tn), lambda l, j: (l, 0, j)),
        ],
        out_specs=pl.BlockSpec((1, 8, tn), lambda l, j: (l, 0, j)),
        out_shape=jax.ShapeDtypeStruct((DEPTH, 8, 6 * D_MODEL), F32),
        compiler_params=_params("parallel", "parallel"),
        name="modulation",
    )(c8, mod_w, mod_b.reshape(DEPTH, 1, 6 * D_MODEL))


def _inproj_kernel(hl_ref, hc_ref, nw_ref, mod_ref, w_ref, wg_ref, o_ref, g_ref, a_scr):
    @pl.when(pl.program_id(1) == 0)
    def _():
        h = jnp.where(pl.program_id(0) >= N_LAT // TM_PROJ, hc_ref[...], hl_ref[...])
        a = _norm_mod(h, nw_ref[...], mod_ref[0, 0:1, :], mod_ref[0, 1:2, :])
        a_scr[...] = a.astype(BF16)
        g_ref[...] = _dot(a_scr[...], wg_ref[...])

    o_ref[...] = _dot(a_scr[...], w_ref[...])


def _inproj(h_lat, h_ctx, ctx_row0, nw, mod5, w_main, w_gates):
    tiles_per_batch = SEQ // TM_PROJ
    n_lat_tiles = N_LAT // TM_PROJ
    ctx_tile0 = ctx_row0 // TM_PROJ
    return pl.pallas_call(
        _inproj_kernel,
        grid=(N_ALL // TM_PROJ, PROJ_W // TN_PROJ),
        in_specs=[
            pl.BlockSpec((TM_PROJ, D_MODEL), lambda i, j: (jnp.minimum(i, n_lat_tiles - 1), 0)),
            pl.BlockSpec((TM_PROJ, D_MODEL), lambda i, j: (ctx_tile0 + jnp.maximum(i - n_lat_tiles, 0), 0)),
            pl.BlockSpec((1, D_MODEL), lambda i, j: (0, 0)),
            pl.BlockSpec((1, 6, D_MODEL), lambda i, j: (i // tiles_per_batch, 0, 0)),
            pl.BlockSpec((D_MODEL, TN_PROJ), lambda i, j: (0, j)),
            pl.BlockSpec((D_MODEL, LANES), lambda i, j: (0, 0)),
        ],
        out_specs=[pl.BlockSpec((TM_PROJ, TN_PROJ), lambda i, j: (i, j)),
                   pl.BlockSpec((TM_PROJ, LANES), lambda i, j: (i, 0))],
        out_shape=[jax.ShapeDtypeStruct((N_ALL, PROJ_W), F32), jax.ShapeDtypeStruct((N_ALL, LANES), F32)],
        scratch_shapes=[pltpu.VMEM((TM_PROJ, D_MODEL), BF16)],
        compiler_params=_params("parallel", "arbitrary"),
        name="inproj",
    )(h_lat, h_ctx, nw, mod5, w_main, w_gates)


N_DR = 2 * NA_WIN_ROWS - 1


def _na_bias_tables(rpb):
    qc = np.arange(GRID_W)[:, None]
    kc = np.arange(GRID_W)[None, :]
    dc = np.clip(kc - qc + NA_WIN_COLS - 1, 0, 2 * NA_WIN_COLS - 2)
    ws = np.clip(qc - NA_WIN_COLS // 2, 0, GRID_W - NA_WIN_COLS)
    col_ok = (kc >= ws) & (kc < ws + NA_WIN_COLS)
    onehot = ((dc[None] == np.arange(2 * NA_WIN_COLS - 1)[:, None, None]) & col_ok[None]).astype(np.float32)
    t = jnp.einsum('hrd,dqk->hrqk', rpb.astype(F32), jnp.asarray(onehot), precision=lax.Precision.HIGHEST)
    t = t + jnp.asarray(np.where(col_ok, 0.0, NEG_INF).astype(np.float32))
    tp = jnp.pad(t, ((0, 0), (1, 2), (0, 0), (0, 0)))
    return jnp.concatenate([tp[:, :N_DR + 2], tp[:, 1:]], axis=-1)


def _na_kernel(q_ref, k_ref, v_ref, kc_ref, vc_ref, tab_ref, o_ref):
    nq = NA_QROWS * GRID_W
    chains = []
    for sub in range(NA_SUBS):
        rows = slice(sub * nq, (sub + 1) * nq)
        chains.append(_na_sub_block((pl.program_id(2) * NA_SUBS + sub) * NA_QROWS, q_ref.at[rows], k_ref, v_ref,
                                    kc_ref, vc_ref, tab_ref, o_ref.at[rows]))
    _round_robin(chains)


def _na_sub_block(q0, q_ref, k_ref, v_ref, kc_ref, vc_ref, tab_ref, o_ref):
    k0 = jnp.clip(q0 - NA_WIN_ROWS // 2, 0, GRID_H - NA_KROWS)
    start = pl.multiple_of(k0 * GRID_W, GRID_W)
    nk = NA_KROWS * GRID_W
    lane = lax.broadcasted_iota(jnp.int32, (GRID_W, LANES), 1)

    tab_idx, penalty = [], []
    for qr in range(NA_QROWS):
        r = q0 + qr
        r0 = jnp.clip(r - NA_WIN_ROWS // 2, 0, GRID_H - NA_WIN_ROWS)
        idx_row, pen_row = [], []
        for j in range(NA_KROWS // 2):
            kra = k0 + 2 * j
            pa = jnp.where(jnp.logical_and(kra >= r0, kra < r0 + NA_WIN_ROWS), 0.0, NEG_INF)
            pb = jnp.where(jnp.logical_and(kra + 1 >= r0, kra + 1 < r0 + NA_WIN_ROWS), 0.0, NEG_INF)
            idx_row.append(jnp.clip(kra - r + NA_WIN_ROWS, 0, N_DR + 1))
            pen_row.append(jnp.where(lane < GRID_W, pa, pb))
        tab_idx.append(idx_row)
        penalty.append(pen_row)

    heads = [dict(hh=hh, sl=slice(hh * NA_HEAD_DIM, (hh + 1) * NA_HEAD_DIM)) for hh in range(2)]
    for c in heads:
        sl = c['sl']
        q = (q_ref[:, sl] * (NA_HEAD_DIM ** -0.5)).astype(BF16)
        c['v'] = v_ref[pl.ds(start, nk), sl].astype(BF16)
        c['vc'] = vc_ref[:, sl].astype(BF16)
        c['s_raw'] = _dot_nt(q, k_ref[pl.ds(start, nk), sl].astype(BF16))
        c['s_ctx'] = _dot_nt(q, kc_ref[:, sl].astype(BF16))
    yield
    for c in heads:
        rows = []
        for qr in range(NA_QROWS):
            cols = []
            for j in range(NA_KROWS // 2):
                s_blk = c['s_raw'][qr * GRID_W:(qr + 1) * GRID_W, j * LANES:(j + 1) * LANES]
                cols.append(s_blk + (tab_ref[c['hh'], tab_idx[qr][j]] + penalty[qr][j]))
            rows.append(jnp.concatenate(cols, axis=1))
        c['s_loc'] = jnp.concatenate(rows, axis=0)
        c['m'] = jnp.maximum(jnp.max(c['s_loc'], axis=-1, keepdims=True), jnp.max(c['s_ctx'], axis=-1, keepdims=True))
    yield
    for c in heads:
        p_loc = jnp.exp(c['s_loc'] - c['m'])
        p_ctx = jnp.exp(c['s_ctx'] - c['m'])
        c['den'] = jnp.sum(p_loc, axis=-1, keepdims=True) + jnp.sum(p_ctx, axis=-1, keepdims=True)
        c['p_loc'] = p_loc.astype(BF16)
        c['p_ctx'] = p_ctx.astype(BF16)
    yield
    outs = [(_dot(c['p_loc'], c['v']) + _dot(c['p_ctx'], c['vc'])) / c['den'] for c in heads]
    o_ref[...] = jnp.concatenate(outs, axis=-1)


def _na_latent(proj, bias):
    nq = NA_SUBS * NA_QROWS * GRID_W
    nblk = GRID_H // (NA_SUBS * NA_QROWS)
    ctx_blk0 = N_LAT // CTX_LEN
    return pl.pallas_call(
        _na_kernel,
        grid=(BATCH, NA_HEADS // 2, nblk),
        in_specs=[
            pl.BlockSpec((nq, LANES), lambda b, hp, blk: (b * nblk + blk, hp)),
            pl.BlockSpec((SEQ, LANES), lambda b, hp, blk: (b, 4 + hp)),
            pl.BlockSpec((SEQ, LANES), lambda b, hp, blk: (b, 8 + hp)),
            pl.BlockSpec((CTX_LEN, LANES), lambda b, hp, blk: (ctx_blk0 + b, 4 + hp)),
            pl.BlockSpec((CTX_LEN, LANES), lambda b, hp, blk: (ctx_blk0 + b, 8 + hp)),
            pl.BlockSpec((2, N_DR + 2, GRID_W, LANES), lambda b, hp, blk: (hp, 0, 0, 0)),
        ],
        out_specs=pl.BlockSpec((nq, LANES), lambda b, hp, blk: (b * nblk + blk, hp)),
        out_shape=jax.ShapeDtypeStruct((N_LAT, NA_HEADS * NA_HEAD_DIM), F32),
        compiler_params=_params("parallel", "parallel", "arbitrary"),
        name="na_latent",
    )(proj, proj, proj, proj, proj, bias)


def _ctx_attn_kernel(q_ref, k_ref, v_ref, o_ref):
    outs = []
    for hh in range(2):
        sl = slice(hh * NA_HEAD_DIM, (hh + 1) * NA_HEAD_DIM)
        q = (q_ref[:, sl] * (NA_HEAD_DIM ** -0.5)).astype(BF16)
        s = _dot_nt(q, k_ref[:, sl].astype(BF16))
        p = jnp.exp(s - jnp.max(s, axis=-1, keepdims=True))
        o = _dot(p.astype(BF16), v_ref[:, sl].astype(BF16))
        outs.append(o / jnp.sum(p, axis=-1, keepdims=True))
    o_ref[...] = jnp.concatenate(outs, axis=-1)


def _ctx_attention(proj):
    ctx_blk0 = N_LAT // CTX_LEN
    return pl.pallas_call(
        _ctx_attn_kernel,
        grid=(BATCH, NA_HEADS // 2),
        in_specs=[
            pl.BlockSpec((CTX_LEN, LANES), lambda b, hp: (ctx_blk0 + b, hp)),
            pl.BlockSpec((CTX_LEN, LANES), lambda b, hp: (ctx_blk0 + b, 4 + hp)),
            pl.BlockSpec((CTX_LEN, LANES), lambda b, hp: (ctx_blk0 + b, 8 + hp)),
        ],
        out_specs=pl.BlockSpec((CTX_LEN, LANES), lambda b, hp: (b, hp)),
        out_shape=jax.ShapeDtypeStruct((N_CTX, NA_HEADS * NA_HEAD_DIM), F32),
        compiler_params=_params("parallel", "parallel"),
        name="ctx_attention",
    )(proj, proj, proj)


N_CTX_CHUNKS = CTX_LEN // CHUNK
N_LAT_CHUNKS = SEQ // CHUNK
N_STEPS = N_CTX_CHUNKS + N_LAT_CHUNKS
SUB = 128
HG_CHUNK = 2 * SUB
HG_LEVELS = SUB.bit_length() - 1


def _chunk_block(b, s, rev, chunk=CHUNK):
    n_ctx, n_lat = CTX_LEN // chunk, SEQ // chunk
    c_ctx = (n_ctx - 1 - s) if rev else s
    c_lat = (n_lat - 1 - (s - n_ctx)) if rev else (s - n_ctx)
    ctx_blk = N_LAT // chunk + b * n_ctx + c_ctx
    lat_blk = b * n_lat + c_lat
    return jnp.where(s < n_ctx, ctx_blk, lat_blk)


def _tri_consts():
    i = np.arange(CHUNK)
    low = (i[None, :] <= i[:, None]).astype(np.float32)
    eye = np.eye(CHUNK, dtype=np.float32)
    return jnp.asarray(np.stack([low, low.T, eye]), BF16)


def _hgrn_consts():
    t = np.arange(SUB)[:, None]
    s = np.arange(SUB)[None, :]
    low = (s <= t).astype(np.float32)
    masks = np.zeros((2, HG_LEVELS + 1, SUB, SUB), np.float32)
    for l in range(HG_LEVELS):
        pair = ((t ^ s) >> l) == 1
        masks[0, l] = pair & (t > s)
        masks[1, l] = pair & (t < s)
    masks[:, HG_LEVELS] = (t == s)
    return jnp.asarray(np.stack([low, low.T]), BF16), jnp.asarray(masks)


def _hgrn_chain(q_ref, z_ref, v_ref, o_ref, sl, lb, st_ref, tri, masks_ref, rev):
    L, S = HG_CHUNK, SUB
    two = L == 2 * S
    z = z_ref[:, sl]
    v = v_ref[:, sl]
    e = jnp.exp(-jnp.abs(z))
    r = 1.0 / (1.0 + e)
    pos = z >= 0
    sig = jnp.where(pos, r, e * r)
    nsig = jnp.where(pos, e * r, r)
    k = (1.0 - lb) * nsig
    log2f = jnp.log2(jnp.maximum(lb + (1.0 - lb) * sig, F_FLOOR))
    q = _silu(q_ref[:, sl])
    yield

    g = _sel_dot(tri, log2f[:S])
    if two:
        g1 = _sel_dot(tri, log2f[S:])
        if rev:
            g = g + g1[0:1, :]
        else:
            g1 = g1 + g[S - 1:S, :]
        g = jnp.concatenate([g, g1], axis=0)
    row = lax.broadcasted_iota(jnp.int32, (L, HEAD_DIM), 0)

    st = st_ref[...]
    o_inter = _dot_nt((q * jnp.exp2(g)).astype(BF16), st.astype(BF16))
    yield

    nsub = L // S
    a = [jnp.zeros((S, S), F32) for _ in range(nsub)]
    cross = None
    q16 = q.astype(BF16)
    k16 = k.astype(BF16)
    bnd = g
    for l in range(HG_LEVELS + (1 if two else 0)):
        blk = 1 << l
        q_side = ((row & blk) == 0) if rev else ((row & blk) != 0)
        prev_end = pltpu.roll(bnd, (L - blk) if rev else blk, 0)
        w16 = jnp.exp2(jnp.where(q_side, g - prev_end, bnd - g)).astype(BF16)
        qb = q16 * w16
        kb = k16 * w16
        if blk < S:
            for c in range(nsub):
                a[c] = a[c] + masks_ref[l] * _dot_nt(qb[c * S:(c + 1) * S], kb[c * S:(c + 1) * S])
        elif rev:
            cross = _dot_nt(qb[:S], kb[S:])
        else:
            cross = _dot_nt(qb[S:], kb[:S])
        nxt = pltpu.roll(bnd, blk if rev else (L - blk), 0)
        bnd = jnp.where(q_side, bnd, nxt)
        yield
    v16 = v.astype(BF16)
    for c in range(nsub):
        a[c] = a[c] + masks_ref[HG_LEVELS] * _dot_nt(q16[c * S:(c + 1) * S], k16[c * S:(c + 1) * S])
    if not two:
        o_intra = _dot(a[0].astype(BF16), v16)
    elif rev:
        o_intra = jnp.concatenate([_dot(jnp.concatenate([a[0], cross], axis=1).astype(BF16), v16),
                                   _dot(a[1].astype(BF16), v16[S:])], axis=0)
    else:
        o_intra = jnp.concatenate([_dot(a[0].astype(BF16), v16[:S]),
                                   _dot(jnp.concatenate([cross, a[1]], axis=1).astype(BF16), v16)], axis=0)
    o_ref[:, sl] = o_inter + o_intra
    yield

    kd = k * jnp.exp2(bnd - g)
    st_ref[...] = jnp.exp2(bnd[0:1, :]) * st + lax.dot_general(
        v16, kd.astype(BF16), (((0,), (0,)), ((), ())), preferred_element_type=F32)


def _round_robin(chains):
    alive = list(chains)
    while alive:
        still = []
        for c in alive:
            try:
                next(c)
                still.append(c)
            except StopIteration:
                pass
        alive = still


def _hgrn_kernel(qf_ref, zf_ref, vf_ref, qb_ref, zb_ref, vb_ref, lb_ref, tri_ref, masks_ref,
                 of_ref, ob_ref, stf_ref, stb_ref):
    @pl.when(pl.program_id(1) == 0)
    def _():
        stf_ref[...] = jnp.zeros_like(stf_ref)
        stb_ref[...] = jnp.zeros_like(stb_ref)

    chains = []
    for h in range(HG_HEADS):
        sl = slice(h * HEAD_DIM, (h + 1) * HEAD_DIM)
        chains.append(_hgrn_chain(qf_ref, zf_ref, vf_ref, of_ref, sl, lb_ref[0:1, sl], stf_ref.at[h],
                                  tri_ref[0], masks_ref.at[0], False))
        chains.append(_hgrn_chain(qb_ref, zb_ref, vb_ref, ob_ref, sl, lb_ref[1:2, sl], stb_ref.at[h],
                                  tri_ref[1], masks_ref.at[1], True))
    _round_robin(chains)


def _hgrn2(proj, lower, tri, masks):
    width = HG_HEADS * HEAD_DIM

    def spec(col, rev):
        return pl.BlockSpec((HG_CHUNK, width), lambda b, s: (_chunk_block(b, s, rev, HG_CHUNK), col))

    out_spec = lambda rev: pl.BlockSpec((HG_CHUNK, width), lambda b, s: (_chunk_block(b, s, rev, HG_CHUNK), 0))
    shape = jax.ShapeDtypeStruct((N_ALL, width), F32)
    return pl.pallas_call(
        _hgrn_kernel,
        grid=(BATCH, (CTX_LEN + SEQ) // HG_CHUNK),
        in_specs=[
            spec(3, False), spec(4, False), spec(6, False),
            spec(3, True), spec(5, True), spec(6, True),
            pl.BlockSpec((2, width), lambda b, s: (0, 0)),
            pl.BlockSpec((2, SUB, SUB), lambda b, s: (0, 0, 0)),
            pl.BlockSpec((2, HG_LEVELS + 1, SUB, SUB), lambda b, s: (0, 0, 0, 0)),
        ],
        out_specs=[out_spec(False), out_spec(True)],
        out_shape=[shape, shape],
        scratch_shapes=[pltpu.VMEM((HG_HEADS, HEAD_DIM, HEAD_DIM), F32), pltpu.VMEM((HG_HEADS, HEAD_DIM, HEAD_DIM), F32)],
        compiler_params=_params("parallel", "arbitrary"),
        name="hgrn2",
    )(proj, proj, proj, proj, proj, proj, lower, tri, masks)


def _rope_tables():
    n_freq = HEAD_DIM // 4
    inv_freq = ROPE_BASE ** (-np.arange(n_freq, dtype=np.float64) / n_freq)
    t = np.arange(SEQ)
    ang_r = (t // GRID_W).astype(np.float64)[:, None] * inv_freq
    ang_c = (t % GRID_W).astype(np.float64)[:, None] * inv_freq
    cos = np.concatenate([np.cos(ang_r), np.cos(ang_r), np.cos(ang_c), np.cos(ang_c)], axis=-1)
    sin = np.concatenate([-np.sin(ang_r), np.sin(ang_r), -np.sin(ang_c), np.sin(ang_c)], axis=-1)
    cos = np.concatenate([cos, np.ones((CHUNK, HEAD_DIM))], axis=0)
    sin = np.concatenate([sin, np.zeros((CHUNK, HEAD_DIM))], axis=0)
    return jnp.asarray(cos, F32), jnp.asarray(sin, F32)


def _rope_swap():
    l = np.arange(HEAD_DIM)
    return jnp.asarray((l[:, None] == (l[None, :] ^ 32)).astype(np.float32), BF16)


def _rope(x, cos, sin, swap):
    hi = x.astype(BF16)
    mid = (x - hi.astype(F32)).astype(BF16)
    partner = _dot(hi, swap) + _dot(mid, swap)
    return x * cos + partner * sin


def _mlstm_gates(g_ref, gbias, tri_ref, rev):
    L = CHUNK
    log_i = g_ref[0, 0] + gbias[0][:, 0:1]
    xf = g_ref[0, 1] + gbias[1][:, 0:1]
    log_f = jnp.minimum(xf, 0.0) - jnp.log(1.0 + jnp.exp(-jnp.abs(xf)))
    r8 = jnp.concatenate([log_f, log_i], axis=0)
    low, up, eye = tri_ref[0], tri_ref[1], tri_ref[2]
    b_row = _dot_sel(r8, low if rev else up)[0:ML_HEADS, :]
    b_col = _sel_dot_nt(up if rev else low, r8)[:, 0:ML_HEADS]
    i_col = _sel_dot_nt(eye, r8)[:, ML_HEADS:2 * ML_HEADS]
    return log_i, b_row, b_col, i_col


def _mlstm_kernel(qf_ref, kf_ref, vf_ref, gf_ref, cf_ref, sf_ref,
                  qb_ref, kb_ref, vb_ref, gb_ref, cb_ref, sb_ref,
                  gbias_ref, tri_ref, swap_ref, of_ref, ob_ref, cnf_ref, cnb_ref, mf_ref, mb_ref):
    @pl.when(pl.program_id(1) == 0)
    def _():
        cnf_ref[...] = jnp.zeros_like(cnf_ref)
        cnb_ref[...] = jnp.zeros_like(cnb_ref)
        mf_ref[...] = jnp.zeros_like(mf_ref)
        mb_ref[...] = jnp.zeros_like(mb_ref)

    L = CHUNK
    ti = lax.broadcasted_iota(jnp.int32, (L, L), 0)
    si = lax.broadcasted_iota(jnp.int32, (L, L), 1)
    ones = jnp.ones((L, HEAD_DIM), BF16)
    swap = swap_ref[...]
    dirs = ((qf_ref, kf_ref, vf_ref, gf_ref, cf_ref, sf_ref, of_ref, cnf_ref, mf_ref, False),
            (qb_ref, kb_ref, vb_ref, gb_ref, cb_ref, sb_ref, ob_ref, cnb_ref, mb_ref, True))

    chains = []
    for d, (q_ref, k_ref, v_ref, g_ref, c_ref, s_ref, o_ref, cn_ref, m_ref, rev) in enumerate(dirs):
        log_i, b_row, b_col, i_col = _mlstm_gates(g_ref, gbias_ref[d], tri_ref, rev)
        for h in range(ML_HEADS):
            bc = jnp.broadcast_to(b_col[:, h:h + 1], (L, HEAD_DIM))
            ic = jnp.broadcast_to(i_col[:, h:h + 1], (L, HEAD_DIM))
            chains.append(dict(
                sl=slice(h * HEAD_DIM, (h + 1) * HEAD_DIM), rev=rev, q_ref=q_ref, k_ref=k_ref, v_ref=v_ref,
                c_ref=c_ref, s_ref=s_ref, o_ref=o_ref, cn_ref=cn_ref.at[h], m_ref=m_ref.at[h],
                log_i=log_i[h:h + 1, :], b_row=b_row[h:h + 1, :], b_col=bc, i_col=ic))

    for c in chains:
        cos, sin = c['c_ref'][...], c['s_ref'][...]
        c['qc'] = _rope(c['q_ref'][:, c['sl']], cos, sin, swap).astype(BF16)
        kc = _rope(c['k_ref'][:, c['sl']] * (HEAD_DIM ** -0.5), cos, sin, swap)
        c['kc'] = kc
        c['s'] = _dot_nt(c['qc'], kc.astype(BF16))
    for c in chains:
        b_row = c['b_row']
        c['b_end'] = b_row[:, 0:1] if c['rev'] else b_row[:, L - 1:L]
        tri = (si >= ti) if c['rev'] else (si <= ti)
        c['m_prev'] = c['m_ref'][0:1, 0:1]
        bc2 = jnp.concatenate([c['b_col'], c['b_col']], axis=-1)
        dmat = jnp.where(tri, bc2 + (c['log_i'] - b_row), NEG_INF)
        inter = c['b_col'] + c['m_prev']
        m_t = jnp.maximum(inter, jnp.max(dmat, axis=-1, keepdims=True))
        c['m_t'] = m_t
        c['w_inter'] = jnp.exp(inter - m_t)
        c['p'] = (jnp.exp(dmat - jnp.concatenate([m_t, m_t], axis=-1)) * c['s']).astype(BF16)
    for c in chains:
        c['v_ext'] = jnp.concatenate([c['v_ref'][:, c['sl']].astype(BF16), ones], axis=-1)
        c['cn'] = c['cn_ref'][...]
        w2 = jnp.concatenate([c['w_inter'], c['w_inter']], axis=-1)
        acc = _dot(c['p'], c['v_ext']) + w2 * _dot(c['qc'], c['cn'].astype(BF16))
        den = acc[:, HEAD_DIM:]
        c['o_ref'][:, c['sl']] = acc[:, :HEAD_DIM] / jnp.maximum(jnp.abs(den), jnp.exp(-c['m_t']))
    for c in chains:
        e_row = c['b_end'] + (c['log_i'] - c['b_row'])
        m_new = jnp.maximum(c['b_end'] + c['m_prev'], jnp.max(e_row, axis=-1, keepdims=True))
        w_old = jnp.exp(c['b_end'] + c['m_prev'] - m_new)
        w_s = jnp.exp(c['b_end'] - c['b_col'] + c['i_col'] - m_new)
        c['cn_ref'][...] = w_old * c['cn'] + lax.dot_general(
            (w_s * c['kc']).astype(BF16), c['v_ext'], (((0,), (0,)), ((), ())), preferred_element_type=F32)
        c['m_ref'][...] = jnp.broadcast_to(m_new, c['m_ref'].shape)


def _mlstm(proj, gates_t, gbias, cos, sin, tri):
    width = ML_HEADS * HEAD_DIM

    def spec(col, rev):
        return pl.BlockSpec((CHUNK, width), lambda b, s: (_chunk_block(b, s, rev), col))

    def gate_spec(rev):
        d = 1 if rev else 0
        return pl.BlockSpec((1, 2, ML_HEADS, CHUNK), lambda b, s: (d, 0, 0, _chunk_block(b, s, rev)))

    def rope_spec(rev):
        def idx(b, s):
            lat = _chunk_block(b, s, rev) - b * N_LAT_CHUNKS
            return (jnp.where(s < N_CTX_CHUNKS, N_LAT_CHUNKS, lat), 0)
        return pl.BlockSpec((CHUNK, LANES), idx)

    out_spec = lambda rev: pl.BlockSpec((CHUNK, width), lambda b, s: (_chunk_block(b, s, rev), 0))
    shape = jax.ShapeDtypeStruct((N_ALL, width), F32)
    per_dir = lambda rev: [spec(8, rev), spec(9, rev), spec(10, rev), gate_spec(rev), rope_spec(rev), rope_spec(rev)]
    args_dir = [proj, proj, proj, gates_t, cos, sin]
    return pl.pallas_call(
        _mlstm_kernel,
        grid=(BATCH, N_STEPS),
        in_specs=per_dir(False) + per_dir(True) + [
            pl.BlockSpec((2, 2, ML_HEADS, LANES), lambda b, s: (0, 0, 0, 0)),
            pl.BlockSpec((3, CHUNK, CHUNK), lambda b, s: (0, 0, 0)),
            pl.BlockSpec((HEAD_DIM, HEAD_DIM), lambda b, s: (0, 0)),
        ],
        out_specs=[out_spec(False), out_spec(True)],
        out_shape=[shape, shape],
        scratch_shapes=[pltpu.VMEM((ML_HEADS, HEAD_DIM, 2 * HEAD_DIM), F32),
                        pltpu.VMEM((ML_HEADS, HEAD_DIM, 2 * HEAD_DIM), F32),
                        pltpu.VMEM((ML_HEADS, 8, LANES), F32), pltpu.VMEM((ML_HEADS, 8, LANES), F32)],
        compiler_params=_params("parallel", "arbitrary"),
        name="mlstm",
    )(*args_dir, *args_dir, gbias, tri, _rope_swap())


def _head_rms(x, w):
    parts = []
    for hh in range(x.shape[-1] // HEAD_DIM):
        xs = x[:, hh * HEAD_DIM:(hh + 1) * HEAD_DIM]
        parts.append(xs * lax.rsqrt(jnp.mean(xs * xs, axis=-1, keepdims=True) + EPS))
    return jnp.concatenate(parts, axis=-1) * w


def _merge_kernel(na_ref, nac_ref, hgf_ref, hgb_ref, mlf_ref, mlb_ref, hgg_ref, mlo_ref, bg0_ref, bg1_ref, bg2_ref,
                  hl_ref, hc_ref, mod_ref, hgw_ref, mlw_ref, wb_ref, wo_ref, o_ref):
    hg = _head_rms(hgf_ref[...] + hgb_ref[...], hgw_ref[...]) * _silu(hgg_ref[...])
    ml = _sigmoid(mlo_ref[...]) * _head_rms(mlf_ref[...] + mlb_ref[...], mlw_ref[...])
    is_ctx = pl.program_id(0) >= N_LAT // TM_MERGE
    na = jnp.where(is_ctx, nac_ref[...], na_ref[...])
    y2 = None
    for bg_ref, branch, i in ((bg0_ref, na, 0), (bg1_ref, hg, 1), (bg2_ref, ml, 2)):
        p = _dot(branch.astype(BF16), wb_ref[i])
        term = jnp.tanh(0.5 * bg_ref[...]) * p + p
        y2 = term if y2 is None else y2 + term
    h = jnp.where(is_ctx, hc_ref[...], hl_ref[...])
    o_ref[...] = h + (0.5 * mod_ref[0, 2:3, :]) * _dot(y2.astype(BF16), wo_ref[...])


def _merge(n_rows, na, na_ctx, hgf, hgb, mlf, mlb, proj, h_lat, h_ctx, ctx_row0, mod5, hg_w, ml_w, w_branch, w_out):
    tm = TM_MERGE
    tiles_per_batch = SEQ // tm
    n_lat_tiles = N_LAT // tm
    ctx_tile0 = ctx_row0 // tm
    row = lambda w, c: pl.BlockSpec((tm, w), lambda i: (i, c))
    const = lambda shape: pl.BlockSpec(shape, lambda i: (0,) * len(shape))
    bg0 = BG_OFF // D_MODEL
    return pl.pallas_call(
        _merge_kernel,
        grid=(n_rows // tm,),
        in_specs=[
            pl.BlockSpec((tm, 512), lambda i: (jnp.minimum(i, n_lat_tiles - 1), 0)),
            pl.BlockSpec((tm, 512), lambda i: (jnp.maximum(i - n_lat_tiles, 0), 0)),
            row(512, 0), row(512, 0), row(512, 0), row(512, 0),
            row(512, 7), row(512, 11),
            row(D_MODEL, bg0), row(D_MODEL, bg0 + 1), row(D_MODEL, bg0 + 2),
            pl.BlockSpec((tm, D_MODEL), lambda i: (jnp.minimum(i, n_lat_tiles - 1), 0)),
            pl.BlockSpec((tm, D_MODEL), lambda i: (ctx_tile0 + jnp.maximum(i - n_lat_tiles, 0), 0)),
            pl.BlockSpec((1, 6, D_MODEL), lambda i: (jnp.minimum(i // tiles_per_batch, BATCH), 0, 0)),
            const((1, 512)), const((1, 512)),
            const((3, BRANCH_WIDTH, D_MODEL)), const((D_MODEL, D_MODEL)),
        ],
        out_specs=row(D_MODEL, 0),
        out_shape=jax.ShapeDtypeStruct((n_rows, D_MODEL), F32),
        compiler_params=_params("parallel"),
        name="merge",
    )(na, na_ctx, hgf, hgb, mlf, mlb, proj, proj, proj, proj, proj, h_lat, h_ctx, mod5, hg_w, ml_w, w_branch, w_out)


def _ffn_kernel(h_ref, nw_ref, mod_ref, wa_ref, wu_ref, wd_ref, o_ref):
    h = h_ref[...]
    f = _norm_mod(h, nw_ref[...], mod_ref[0, 3:4, :], mod_ref[0, 4:5, :]).astype(BF16)
    g = _silu(_dot(f, wa_ref[...])) * _dot(f, wu_ref[...])
    o_ref[...] = h + mod_ref[0, 5:6, :] * _dot(g.astype(BF16), wd_ref[...])


def _ffn(h_all, nw, mod5, w_up, w_down):
    n_rows = h_all.shape[0]
    tiles_per_batch = SEQ // TM_FFN
    resident = pl.Buffered(1)
    return pl.pallas_call(
        _ffn_kernel,
        grid=(n_rows // TM_FFN,),
        in_specs=[
            pl.BlockSpec((TM_FFN, D_MODEL), lambda i: (i, 0)),
            pl.BlockSpec((1, D_MODEL), lambda i: (0, 0)),
            pl.BlockSpec((1, 6, D_MODEL), lambda i: (i // tiles_per_batch, 0, 0)),
            pl.BlockSpec((D_MODEL, FFN_DIM), lambda i: (0, 0), pipeline_mode=resident),
            pl.BlockSpec((D_MODEL, FFN_DIM), lambda i: (0, 1), pipeline_mode=resident),
            pl.BlockSpec((FFN_DIM, D_MODEL), lambda i: (0, 0), pipeline_mode=resident),
        ],
        out_specs=pl.BlockSpec((TM_FFN, D_MODEL), lambda i: (i, 0)),
        out_shape=jax.ShapeDtypeStruct((n_rows, D_MODEL), F32),
        compiler_params=_params("parallel"),
        name="ffn",
    )(h_all, nw, mod5, w_up, w_up, w_down)


def _router_kernel(h_ref, nw_ref, mod_ref, wr_ref, f_ref, r_ref):
    f = _norm_mod(h_ref[...], nw_ref[...], mod_ref[0, 3:4, :], mod_ref[0, 4:5, :])
    f_ref[...] = f
    logits = jnp.dot(f, wr_ref[...], preferred_element_type=F32, precision=lax.Precision.HIGHEST)
    lane = lax.broadcasted_iota(jnp.int32, logits.shape, 1)
    logits = jnp.where(lane < N_EXPERTS, logits, -jnp.inf)
    m1 = jnp.max(logits, axis=-1, keepdims=True)
    i1 = jnp.min(jnp.where(logits == m1, lane, LANES), axis=-1, keepdims=True)
    rest = jnp.where(lane == i1, -jnp.inf, logits)
    m2 = jnp.max(rest, axis=-1, keepdims=True)
    i2 = jnp.min(jnp.where(rest == m2, lane, LANES), axis=-1, keepdims=True)
    e2 = jnp.exp(m2 - m1)
    w1 = 1.0 / (1.0 + e2)
    w2 = e2 / (1.0 + e2)
    r_ref[...] = jnp.where(lane == 0, i1.astype(F32),
                           jnp.where(lane == 1, i2.astype(F32),
                                     jnp.where(lane == 2, w1, jnp.where(lane == 3, w2, 0.0))))


def _router(h_lat, nw, mod5, w_router_pad):
    tm = 512
    tiles_per_batch = SEQ // tm
    return pl.pallas_call(
        _router_kernel,
        grid=(N_LAT // tm,),
        in_specs=[
            pl.BlockSpec((tm, D_MODEL), lambda i: (i, 0)),
            pl.BlockSpec((1, D_MODEL), lambda i: (0, 0)),
            pl.BlockSpec((1, 6, D_MODEL), lambda i: (i // tiles_per_batch, 0, 0)),
            pl.BlockSpec((D_MODEL, LANES), lambda i: (0, 0)),
        ],
        out_specs=[pl.BlockSpec((tm, D_MODEL), lambda i: (i, 0)), pl.BlockSpec((tm, LANES), lambda i: (i, 0))],
        out_shape=[jax.ShapeDtypeStruct((N_LAT, D_MODEL), F32), jax.ShapeDtypeStruct((N_LAT, LANES), F32)],
        compiler_params=_params("parallel"),
        name="router",
    )(h_lat, nw, mod5, w_router_pad)


def _moe_kernel(be_ref, nused_ref, nvalid_ref, code_ref, f_hbm, wa_ref, wu_ref, wd_ref, y_hbm,
                xbuf, x16, acc, ybuf, sem_in, sem_out):
    i = pl.program_id(0)
    j = pl.program_id(1)
    last_j = pl.num_programs(1) - 1
    n_used = nused_ref[0]
    active = i < n_used
    slot = i % 2

    def start_gather(blk, buf):
        def body(r, c):
            tok = jnp.maximum(code_ref[blk * TM_MOE + r], 0) >> 1
            pltpu.make_async_copy(f_hbm.at[pl.ds(tok, 1)], xbuf.at[buf, pl.ds(r, 1)],
                                  sem_in.at[buf]).start(priority=GATHER_DMA_PRIORITY)
            return c
        lax.fori_loop(0, TM_MOE, body, 0, unroll=8)

    def wait_gather(buf):
        pltpu.make_async_copy(f_hbm.at[pl.ds(0, TM_MOE)], xbuf.at[buf], sem_in.at[buf]).wait()

    def start_scatter(blk):
        def body(r, c):
            code = code_ref[blk * TM_MOE + r]
            dst = (code & 1) * N_LAT + (code >> 1)
            pltpu.make_async_copy(ybuf.at[pl.ds(r, 1)], y_hbm.at[pl.ds(dst, 1)], sem_out).start()
            return c
        lax.fori_loop(0, nvalid_ref[blk], body, 0)

    def wait_scatter(blk):
        n = nvalid_ref[blk]
        p = TM_MOE
        while p >= 8:
            @pl.when((n & p) != 0)
            def _(p=p):
                pltpu.make_async_copy(ybuf.at[pl.ds(0, p)], y_hbm.at[pl.ds(0, p)], sem_out).wait()
            p //= 2

        def one(r, c):
            pltpu.make_async_copy(ybuf.at[pl.ds(0, 1)], y_hbm.at[pl.ds(0, 1)], sem_out).wait()
            return c
        lax.fori_loop(0, n & 7, one, 0)

    def prefetch_rows(first, count):
        for r in range(first, first + count):
            tok = jnp.maximum(code_ref[(i + 1) * TM_MOE + r], 0) >> 1
            pltpu.make_async_copy(f_hbm.at[pl.ds(tok, 1)], xbuf.at[1 - slot, pl.ds(r, 1)],
                                  sem_in.at[1 - slot]).start(priority=GATHER_DMA_PRIORITY)

    def partial_out():
        x = x16[...]
        g = _silu(_dot(x, wa_ref[0])) * _dot(x, wu_ref[0])
        return _dot(g.astype(BF16), wd_ref[0])

    @pl.when(jnp.logical_and(active, j == 0))
    def _():
        @pl.when(i == 0)
        def _():
            start_gather(0, 0)

        wait_gather(slot)
        x16[...] = xbuf[slot].astype(BF16)

    @pl.when(jnp.logical_and(active, j == 0))
    def _():
        prefetch_rows(0, TM_MOE // 2)
        acc[...] = partial_out()

    @pl.when(jnp.logical_and(active, j == last_j))
    def _():
        @pl.when(i > 0)
        def _():
            wait_scatter(i - 1)

    @pl.when(jnp.logical_and(active, j == last_j))
    def _():
        prefetch_rows(TM_MOE // 2, TM_MOE // 2)
        ybuf[...] = acc[...] + partial_out()

    @pl.when(jnp.logical_and(active, j == last_j))
    def _():
        start_scatter(i)

        @pl.when(i == n_used - 1)
        def _():
            wait_scatter(i)
            wait_gather(1 - slot)


def _moe_experts(block_e, n_used, n_valid, codes, f_lat, w_up, w_down):
    nj = EXPERT_DIM // TH_MOE
    assert nj == 2 and nj * TH_MOE == EXPERT_DIM

    def jj(i, j, nu):
        return jnp.where(i < nu[0], j, nj - 1)

    grid_spec = pltpu.PrefetchScalarGridSpec(
        num_scalar_prefetch=4,
        grid=(N_MOE_BLOCKS, nj),
        in_specs=[
            pl.BlockSpec(memory_space=pl.ANY),
            pl.BlockSpec((1, D_MODEL, TH_MOE), lambda i, j, be, nu, nv, cd: (be[i], 0, jj(i, j, nu))),
            pl.BlockSpec((1, D_MODEL, TH_MOE), lambda i, j, be, nu, nv, cd: (be[i], 0, nj + jj(i, j, nu))),
            pl.BlockSpec((1, TH_MOE, D_MODEL), lambda i, j, be, nu, nv, cd: (be[i], jj(i, j, nu), 0)),
        ],
        out_specs=pl.BlockSpec(memory_space=pl.ANY),
        scratch_shapes=[
            pltpu.VMEM((2, TM_MOE, D_MODEL), F32), pltpu.VMEM((TM_MOE, D_MODEL), BF16),
            pltpu.VMEM((TM_MOE, D_MODEL), F32), pltpu.VMEM((TM_MOE, D_MODEL), F32),
            pltpu.SemaphoreType.DMA((2,)), pltpu.SemaphoreType.DMA(()),
        ],
    )
    return pl.pallas_call(
        _moe_kernel,
        grid_spec=grid_spec,
        out_shape=jax.ShapeDtypeStruct((2 * N_LAT, D_MODEL), F32),
        compiler_params=_params("arbitrary", "arbitrary"),
        name="moe_experts",
    )(block_e, n_used, n_valid, codes, f_lat, w_up, w_up, w_down)


def _combine_kernel(h_ref, y1_ref, y2_ref, r_ref, mod_ref, fw_ref, o_ref):
    r = r_ref[...]
    y = r[:, 2:3] * y1_ref[...] + r[:, 3:4] * y2_ref[...]
    h = h_ref[...] + mod_ref[0, 5:6, :] * y
    o_ref[...] = h * lax.rsqrt(jnp.mean(h * h, axis=-1, keepdims=True) + EPS) * fw_ref[...]


def _combine_final(h_lat, y, route, mod5, final_w):
    tm = 512
    tiles_per_batch = SEQ // tm
    return pl.pallas_call(
        _combine_kernel,
        grid=(N_LAT // tm,),
        in_specs=[
            pl.BlockSpec((tm, D_MODEL), lambda i: (i, 0)),
            pl.BlockSpec((tm, D_MODEL), lambda i: (i, 0)),
            pl.BlockSpec((tm, D_MODEL), lambda i: (N_LAT // tm + i, 0)),
            pl.BlockSpec((tm, LANES), lambda i: (i, 0)),
            pl.BlockSpec((1, 6, D_MODEL), lambda i: (i // tiles_per_batch, 0, 0)),
            pl.BlockSpec((1, D_MODEL), lambda i: (0, 0)),
        ],
        out_specs=pl.BlockSpec((tm, D_MODEL), lambda i: (i, 0)),
        out_shape=jax.ShapeDtypeStruct((N_LAT, D_MODEL), F32),
        compiler_params=_params("parallel"),
        name="combine_final",
    )(h_lat, y, y, route, mod5, final_w)


def _moe_plan(route):
    e12 = route[:, 0:2].astype(jnp.int32)
    onehot = (e12[:, :, None] == jnp.arange(N_EXPERTS, dtype=jnp.int32)).astype(jnp.int32).sum(axis=1)
    before = jnp.cumsum(onehot, axis=0) - onehot
    counts = jnp.sum(onehot, axis=0)
    nblk = (counts + TM_MOE - 1) // TM_MOE
    blk_end = jnp.cumsum(nblk)
    slot0 = (blk_end - nblk) * TM_MOE
    rank = jnp.take_along_axis(before, e12, axis=1)
    dest = slot0[e12] + rank
    codes = jnp.full((N_SLOTS,), -1, jnp.int32).at[dest.reshape(-1)].set(jnp.arange(2 * N_LAT, dtype=jnp.int32))
    n_used = blk_end[-1]
    blocks = jnp.minimum(jnp.arange(N_MOE_BLOCKS, dtype=jnp.int32), n_used - 1)
    block_e = jnp.minimum(jnp.sum((blocks[:, None] >= blk_end[None, :]).astype(jnp.int32), axis=1), N_EXPERTS - 1)
    n_valid = jnp.sum((codes >= 0).astype(jnp.int32).reshape(N_MOE_BLOCKS, TM_MOE), axis=1)
    return block_e, n_used.reshape(1).astype(jnp.int32), n_valid, codes


def kernel(x, c, ctx, c_ctx, mod_w, mod_b, norm1_w, w_in, na_rpb, hg_lb, hg_norm_w, ml_gate_b, ml_norm_w,
           w_branch, w_out, norm2_w, ffn_w_up, ffn_w_down, moe_router, moe_w_up, moe_w_down, final_norm_w):
    h_lat, h_ctx, ctx_row0 = x.reshape(N_LAT, D_MODEL), ctx.reshape(N_CTX, D_MODEL), 0
    c8 = jnp.concatenate([c, c_ctx[None, :], jnp.zeros((3, D_MODEL), F32)], axis=0)
    mods = _modulation(c8, mod_w, mod_b).reshape(DEPTH, 8, 6, D_MODEL)

    lb_p = jax.nn.softmax(hg_lb.astype(F32), axis=0)
    hg_lower = jnp.cumsum(lb_p, axis=0) - lb_p[0]
    tri = _tri_consts()
    hg_tri, hg_masks = _hgrn_consts()
    cos, sin = _rope_tables()

    out = None
    for layer in range(DEPTH):
        last = layer == DEPTH - 1
        mod5 = mods[layer, :5]
        wl = w_in[layer]
        w_main = jnp.concatenate([wl[:, :MAIN_W], wl[:, MAIN_W + ML_GATE_COLS:]], axis=1).astype(BF16)
        w_gates = jnp.pad(wl[:, MAIN_W:MAIN_W + ML_GATE_COLS], ((0, 0), (0, LANES - ML_GATE_COLS))).astype(BF16)
        proj, gates = _inproj(h_lat, h_ctx, ctx_row0, norm1_w[layer][None, :], mod5, w_main, w_gates)

        na = _na_latent(proj, _na_bias_tables(na_rpb[layer]))
        na_ctx = na if last else _ctx_attention(proj)
        hgf, hgb = _hgrn2(proj, hg_lower[layer], hg_tri, hg_masks)
        gates_t = gates[:, :ML_GATE_COLS].T.reshape(2, 2, ML_HEADS, N_ALL)
        gbias = jnp.broadcast_to(ml_gate_b[layer][..., None], (2, 2, ML_HEADS, LANES))
        mlf, mlb = _mlstm(proj, gates_t, gbias, cos, sin, tri)

        n_rows = N_LAT if last else N_ALL
        h_all = _merge(n_rows, na, na_ctx, hgf, hgb, mlf, mlb, proj, h_lat, h_ctx, ctx_row0, mod5,
                       hg_norm_w[layer][None, :], ml_norm_w[layer][None, :],
                       w_branch[layer].astype(BF16), w_out[layer].astype(BF16))
        i = layer // 2
        if layer % 2 == 0:
            h_all = _ffn(h_all, norm2_w[layer][None, :], mod5, ffn_w_up[i].astype(BF16), ffn_w_down[i].astype(BF16))
            h_lat, h_ctx, ctx_row0 = h_all, h_all, N_LAT
            if last:
                raise NotImplementedError("final norm after a dense last layer")
        else:
            if not last:
                raise NotImplementedError("MoE on the context stream")
            w_router_pad = jnp.pad(moe_router[i], ((0, 0), (0, LANES - N_EXPERTS)))
            f_lat, route = _router(h_all, norm2_w[layer][None, :], mod5, w_router_pad)
            block_e, n_used, n_valid, codes = _moe_plan(route)
            y = _moe_experts(block_e, n_used, n_valid, codes, f_lat,
                             moe_w_up[i].astype(BF16), moe_w_down[i].astype(BF16))
            out = _combine_final(h_all, y, route, mod5, final_norm_w[None, :])
    return out.reshape(BATCH, SEQ, D_MODEL)
```

```python
import functools

import numpy as np
import jax
import jax.numpy as jnp
from jax import lax
from jax.experimental import pallas as pl
from jax.experimental.pallas import tpu as pltpu

F32 = jnp.float32
BF16 = jnp.bfloat16

D_MODEL = 1024
BATCH = 4
SEQ = 4096
DEPTH = 2
GRID_W = 64
GRID_H = SEQ // GRID_W
CTX_LEN = 256
EPS = 1e-6
NEG_INF = -1e30
F_FLOOR = 1e-30
NA_HEADS = 8
NA_HEAD_DIM = 64
NA_WIN_ROWS = 8
NA_WIN_COLS = 16
HG_HEADS = 4
ML_HEADS = 4
HEAD_DIM = 128
ML_GATE_COLS = 16
ROPE_BASE = 10000.0
BRANCH_WIDTH = 512
FFN_DIM = 2816
N_EXPERTS = 8
EXPERT_DIM = 3584

N_LAT = BATCH * SEQ
N_CTX = BATCH * CTX_LEN
N_ALL = N_LAT + N_CTX

LANES = 128
VMEM_LIMIT = 56 * 1024 * 1024

MAIN_W = 12 * 512
BG_OFF = MAIN_W
PROJ_W = MAIN_W + 3 * D_MODEL

TM_PROJ = 1024
TN_PROJ = PROJ_W // 4
TM_MERGE = 256
TM_FFN = 512
NA_SUBS = 2
NA_QROWS = 4
NA_KROWS = NA_QROWS + NA_WIN_ROWS
CHUNK = 256
TM_MOE = 512
TH_MOE = EXPERT_DIM // 2
GATHER_DMA_PRIORITY = 1
N_MOE_BLOCKS = -(-(2 * N_LAT + N_EXPERTS * (TM_MOE - 1)) // TM_MOE)
N_SLOTS = N_MOE_BLOCKS * TM_MOE


def _params(*sem):
    return pltpu.CompilerParams(dimension_semantics=sem, vmem_limit_bytes=VMEM_LIMIT)


def _sigmoid(x):
    return 0.5 * jnp.tanh(0.5 * x) + 0.5


def _silu(x):
    return x * _sigmoid(x)


def _dot(a, b):
    return jnp.dot(a, b, preferred_element_type=F32)


def _dot_nt(a, b):
    return lax.dot_general(a, b, (((1,), (1,)), ((), ())), preferred_element_type=F32)


def _split3(x):
    hi = x.astype(BF16)
    r = x - hi.astype(F32)
    mid = r.astype(BF16)
    lo = (r - mid.astype(F32)).astype(BF16)
    return hi, mid, lo


def _sel_dot(sel, x):
    hi, mid, lo = _split3(x)
    return _dot(sel, lo) + _dot(sel, mid) + _dot(sel, hi)


def _sel_dot_nt(sel, x):
    hi, mid, lo = _split3(x)
    return _dot_nt(sel, lo) + _dot_nt(sel, mid) + _dot_nt(sel, hi)


def _dot_sel(x, sel):
    hi, mid, lo = _split3(x)
    return _dot(lo, sel) + _dot(mid, sel) + _dot(hi, sel)


def _norm_mod(x, nw, shift, scale):
    y = x * lax.rsqrt(jnp.mean(x * x, axis=-1, keepdims=True) + EPS) * nw
    return y * (1.0 + scale) + shift


def _mod_kernel(c_ref, w_ref, b_ref, o_ref):
    s = _silu(c_ref[...])
    o_ref[0] = jnp.dot(s, w_ref[0], preferred_element_type=F32, precision=lax.Precision.HIGHEST) + b_ref[0]


def _modulation(c8, mod_w, mod_b):
    tn = 1536
    return pl.pallas_call(
        _mod_kernel,
        grid=(DEPTH, 6 * D_MODEL // tn),
        in_specs=[
            pl.BlockSpec((8, D_MODEL), lambda l, j: (0, 0)),
            pl.BlockSpec((1, D_MODEL, tn), lambda l, j: (l, 0, j)),
            pl.BlockSpec((1, 1, tn), lambda l, j: (l, 0, j)),
        ],
        out_specs=pl.BlockSpec((1, 8, tn), lambda l, j: (l, 0, j)),
        out_shape=jax.ShapeDtypeStruct((DEPTH, 8, 6 * D_MODEL), F32),
        compiler_params=_params("parallel", "parallel"),
        name="modulation",
    )(c8, mod_w, mod_b.reshape(DEPTH, 1, 6 * D_MODEL))


def _inproj_kernel(hl_ref, hc_ref, nw_ref, mod_ref, w_ref, wg_ref, o_ref, g_ref, a_scr):
    @pl.when(pl.program_id(1) == 0)
    def _():
        h = jnp.where(pl.program_id(0) >= N_LAT // TM_PROJ, hc_ref[...], hl_ref[...])
        a = _norm_mod(h, nw_ref[...], mod_ref[0, 0:1, :], mod_ref[0, 1:2, :])
        a_scr[...] = a.astype(BF16)
        g_ref[...] = _dot(a_scr[...], wg_ref[...])

    o_ref[...] = _dot(a_scr[...], w_ref[...])


def _inproj(h_lat, h_ctx, ctx_row0, nw, mod5, w_main, w_gates):
    tiles_per_batch = SEQ // TM_PROJ
    n_lat_tiles = N_LAT // TM_PROJ
    ctx_tile0 = ctx_row0 // TM_PROJ
    return pl.pallas_call(
        _inproj_kernel,
        grid=(N_ALL // TM_PROJ, PROJ_W // TN_PROJ),
        in_specs=[
            pl.BlockSpec((TM_PROJ, D_MODEL), lambda i, j: (jnp.minimum(i, n_lat_tiles - 1), 0)),
            pl.BlockSpec((TM_PROJ, D_MODEL), lambda i, j: (ctx_tile0 + jnp.maximum(i - n_lat_tiles, 0), 0)),
            pl.BlockSpec((1, D_MODEL), lambda i, j: (0, 0)),
            pl.BlockSpec((1, 6, D_MODEL), lambda i, j: (i // tiles_per_batch, 0, 0)),
            pl.BlockSpec((D_MODEL, TN_PROJ), lambda i, j: (0, j)),
            pl.BlockSpec((D_MODEL, LANES), lambda i, j: (0, 0)),
        ],
        out_specs=[pl.BlockSpec((TM_PROJ, TN_PROJ), lambda i, j: (i, j)),
                   pl.BlockSpec((TM_PROJ, LANES), lambda i, j: (i, 0))],
        out_shape=[jax.ShapeDtypeStruct((N_ALL, PROJ_W), F32), jax.ShapeDtypeStruct((N_ALL, LANES), F32)],
        scratch_shapes=[pltpu.VMEM((TM_PROJ, D_MODEL), BF16)],
        compiler_params=_params("parallel", "arbitrary"),
        name="inproj",
    )(h_lat, h_ctx, nw, mod5, w_main, w_gates)


N_DR = 2 * NA_WIN_ROWS - 1


def _na_bias_tables(rpb):
    qc = np.arange(GRID_W)[:, None]
    kc = np.arange(GRID_W)[None, :]
    dc = np.clip(kc - qc + NA_WIN_COLS - 1, 0, 2 * NA_WIN_COLS - 2)
    ws = np.clip(qc - NA_WIN_COLS // 2, 0, GRID_W - NA_WIN_COLS)
    col_ok = (kc >= ws) & (kc < ws + NA_WIN_COLS)
    onehot = ((dc[None] == np.arange(2 * NA_WIN_COLS - 1)[:, None, None]) & col_ok[None]).astype(np.float32)
    t = jnp.einsum('hrd,dqk->hrqk', rpb.astype(F32), jnp.asarray(onehot), precision=lax.Precision.HIGHEST)
    t = t + jnp.asarray(np.where(col_ok, 0.0, NEG_INF).astype(np.float32))
    tp = jnp.pad(t, ((0, 0), (1, 2), (0, 0), (0, 0)))
    return jnp.concatenate([tp[:, :N_DR + 2], tp[:, 1:]], axis=-1)


def _na_kernel(q_ref, k_ref, v_ref, kc_ref, vc_ref, tab_ref, o_ref):
    nq = NA_QROWS * GRID_W
    chains = []
    for sub in range(NA_SUBS):
        rows = slice(sub * nq, (sub + 1) * nq)
        chains.append(_na_sub_block((pl.program_id(2) * NA_SUBS + sub) * NA_QROWS, q_ref.at[rows], k_ref, v_ref,
                                    kc_ref, vc_ref, tab_ref, o_ref.at[rows]))
    _round_robin(chains)


def _na_sub_block(q0, q_ref, k_ref, v_ref, kc_ref, vc_ref, tab_ref, o_ref):
    k0 = jnp.clip(q0 - NA_WIN_ROWS // 2, 0, GRID_H - NA_KROWS)
    start = pl.multiple_of(k0 * GRID_W, GRID_W)
    nk = NA_KROWS * GRID_W
    lane = lax.broadcasted_iota(jnp.int32, (GRID_W, LANES), 1)

    tab_idx, penalty = [], []
    for qr in range(NA_QROWS):
        r = q0 + qr
        r0 = jnp.clip(r - NA_WIN_ROWS // 2, 0, GRID_H - NA_WIN_ROWS)
        idx_row, pen_row = [], []
        for j in range(NA_KROWS // 2):
            kra = k0 + 2 * j
            pa = jnp.where(jnp.logical_and(kra >= r0, kra < r0 + NA_WIN_ROWS), 0.0, NEG_INF)
            pb = jnp.where(jnp.logical_and(kra + 1 >= r0, kra + 1 < r0 + NA_WIN_ROWS), 0.0, NEG_INF)
            idx_row.append(jnp.clip(kra - r + NA_WIN_ROWS, 0, N_DR + 1))
            pen_row.append(jnp.where(lane < GRID_W, pa, pb))
        tab_idx.append(idx_row)
        penalty.append(pen_row)

    heads = [dict(hh=hh, sl=slice(hh * NA_HEAD_DIM, (hh + 1) * NA_HEAD_DIM)) for hh in range(2)]
    for c in heads:
        sl = c['sl']
        q = (q_ref[:, sl] * (NA_HEAD_DIM ** -0.5)).astype(BF16)
        c['v'] = v_ref[pl.ds(start, nk), sl].astype(BF16)
        c['vc'] = vc_ref[:, sl].astype(BF16)
        c['s_raw'] = _dot_nt(q, k_ref[pl.ds(start, nk), sl].astype(BF16))
        c['s_ctx'] = _dot_nt(q, kc_ref[:, sl].astype(BF16))
    yield
    for c in heads:
        rows = []
        for qr in range(NA_QROWS):
            cols = []
            for j in range(NA_KROWS // 2):
                s_blk = c['s_raw'][qr * GRID_W:(qr + 1) * GRID_W, j * LANES:(j + 1) * LANES]
                cols.append(s_blk + (tab_ref[c['hh'], tab_idx[qr][j]] + penalty[qr][j]))
            rows.append(jnp.concatenate(cols, axis=1))
        c['s_loc'] = jnp.concatenate(rows, axis=0)
        c['m'] = jnp.maximum(jnp.max(c['s_loc'], axis=-1, keepdims=True), jnp.max(c['s_ctx'], axis=-1, keepdims=True))
    yield
    for c in heads:
        p_loc = jnp.exp(c['s_loc'] - c['m'])
        p_ctx = jnp.exp(c['s_ctx'] - c['m'])
        c['den'] = jnp.sum(p_loc, axis=-1, keepdims=True) + jnp.sum(p_ctx, axis=-1, keepdims=True)
        c['p_loc'] = p_loc.astype(BF16)
        c['p_ctx'] = p_ctx.astype(BF16)
    yield
    outs = [(_dot(c['p_loc'], c['v']) + _dot(c['p_ctx'], c['vc'])) / c['den'] for c in heads]
    o_ref[...] = jnp.concatenate(outs, axis=-1)


def _na_latent(proj, bias):
    nq = NA_SUBS * NA_QROWS * GRID_W
    nblk = GRID_H // (NA_SUBS * NA_QROWS)
    ctx_blk0 = N_LAT // CTX_LEN
    return pl.pallas_call(
        _na_kernel,
        grid=(BATCH, NA_HEADS // 2, nblk),
        in_specs=[
            pl.BlockSpec((nq, LANES), lambda b, hp, blk: (b * nblk + blk, hp)),
            pl.BlockSpec((SEQ, LANES), lambda b, hp, blk: (b, 4 + hp)),
            pl.BlockSpec((SEQ, LANES), lambda b, hp, blk: (b, 8 + hp)),
            pl.BlockSpec((CTX_LEN, LANES), lambda b, hp, blk: (ctx_blk0 + b, 4 + hp)),
            pl.BlockSpec((CTX_LEN, LANES), lambda b, hp, blk: (ctx_blk0 + b, 8 + hp)),
            pl.BlockSpec((2, N_DR + 2, GRID_W, LANES), lambda b, hp, blk: (hp, 0, 0, 0)),
        ],
        out_specs=pl.BlockSpec((nq, LANES), lambda b, hp, blk: (b * nblk + blk, hp)),
        out_shape=jax.ShapeDtypeStruct((N_LAT, NA_HEADS * NA_HEAD_DIM), F32),
        compiler_params=_params("parallel", "parallel", "arbitrary"),
        name="na_latent",
    )(proj, proj, proj, proj, proj, bias)


def _ctx_attn_kernel(q_ref, k_ref, v_ref, o_ref):
    outs = []
    for hh in range(2):
        sl = slice(hh * NA_HEAD_DIM, (hh + 1) * NA_HEAD_DIM)
        q = (q_ref[:, sl] * (NA_HEAD_DIM ** -0.5)).astype(BF16)
        s = _dot_nt(q, k_ref[:, sl].astype(BF16))
        p = jnp.exp(s - jnp.max(s, axis=-1, keepdims=True))
        o = _dot(p.astype(BF16), v_ref[:, sl].astype(BF16))
        outs.append(o / jnp.sum(p, axis=-1, keepdims=True))
    o_ref[...] = jnp.concatenate(outs, axis=-1)


def _ctx_attention(proj):
    ctx_blk0 = N_LAT // CTX_LEN
    return pl.pallas_call(
        _ctx_attn_kernel,
        grid=(BATCH, NA_HEADS // 2),
        in_specs=[
            pl.BlockSpec((CTX_LEN, LANES), lambda b, hp: (ctx_blk0 + b, hp)),
            pl.BlockSpec((CTX_LEN, LANES), lambda b, hp: (ctx_blk0 + b, 4 + hp)),
            pl.BlockSpec((CTX_LEN, LANES), lambda b, hp: (ctx_blk0 + b, 8 + hp)),
        ],
        out_specs=pl.BlockSpec((CTX_LEN, LANES), lambda b, hp: (b, hp)),
        out_shape=jax.ShapeDtypeStruct((N_CTX, NA_HEADS * NA_HEAD_DIM), F32),
        compiler_params=_params("parallel", "parallel"),
        name="ctx_attention",
    )(proj, proj, proj)


N_CTX_CHUNKS = CTX_LEN // CHUNK
N_LAT_CHUNKS = SEQ // CHUNK
N_STEPS = N_CTX_CHUNKS + N_LAT_CHUNKS
SUB = 128
HG_CHUNK = 2 * SUB
HG_LEVELS = SUB.bit_length() - 1


def _chunk_block(b, s, rev, chunk=CHUNK):
    n_ctx, n_lat = CTX_LEN // chunk, SEQ // chunk
    c_ctx = (n_ctx - 1 - s) if rev else s
    c_lat = (n_lat - 1 - (s - n_ctx)) if rev else (s - n_ctx)
    ctx_blk = N_LAT // chunk + b * n_ctx + c_ctx
    lat_blk = b * n_lat + c_lat
    return jnp.where(s < n_ctx, ctx_blk, lat_blk)


def _tri_consts():
    i = np.arange(CHUNK)
    low = (i[None, :] <= i[:, None]).astype(np.float32)
    eye = np.eye(CHUNK, dtype=np.float32)
    return jnp.asarray(np.stack([low, low.T, eye]), BF16)


def _hgrn_consts():
    t = np.arange(SUB)[:, None]
    s = np.arange(SUB)[None, :]
    low = (s <= t).astype(np.float32)
    masks = np.zeros((2, HG_LEVELS + 1, SUB, SUB), np.float32)
    for l in range(HG_LEVELS):
        pair = ((t ^ s) >> l) == 1
        masks[0, l] = pair & (t > s)
        masks[1, l] = pair & (t < s)
    masks[:, HG_LEVELS] = (t == s)
    return jnp.asarray(np.stack([low, low.T]), BF16), jnp.asarray(masks)


def _hgrn_chain(q_ref, z_ref, v_ref, o_ref, sl, lb, st_ref, tri, masks_ref, rev):
    L, S = HG_CHUNK, SUB
    two = L == 2 * S
    z = z_ref[:, sl]
    v = v_ref[:, sl]
    e = jnp.exp(-jnp.abs(z))
    r = 1.0 / (1.0 + e)
    pos = z >= 0
    sig = jnp.where(pos, r, e * r)
    nsig = jnp.where(pos, e * r, r)
    k = (1.0 - lb) * nsig
    log2f = jnp.log2(jnp.maximum(lb + (1.0 - lb) * sig, F_FLOOR))
    q = _silu(q_ref[:, sl])
    yield

    g = _sel_dot(tri, log2f[:S])
    if two:
        g1 = _sel_dot(tri, log2f[S:])
        if rev:
            g = g + g1[0:1, :]
        else:
            g1 = g1 + g[S - 1:S, :]
        g = jnp.concatenate([g, g1], axis=0)
    row = lax.broadcasted_iota(jnp.int32, (L, HEAD_DIM), 0)

    st = st_ref[...]
    o_inter = _dot_nt((q * jnp.exp2(g)).astype(BF16), st.astype(BF16))
    yield

    nsub = L // S
    a = [jnp.zeros((S, S), F32) for _ in range(nsub)]
    cross = None
    q16 = q.astype(BF16)
    k16 = k.astype(BF16)
    bnd = g
    for l in range(HG_LEVELS + (1 if two else 0)):
        blk = 1 << l
        q_side = ((row & blk) == 0) if rev else ((row & blk) != 0)
        prev_end = pltpu.roll(bnd, (L - blk) if rev else blk, 0)
        w16 = jnp.exp2(jnp.where(q_side, g - prev_end, bnd - g)).astype(BF16)
        qb = q16 * w16
        kb = k16 * w16
        if blk < S:
            for c in range(nsub):
                a[c] = a[c] + masks_ref[l] * _dot_nt(qb[c * S:(c + 1) * S], kb[c * S:(c + 1) * S])
        elif rev:
            cross = _dot_nt(qb[:S], kb[S:])
        else:
            cross = _dot_nt(qb[S:], kb[:S])
        nxt = pltpu.roll(bnd, blk if rev else (L - blk), 0)
        bnd = jnp.where(q_side, bnd, nxt)
        yield
    v16 = v.astype(BF16)
    for c in range(nsub):
        a[c] = a[c] + masks_ref[HG_LEVELS] * _dot_nt(q16[c * S:(c + 1) * S], k16[c * S:(c + 1) * S])
    if not two:
        o_intra = _dot(a[0].astype(BF16), v16)
    elif rev:
        o_intra = jnp.concatenate([_dot(jnp.concatenate([a[0], cross], axis=1).astype(BF16), v16),
                                   _dot(a[1].astype(BF16), v16[S:])], axis=0)
    else:
        o_intra = jnp.concatenate([_dot(a[0].astype(BF16), v16[:S]),
                                   _dot(jnp.concatenate([cross, a[1]], axis=1).astype(BF16), v16)], axis=0)
    o_ref[:, sl] = o_inter + o_intra
    yield

    kd = k * jnp.exp2(bnd - g)
    st_ref[...] = jnp.exp2(bnd[0:1, :]) * st + lax.dot_general(
        v16, kd.astype(BF16), (((0,), (0,)), ((), ())), preferred_element_type=F32)


def _round_robin(chains):
    alive = list(chains)
    while alive:
        still = []
        for c in alive:
            try:
                next(c)
                still.append(c)
            except StopIteration:
                pass
        alive = still


def _hgrn_kernel(qf_ref, zf_ref, vf_ref, qb_ref, zb_ref, vb_ref, lb_ref, tri_ref, masks_ref,
                 of_ref, ob_ref, stf_ref, stb_ref):
    @pl.when(pl.program_id(1) == 0)
    def _():
        stf_ref[...] = jnp.zeros_like(stf_ref)
        stb_ref[...] = jnp.zeros_like(stb_ref)

    chains = []
    for h in range(HG_HEADS):
        sl = slice(h * HEAD_DIM, (h + 1) * HEAD_DIM)
        chains.append(_hgrn_chain(qf_ref, zf_ref, vf_ref, of_ref, sl, lb_ref[0:1, sl], stf_ref.at[h],
                                  tri_ref[0], masks_ref.at[0], False))
        chains.append(_hgrn_chain(qb_ref, zb_ref, vb_ref, ob_ref, sl, lb_ref[1:2, sl], stb_ref.at[h],
                                  tri_ref[1], masks_ref.at[1], True))
    _round_robin(chains)


def _hgrn2(proj, lower, tri, masks):
    width = HG_HEADS * HEAD_DIM

    def spec(col, rev):
        return pl.BlockSpec((HG_CHUNK, width), lambda b, s: (_chunk_block(b, s, rev, HG_CHUNK), col))

    out_spec = lambda rev: pl.BlockSpec((HG_CHUNK, width), lambda b, s: (_chunk_block(b, s, rev, HG_CHUNK), 0))
    shape = jax.ShapeDtypeStruct((N_ALL, width), F32)
    return pl.pallas_call(
        _hgrn_kernel,
        grid=(BATCH, (CTX_LEN + SEQ) // HG_CHUNK),
        in_specs=[
            spec(3, False), spec(4, False), spec(6, False),
            spec(3, True), spec(5, True), spec(6, True),
            pl.BlockSpec((2, width), lambda b, s: (0, 0)),
            pl.BlockSpec((2, SUB, SUB), lambda b, s: (0, 0, 0)),
            pl.BlockSpec((2, HG_LEVELS + 1, SUB, SUB), lambda b, s: (0, 0, 0, 0)),
        ],
        out_specs=[out_spec(False), out_spec(True)],
        out_shape=[shape, shape],
        scratch_shapes=[pltpu.VMEM((HG_HEADS, HEAD_DIM, HEAD_DIM), F32), pltpu.VMEM((HG_HEADS, HEAD_DIM, HEAD_DIM), F32)],
        compiler_params=_params("parallel", "arbitrary"),
        name="hgrn2",
    )(proj, proj, proj, proj, proj, proj, lower, tri, masks)


def _rope_tables():
    n_freq = HEAD_DIM // 4
    inv_freq = ROPE_BASE ** (-np.arange(n_freq, dtype=np.float64) / n_freq)
    t = np.arange(SEQ)
    ang_r = (t // GRID_W).astype(np.float64)[:, None] * inv_freq
    ang_c = (t % GRID_W).astype(np.float64)[:, None] * inv_freq
    cos = np.concatenate([np.cos(ang_r), np.cos(ang_r), np.cos(ang_c), np.cos(ang_c)], axis=-1)
    sin = np.concatenate([-np.sin(ang_r), np.sin(ang_r), -np.sin(ang_c), np.sin(ang_c)], axis=-1)
    cos = np.concatenate([cos, np.ones((CHUNK, HEAD_DIM))], axis=0)
    sin = np.concatenate([sin, np.zeros((CHUNK, HEAD_DIM))], axis=0)
    return jnp.asarray(cos, F32), jnp.asarray(sin, F32)


def _rope_swap():
    l = np.arange(HEAD_DIM)
    p = (l[:, None] == (l[None, :] ^ 32)).astype(np.float32)
    return jnp.asarray(np.concatenate([p, p], axis=0), BF16)


def _rope(x, cos, sin, swap):
    hi = x.astype(BF16)
    mid = (x - hi.astype(F32)).astype(BF16)
    partner = _dot(jnp.concatenate([hi, mid], axis=-1), swap)
    return x * cos + partner * sin


def _mlstm_gates(g_ref, gbias, tri_ref, rev):
    L = CHUNK
    log_i = g_ref[0, 0] + gbias[0][:, 0:1]
    xf = g_ref[0, 1] + gbias[1][:, 0:1]
    log_f = jnp.minimum(xf, 0.0) - jnp.log(1.0 + jnp.exp(-jnp.abs(xf)))
    r8 = jnp.concatenate([log_f, log_i], axis=0)
    low, up, eye = tri_ref[0], tri_ref[1], tri_ref[2]
    b_row = _dot_sel(r8, low if rev else up)[0:ML_HEADS, :]
    b_col = _sel_dot_nt(up if rev else low, r8)[:, 0:ML_HEADS]
    i_col = _sel_dot_nt(eye, r8)[:, ML_HEADS:2 * ML_HEADS]
    return log_i, b_row, b_col, i_col


def _mlstm_kernel(qf_ref, kf_ref, vf_ref, gf_ref, cf_ref, sf_ref,
                  qb_ref, kb_ref, vb_ref, gb_ref, cb_ref, sb_ref,
                  gbias_ref, tri_ref, swap_ref, of_ref, ob_ref, cnf_ref, cnb_ref, mf_ref, mb_ref):
    @pl.when(pl.program_id(1) == 0)
    def _():
        cnf_ref[...] = jnp.zeros_like(cnf_ref)
        cnb_ref[...] = jnp.zeros_like(cnb_ref)
        mf_ref[...] = jnp.zeros_like(mf_ref)
        mb_ref[...] = jnp.zeros_like(mb_ref)

    L = CHUNK
    ti = lax.broadcasted_iota(jnp.int32, (L, L), 0)
    si = lax.broadcasted_iota(jnp.int32, (L, L), 1)
    ones = jnp.ones((L, HEAD_DIM), BF16)
    swap = swap_ref[...]
    dirs = ((qf_ref, kf_ref, vf_ref, gf_ref, cf_ref, sf_ref, of_ref, cnf_ref, mf_ref, False),
            (qb_ref, kb_ref, vb_ref, gb_ref, cb_ref, sb_ref, ob_ref, cnb_ref, mb_ref, True))

    chains = []
    for d, (q_ref, k_ref, v_ref, g_ref, c_ref, s_ref, o_ref, cn_ref, m_ref, rev) in enumerate(dirs):
        log_i, b_row, b_col, i_col = _mlstm_gates(g_ref, gbias_ref[d], tri_ref, rev)
        for h in range(ML_HEADS):
            bc = jnp.broadcast_to(b_col[:, h:h + 1], (L, HEAD_DIM))
            ic = jnp.broadcast_to(i_col[:, h:h + 1], (L, HEAD_DIM))
            chains.append(dict(
                sl=slice(h * HEAD_DIM, (h + 1) * HEAD_DIM), rev=rev, q_ref=q_ref, k_ref=k_ref, v_ref=v_ref,
                c_ref=c_ref, s_ref=s_ref, o_ref=o_ref, cn_ref=cn_ref.at[h], m_ref=m_ref.at[h],
                log_i=log_i[h:h + 1, :], b_row=b_row[h:h + 1, :], b_col=bc, i_col=ic))

    for c in chains:
        cos, sin = c['c_ref'][...], c['s_ref'][...]
        c['qc'] = _rope(c['q_ref'][:, c['sl']], cos, sin, swap).astype(BF16)
        kc = _rope(c['k_ref'][:, c['sl']] * (HEAD_DIM ** -0.5), cos, sin, swap)
        c['kc'] = kc
        c['s'] = _dot_nt(c['qc'], kc.astype(BF16))
    for c in chains:
        b_row = c['b_row']
        c['b_end'] = b_row[:, 0:1] if c['rev'] else b_row[:, L - 1:L]
        tri = (si >= ti) if c['rev'] else (si <= ti)
        c['m_prev'] = c['m_ref'][0:1, 0:1]
        bc2 = jnp.concatenate([c['b_col'], c['b_col']], axis=-1)
        dmat = jnp.where(tri, bc2 + (c['log_i'] - b_row), NEG_INF)
        inter = c['b_col'] + c['m_prev']
        m_t = jnp.maximum(inter, jnp.max(dmat, axis=-1, keepdims=True))
        c['m_t'] = m_t
        c['w_inter'] = jnp.exp(inter - m_t)
        c['p'] = (jnp.exp(dmat - jnp.concatenate([m_t, m_t], axis=-1)) * c['s']).astype(BF16)
    for c in chains:
        c['v_ext'] = jnp.concatenate([c['v_ref'][:, c['sl']].astype(BF16), ones], axis=-1)
        c['cn'] = c['cn_ref'][...]
        w2 = jnp.concatenate([c['w_inter'], c['w_inter']], axis=-1)
        acc = _dot(c['p'], c['v_ext']) + w2 * _dot(c['qc'], c['cn'].astype(BF16))
        den = acc[:, HEAD_DIM:]
        c['o_ref'][:, c['sl']] = acc[:, :HEAD_DIM] / jnp.maximum(jnp.abs(den), jnp.exp(-c['m_t']))
    for c in chains:
        e_row = c['b_end'] + (c['log_i'] - c['b_row'])
        m_new = jnp.maximum(c['b_end'] + c['m_prev'], jnp.max(e_row, axis=-1, keepdims=True))
        w_old = jnp.exp(c['b_end'] + c['m_prev'] - m_new)
        w_s = jnp.exp(c['b_end'] - c['b_col'] + c['i_col'] - m_new)
        c['cn_ref'][...] = w_old * c['cn'] + lax.dot_general(
            (w_s * c['kc']).astype(BF16), c['v_ext'], (((0,), (0,)), ((), ())), preferred_element_type=F32)
        c['m_ref'][...] = jnp.broadcast_to(m_new, c['m_ref'].shape)


def _mlstm(proj, gates_t, gbias, cos, sin, tri):
    width = ML_HEADS * HEAD_DIM

    def spec(col, rev):
        return pl.BlockSpec((CHUNK, width), lambda b, s: (_chunk_block(b, s, rev), col))

    def gate_spec(rev):
        d = 1 if rev else 0
        return pl.BlockSpec((1, 2, ML_HEADS, CHUNK), lambda b, s: (d, 0, 0, _chunk_block(b, s, rev)))

    def rope_spec(rev):
        def idx(b, s):
            lat = _chunk_block(b, s, rev) - b * N_LAT_CHUNKS
            return (jnp.where(s < N_CTX_CHUNKS, N_LAT_CHUNKS, lat), 0)
        return pl.BlockSpec((CHUNK, LANES), idx)

    out_spec = lambda rev: pl.BlockSpec((CHUNK, width), lambda b, s: (_chunk_block(b, s, rev), 0))
    shape = jax.ShapeDtypeStruct((N_ALL, width), F32)
    per_dir = lambda rev: [spec(8, rev), spec(9, rev), spec(10, rev), gate_spec(rev), rope_spec(rev), rope_spec(rev)]
    args_dir = [proj, proj, proj, gates_t, cos, sin]
    return pl.pallas_call(
        _mlstm_kernel,
        grid=(BATCH, N_STEPS),
        in_specs=per_dir(False) + per_dir(True) + [
            pl.BlockSpec((2, 2, ML_HEADS, LANES), lambda b, s: (0, 0, 0, 0)),
            pl.BlockSpec((3, CHUNK, CHUNK), lambda b, s: (0, 0, 0)),
            pl.BlockSpec((2 * HEAD_DIM, HEAD_DIM), lambda b, s: (0, 0)),
        ],
        out_specs=[out_spec(False), out_spec(True)],
        out_shape=[shape, shape],
        scratch_shapes=[pltpu.VMEM((ML_HEADS, HEAD_DIM, 2 * HEAD_DIM), F32),
                        pltpu.VMEM((ML_HEADS, HEAD_DIM, 2 * HEAD_DIM), F32),
                        pltpu.VMEM((ML_HEADS, 8, LANES), F32), pltpu.VMEM((ML_HEADS, 8, LANES), F32)],
        compiler_params=_params("parallel", "arbitrary"),
        name="mlstm",
    )(*args_dir, *args_dir, gbias, tri, _rope_swap())


def _head_rms(x, w):
    parts = []
    for hh in range(x.shape[-1] // HEAD_DIM):
        xs = x[:, hh * HEAD_DIM:(hh + 1) * HEAD_DIM]
        parts.append(xs * lax.rsqrt(jnp.mean(xs * xs, axis=-1, keepdims=True) + EPS))
    return jnp.concatenate(parts, axis=-1) * w


def _merge_kernel(na_ref, nac_ref, hgf_ref, hgb_ref, mlf_ref, mlb_ref, hgg_ref, mlo_ref, bg0_ref, bg1_ref, bg2_ref,
                  hl_ref, hc_ref, mod_ref, hgw_ref, mlw_ref, wb_ref, wo_ref, o_ref):
    hg = _head_rms(hgf_ref[...] + hgb_ref[...], hgw_ref[...]) * _silu(hgg_ref[...])
    ml = _sigmoid(mlo_ref[...]) * _head_rms(mlf_ref[...] + mlb_ref[...], mlw_ref[...])
    is_ctx = pl.program_id(0) >= N_LAT // TM_MERGE
    na = jnp.where(is_ctx, nac_ref[...], na_ref[...])
    y2 = None
    for bg_ref, branch, i in ((bg0_ref, na, 0), (bg1_ref, hg, 1), (bg2_ref, ml, 2)):
        p = _dot(branch.astype(BF16), wb_ref[i])
        term = jnp.tanh(0.5 * bg_ref[...]) * p + p
        y2 = term if y2 is None else y2 + term
    h = jnp.where(is_ctx, hc_ref[...], hl_ref[...])
    o_ref[...] = h + (0.5 * mod_ref[0, 2:3, :]) * _dot(y2.astype(BF16), wo_ref[...])


def _merge(n_rows, na, na_ctx, hgf, hgb, mlf, mlb, proj, h_lat, h_ctx, ctx_row0, mod5, hg_w, ml_w, w_branch, w_out):
    tm = TM_MERGE
    tiles_per_batch = SEQ // tm
    n_lat_tiles = N_LAT // tm
    ctx_tile0 = ctx_row0 // tm
    row = lambda w, c: pl.BlockSpec((tm, w), lambda i: (i, c))
    const = lambda shape: pl.BlockSpec(shape, lambda i: (0,) * len(shape))
    bg0 = BG_OFF // D_MODEL
    return pl.pallas_call(
        _merge_kernel,
        grid=(n_rows // tm,),
        in_specs=[
            pl.BlockSpec((tm, 512), lambda i: (jnp.minimum(i, n_lat_tiles - 1), 0)),
            pl.BlockSpec((tm, 512), lambda i: (jnp.maximum(i - n_lat_tiles, 0), 0)),
            row(512, 0), row(512, 0), row(512, 0), row(512, 0),
            row(512, 7), row(512, 11),
            row(D_MODEL, bg0), row(D_MODEL, bg0 + 1), row(D_MODEL, bg0 + 2),
            pl.BlockSpec((tm, D_MODEL), lambda i: (jnp.minimum(i, n_lat_tiles - 1), 0)),
            pl.BlockSpec((tm, D_MODEL), lambda i: (ctx_tile0 + jnp.maximum(i - n_lat_tiles, 0), 0)),
            pl.BlockSpec((1, 6, D_MODEL), lambda i: (jnp.minimum(i // tiles_per_batch, BATCH), 0, 0)),
            const((1, 512)), const((1, 512)),
            const((3, BRANCH_WIDTH, D_MODEL)), const((D_MODEL, D_MODEL)),
        ],
        out_specs=row(D_MODEL, 0),
        out_shape=jax.ShapeDtypeStruct((n_rows, D_MODEL), F32),
        compiler_params=_params("parallel"),
        name="merge",
    )(na, na_ctx, hgf, hgb, mlf, mlb, proj, proj, proj, proj, proj, h_lat, h_ctx, mod5, hg_w, ml_w, w_branch, w_out)


def _ffn_kernel(h_ref, nw_ref, mod_ref, wa_ref, wu_ref, wd_ref, o_ref):
    h = h_ref[...]
    f = _norm_mod(h, nw_ref[...], mod_ref[0, 3:4, :], mod_ref[0, 4:5, :]).astype(BF16)
    g = _silu(_dot(f, wa_ref[...])) * _dot(f, wu_ref[...])
    o_ref[...] = h + mod_ref[0, 5:6, :] * _dot(g.astype(BF16), wd_ref[...])


def _ffn(h_all, nw, mod5, w_up, w_down):
    n_rows = h_all.shape[0]
    tiles_per_batch = SEQ // TM_FFN
    resident = pl.Buffered(1)
    return pl.pallas_call(
        _ffn_kernel,
        grid=(n_rows // TM_FFN,),
        in_specs=[
            pl.BlockSpec((TM_FFN, D_MODEL), lambda i: (i, 0)),
            pl.BlockSpec((1, D_MODEL), lambda i: (0, 0)),
            pl.BlockSpec((1, 6, D_MODEL), lambda i: (i // tiles_per_batch, 0, 0)),
            pl.BlockSpec((D_MODEL, FFN_DIM), lambda i: (0, 0), pipeline_mode=resident),
            pl.BlockSpec((D_MODEL, FFN_DIM), lambda i: (0, 1), pipeline_mode=resident),
            pl.BlockSpec((FFN_DIM, D_MODEL), lambda i: (0, 0), pipeline_mode=resident),
        ],
        out_specs=pl.BlockSpec((TM_FFN, D_MODEL), lambda i: (i, 0)),
        out_shape=jax.ShapeDtypeStruct((n_rows, D_MODEL), F32),
        compiler_params=_params("parallel"),
        name="ffn",
    )(h_all, nw, mod5, w_up, w_up, w_down)


def _router_kernel(h_ref, nw_ref, mod_ref, wr_ref, f_ref, r_ref):
    f = _norm_mod(h_ref[...], nw_ref[...], mod_ref[0, 3:4, :], mod_ref[0, 4:5, :])
    f_ref[...] = f
    logits = jnp.dot(f, wr_ref[...], preferred_element_type=F32, precision=lax.Precision.HIGHEST)
    lane = lax.broadcasted_iota(jnp.int32, logits.shape, 1)
    logits = jnp.where(lane < N_EXPERTS, logits, -jnp.inf)
    m1 = jnp.max(logits, axis=-1, keepdims=True)
    i1 = jnp.min(jnp.where(logits == m1, lane, LANES), axis=-1, keepdims=True)
    rest = jnp.where(lane == i1, -jnp.inf, logits)
    m2 = jnp.max(rest, axis=-1, keepdims=True)
    i2 = jnp.min(jnp.where(rest == m2, lane, LANES), axis=-1, keepdims=True)
    e2 = jnp.exp(m2 - m1)
    w1 = 1.0 / (1.0 + e2)
    w2 = e2 / (1.0 + e2)
    r_ref[...] = jnp.where(lane == 0, i1.astype(F32),
                           jnp.where(lane == 1, i2.astype(F32),
                                     jnp.where(lane == 2, w1, jnp.where(lane == 3, w2, 0.0))))


def _router(h_lat, nw, mod5, w_router_pad):
    tm = 512
    tiles_per_batch = SEQ // tm
    return pl.pallas_call(
        _router_kernel,
        grid=(N_LAT // tm,),
        in_specs=[
            pl.BlockSpec((tm, D_MODEL), lambda i: (i, 0)),
            pl.BlockSpec((1, D_MODEL), lambda i: (0, 0)),
            pl.BlockSpec((1, 6, D_MODEL), lambda i: (i // tiles_per_batch, 0, 0)),
            pl.BlockSpec((D_MODEL, LANES), lambda i: (0, 0)),
        ],
        out_specs=[pl.BlockSpec((tm, D_MODEL), lambda i: (i, 0)), pl.BlockSpec((tm, LANES), lambda i: (i, 0))],
        out_shape=[jax.ShapeDtypeStruct((N_LAT, D_MODEL), F32), jax.ShapeDtypeStruct((N_LAT, LANES), F32)],
        compiler_params=_params("parallel"),
        name="router",
    )(h_lat, nw, mod5, w_router_pad)


def _moe_kernel(be_ref, nused_ref, nvalid_ref, code_ref, f_hbm, wa_ref, wu_ref, wd_ref, y_hbm,
                xbuf, x16, acc, ybuf, sem_in, sem_out):
    i = pl.program_id(0)
    j = pl.program_id(1)
    last_j = pl.num_programs(1) - 1
    n_used = nused_ref[0]
    active = i < n_used
    slot = i % 2

    def start_gather(blk, buf):
        def body(r, c):
            tok = jnp.maximum(code_ref[blk * TM_MOE + r], 0) >> 1
            pltpu.make_async_copy(f_hbm.at[pl.ds(tok, 1)], xbuf.at[buf, pl.ds(r, 1)],
                                  sem_in.at[buf]).start(priority=GATHER_DMA_PRIORITY)
            return c
        lax.fori_loop(0, TM_MOE, body, 0, unroll=8)

    def wait_gather(buf):
        pltpu.make_async_copy(f_hbm.at[pl.ds(0, TM_MOE)], xbuf.at[buf], sem_in.at[buf]).wait()

    def start_scatter(blk):
        def body(r, c):
            code = code_ref[blk * TM_MOE + r]
            dst = (code & 1) * N_LAT + (code >> 1)
            pltpu.make_async_copy(ybuf.at[pl.ds(r, 1)], y_hbm.at[pl.ds(dst, 1)], sem_out).start()
            return c
        lax.fori_loop(0, nvalid_ref[blk], body, 0)

    def wait_scatter(blk):
        n = nvalid_ref[blk]
        p = TM_MOE
        while p >= 8:
            @pl.when((n & p) != 0)
            def _(p=p):
                pltpu.make_async_copy(ybuf.at[pl.ds(0, p)], y_hbm.at[pl.ds(0, p)], sem_out).wait()
            p //= 2

        def one(r, c):
            pltpu.make_async_copy(ybuf.at[pl.ds(0, 1)], y_hbm.at[pl.ds(0, 1)], sem_out).wait()
            return c
        lax.fori_loop(0, n & 7, one, 0)

    def prefetch_rows(first, count):
        for r in range(first, first + count):
            tok = jnp.maximum(code_ref[(i + 1) * TM_MOE + r], 0) >> 1
            pltpu.make_async_copy(f_hbm.at[pl.ds(tok, 1)], xbuf.at[1 - slot, pl.ds(r, 1)],
                                  sem_in.at[1 - slot]).start(priority=GATHER_DMA_PRIORITY)

    def partial_out():
        x = x16[...]
        g = _silu(_dot(x, wa_ref[0])) * _dot(x, wu_ref[0])
        return _dot(g.astype(BF16), wd_ref[0])

    @pl.when(jnp.logical_and(active, j == 0))
    def _():
        @pl.when(i == 0)
        def _():
            start_gather(0, 0)

        wait_gather(slot)
        x16[...] = xbuf[slot].astype(BF16)

    @pl.when(jnp.logical_and(active, j == 0))
    def _():
        prefetch_rows(0, TM_MOE // 2)
        acc[...] = partial_out()

    @pl.when(jnp.logical_and(active, j == last_j))
    def _():
        @pl.when(i > 0)
        def _():
            wait_scatter(i - 1)

    @pl.when(jnp.logical_and(active, j == last_j))
    def _():
        prefetch_rows(TM_MOE // 2, TM_MOE // 2)
        ybuf[...] = acc[...] + partial_out()

    @pl.when(jnp.logical_and(active, j == last_j))
    def _():
        start_scatter(i)

        @pl.when(i == n_used - 1)
        def _():
            wait_scatter(i)
            wait_gather(1 - slot)


def _moe_experts(block_e, n_used, n_valid, codes, f_lat, w_up, w_down):
    nj = EXPERT_DIM // TH_MOE
    assert nj == 2 and nj * TH_MOE == EXPERT_DIM

    def jj(i, j, nu):
        return jnp.where(i < nu[0], j, nj - 1)

    grid_spec = pltpu.PrefetchScalarGridSpec(
        num_scalar_prefetch=4,
        grid=(N_MOE_BLOCKS, nj),
        in_specs=[
            pl.BlockSpec(memory_space=pl.ANY),
            pl.BlockSpec((1, D_MODEL, TH_MOE), lambda i, j, be, nu, nv, cd: (be[i], 0, jj(i, j, nu))),
            pl.BlockSpec((1, D_MODEL, TH_MOE), lambda i, j, be, nu, nv, cd: (be[i], 0, nj + jj(i, j, nu))),
            pl.BlockSpec((1, TH_MOE, D_MODEL), lambda i, j, be, nu, nv, cd: (be[i], jj(i, j, nu), 0)),
        ],
        out_specs=pl.BlockSpec(memory_space=pl.ANY),
        scratch_shapes=[
            pltpu.VMEM((2, TM_MOE, D_MODEL), F32), pltpu.VMEM((TM_MOE, D_MODEL), BF16),
            pltpu.VMEM((TM_MOE, D_MODEL), F32), pltpu.VMEM((TM_MOE, D_MODEL), F32),
            pltpu.SemaphoreType.DMA((2,)), pltpu.SemaphoreType.DMA(()),
        ],
    )
    return pl.pallas_call(
        _moe_kernel,
        grid_spec=grid_spec,
        out_shape=jax.ShapeDtypeStruct((2 * N_LAT, D_MODEL), F32),
        compiler_params=_params("arbitrary", "arbitrary"),
        name="moe_experts",
    )(block_e, n_used, n_valid, codes, f_lat, w_up, w_up, w_down)


def _combine_kernel(h_ref, y1_ref, y2_ref, r_ref, mod_ref, fw_ref, o_ref):
    r = r_ref[...]
    y = r[:, 2:3] * y1_ref[...] + r[:, 3:4] * y2_ref[...]
    h = h_ref[...] + mod_ref[0, 5:6, :] * y
    o_ref[...] = h * lax.rsqrt(jnp.mean(h * h, axis=-1, keepdims=True) + EPS) * fw_ref[...]


def _combine_final(h_lat, y, route, mod5, final_w):
    tm = 512
    tiles_per_batch = SEQ // tm
    return pl.pallas_call(
        _combine_kernel,
        grid=(N_LAT // tm,),
        in_specs=[
            pl.BlockSpec((tm, D_MODEL), lambda i: (i, 0)),
            pl.BlockSpec((tm, D_MODEL), lambda i: (i, 0)),
            pl.BlockSpec((tm, D_MODEL), lambda i: (N_LAT // tm + i, 0)),
            pl.BlockSpec((tm, LANES), lambda i: (i, 0)),
            pl.BlockSpec((1, 6, D_MODEL), lambda i: (i // tiles_per_batch, 0, 0)),
            pl.BlockSpec((1, D_MODEL), lambda i: (0, 0)),
        ],
        out_specs=pl.BlockSpec((tm, D_MODEL), lambda i: (i, 0)),
        out_shape=jax.ShapeDtypeStruct((N_LAT, D_MODEL), F32),
        compiler_params=_params("parallel"),
        name="combine_final",
    )(h_lat, y, y, route, mod5, final_w)


def _moe_plan(route):
    e12 = route[:, 0:2].astype(jnp.int32)
    onehot = (e12[:, :, None] == jnp.arange(N_EXPERTS, dtype=jnp.int32)).astype(jnp.int32).sum(axis=1)
    before = jnp.cumsum(onehot, axis=0) - onehot
    counts = jnp.sum(onehot, axis=0)
    nblk = (counts + TM_MOE - 1) // TM_MOE
    blk_end = jnp.cumsum(nblk)
    slot0 = (blk_end - nblk) * TM_MOE
    rank = jnp.take_along_axis(before, e12, axis=1)
    dest = slot0[e12] + rank
    codes = jnp.full((N_SLOTS,), -1, jnp.int32).at[dest.reshape(-1)].set(jnp.arange(2 * N_LAT, dtype=jnp.int32))
    n_used = blk_end[-1]
    blocks = jnp.minimum(jnp.arange(N_MOE_BLOCKS, dtype=jnp.int32), n_used - 1)
    block_e = jnp.minimum(jnp.sum((blocks[:, None] >= blk_end[None, :]).astype(jnp.int32), axis=1), N_EXPERTS - 1)
    n_valid = jnp.sum((codes >= 0).astype(jnp.int32).reshape(N_MOE_BLOCKS, TM_MOE), axis=1)
    return block_e, n_used.reshape(1).astype(jnp.int32), n_valid, codes


def kernel(x, c, ctx, c_ctx, mod_w, mod_b, norm1_w, w_in, na_rpb, hg_lb, hg_norm_w, ml_gate_b, ml_norm_w,
           w_branch, w_out, norm2_w, ffn_w_up, ffn_w_down, moe_router, moe_w_up, moe_w_down, final_norm_w):
    h_lat, h_ctx, ctx_row0 = x.reshape(N_LAT, D_MODEL), ctx.reshape(N_CTX, D_MODEL), 0
    c8 = jnp.concatenate([c, c_ctx[None, :], jnp.zeros((3, D_MODEL), F32)], axis=0)
    mods = _modulation(c8, mod_w, mod_b).reshape(DEPTH, 8, 6, D_MODEL)

    lb_p = jax.nn.softmax(hg_lb.astype(F32), axis=0)
    hg_lower = jnp.cumsum(lb_p, axis=0) - lb_p[0]
    tri = _tri_consts()
    hg_tri, hg_masks = _hgrn_consts()
    cos, sin = _rope_tables()

    out = None
    for layer in range(DEPTH):
        last = layer == DEPTH - 1
        mod5 = mods[layer, :5]
        wl = w_in[layer]
        w_main = jnp.concatenate([wl[:, :MAIN_W], wl[:, MAIN_W + ML_GATE_COLS:]], axis=1).astype(BF16)
        w_gates = jnp.pad(wl[:, MAIN_W:MAIN_W + ML_GATE_COLS], ((0, 0), (0, LANES - ML_GATE_COLS))).astype(BF16)
        proj, gates = _inproj(h_lat, h_ctx, ctx_row0, norm1_w[layer][None, :], mod5, w_main, w_gates)

        na = _na_latent(proj, _na_bias_tables(na_rpb[layer]))
        na_ctx = na if last else _ctx_attention(proj)
        hgf, hgb = _hgrn2(proj, hg_lower[layer], hg_tri, hg_masks)
        gates_t = gates[:, :ML_GATE_COLS].T.reshape(2, 2, ML_HEADS, N_ALL)
        gbias = jnp.broadcast_to(ml_gate_b[layer][..., None], (2, 2, ML_HEADS, LANES))
        mlf, mlb = _mlstm(proj, gates_t, gbias, cos, sin, tri)

        n_rows = N_LAT if last else N_ALL
        h_all = _merge(n_rows, na, na_ctx, hgf, hgb, mlf, mlb, proj, h_lat, h_ctx, ctx_row0, mod5,
                       hg_norm_w[layer][None, :], ml_norm_w[layer][None, :],
                       w_branch[layer].astype(BF16), w_out[layer].astype(BF16))
        i = layer // 2
        if layer % 2 == 0:
            h_all = _ffn(h_all, norm2_w[layer][None, :], mod5, ffn_w_up[i].astype(BF16), ffn_w_down[i].astype(BF16))
            h_lat, h_ctx, ctx_row0 = h_all, h_all, N_LAT
            if last:
                raise NotImplementedError("final norm after a dense last layer")
        else:
            if not last:
                raise NotImplementedError("MoE on the context stream")
            w_router_pad = jnp.pad(moe_router[i], ((0, 0), (0, LANES - N_EXPERTS)))
            f_lat, route = _router(h_all, norm2_w[layer][None, :], mod5, w_router_pad)
            block_e, n_used, n_valid, codes = _moe_plan(route)
            y = _moe_experts(block_e, n_used, n_valid, codes, f_lat,
                             moe_w_up[i].astype(BF16), moe_w_down[i].astype(BF16))
            out = _combine_final(h_all, y, route, mod5, final_norm_w[None, :])
    return out.reshape(BATCH, SEQ, D_MODEL)
```
